```python
import math
import jax, jax.numpy as jnp
from jax import lax
import numpy as np

D_MODEL = 4096
BATCH = 4
SEQ = 2048
DEPTH = 2
DEC_BATCH = 128
DEC_SEQ = 1
PAST_LEN = 16384
PAGE_SIZE = 128

N_MIXERS = 2
N_RWKV_LAYERS = (DEPTH + 1) // 2
N_GDN_LAYERS = DEPTH // 2

RWKV_HEAD = 64
RWKV_HEADS = D_MODEL // RWKV_HEAD
RWKV_DECAY_LORA = max(32, round(1.8 * D_MODEL ** 0.5 / 32) * 32)
RWKV_AAA_LORA = max(32, round(1.8 * D_MODEL ** 0.5 / 32) * 32)
RWKV_GATE_LORA = max(32, round(0.6 * D_MODEL ** 0.8 / 32) * 32)
RWKV_LN_EPS = 64e-5

GDN_DK = 128
GDN_DV = 128
GDN_HEADS = D_MODEL // 128
GDN_CONV = 4
GDN_CHUNK = 64
GDN_QK = GDN_HEADS * GDN_DK
GDN_V = GDN_HEADS * GDN_DV
GDN_CONV_DIM = 2 * GDN_QK + GDN_V
GDN_IN = GDN_CONV_DIM + GDN_V + 2 * GDN_HEADS

D_FF = ((8 * D_MODEL // 3 + 255) // 256) * 256
RMS_EPS = 1e-6

kernel_name = "rwkv7_gated_deltanet_adaln_hybrid_step"


def _normal(key, shape, scale):
    return jax.random.normal(key, shape, jnp.float32) * scale


def rmsnorm(x, w):
    xf = x.astype(jnp.float32)
    y = xf * lax.rsqrt(jnp.mean(xf * xf, axis=-1, keepdims=True) + RMS_EPS)
    return (y * w.astype(jnp.float32)).astype(x.dtype)


def l2norm(x):
    xf = x.astype(jnp.float32)
    return xf * lax.rsqrt(jnp.sum(xf * xf, axis=-1, keepdims=True) + 1e-6)


def adaln(c, w, b):
    m = jax.nn.silu(c) @ w + b
    return m.reshape(c.shape[0], 6, 1, D_MODEL)


def swiglu(h, w_gate, w_up, w_down):
    return (jax.nn.silu(h @ w_gate) * (h @ w_up)) @ w_down


def rwkv7_time_mix(h, shift_prev, S0, mix, w_rkv, w_o, w0, w1, w2, a0, a1, a2, g1, g2, k_k, k_a, r_k, ln_w, ln_b):
    B, T, D = h.shape
    H, N = RWKV_HEADS, RWKV_HEAD
    f32 = jnp.float32
    h_prev = jnp.concatenate([shift_prev[:, None, :].astype(h.dtype), h[:, :-1]], axis=1)
    xx = h_prev - h
    x_r, x_k, x_v, x_w, x_a, x_g = (h + xx * mix[i] for i in range(6))
    r = x_r @ w_rkv[0]
    k = x_k @ w_rkv[1]
    v = x_v @ w_rkv[2]
    w_log = -jax.nn.softplus(-(w0 + jnp.tanh(x_w @ w1) @ w2).astype(f32)) - 0.5
    decay = jnp.exp(-jnp.exp(w_log))
    a = jax.nn.sigmoid((a0 + (x_a @ a1) @ a2).astype(f32))
    g = jax.nn.sigmoid(x_g @ g1) @ g2
    heads = lambda t: t.astype(f32).reshape(B, T, H, N)
    kk = l2norm(heads(k * k_k))
    k_mod = k.astype(f32) * (1.0 + (a - 1.0) * k_a.astype(f32))
    r_h, k_h, v_h, w_h, a_h = heads(r), heads(k_mod), heads(v), heads(decay), heads(a)
    b_h = kk * a_h

    def step(S, inp):
        r_t, w_t, k_t, v_t, kk_t, b_t = inp
        sa = jnp.einsum('bhvk,bhk->bhv', S, -kk_t)
        S = S * w_t[:, :, None, :] + sa[..., None] * b_t[:, :, None, :] + v_t[..., None] * k_t[:, :, None, :]
        return S, jnp.einsum('bhvk,bhk->bhv', S, r_t)

    seq_first = lambda t: jnp.swapaxes(t, 0, 1)
    S_T, ys = lax.scan(step, S0.astype(f32), tuple(seq_first(t) for t in (r_h, w_h, k_h, v_h, kk, b_h)))
    y = seq_first(ys)
    mu = jnp.mean(y, axis=-1, keepdims=True)
    var = jnp.mean(jnp.square(y - mu), axis=-1, keepdims=True)
    y = ((y - mu) * lax.rsqrt(var + RWKV_LN_EPS)).reshape(B, T, D) * ln_w.astype(f32) + ln_b.astype(f32)
    bonus = jnp.sum(r_h * k_h * r_k.astype(f32), axis=-1, keepdims=True) * v_h
    y = y + bonus.reshape(B, T, D)
    out = (y * g.astype(f32)).astype(h.dtype) @ w_o
    return out, S_T, h[:, -1]


def causal_depthwise_conv(x, buf, w):
    T = x.shape[1]
    xp = jnp.concatenate([buf.astype(x.dtype), x], axis=1)
    y = xp[:, 0:T] * w[0]
    for j in range(1, GDN_CONV):
        y = y + xp[:, j:j + T] * w[j]
    return y, xp[:, T:]


def chunked_gated_delta_rule(q, k, v, g, beta, S0):
    B, T, H, DK = q.shape
    C = GDN_CHUNK
    pad = (-T) % C
    padt = lambda t: jnp.pad(t, [(0, 0), (0, pad)] + [(0, 0)] * (t.ndim - 2))
    q, k, v, g, beta = padt(q), padt(k), padt(v), padt(g), padt(beta)
    n_chunks = (T + pad) // C

    def to_chunks(t):
        t = t.reshape((B, n_chunks, C) + t.shape[2:])
        return t.transpose((1, 0, 3, 2) + tuple(range(4, t.ndim)))

    q, k, v, g, beta = to_chunks(q), to_chunks(k), to_chunks(v), to_chunks(g), to_chunks(beta)
    gc = jnp.cumsum(g, axis=-1)
    idx = jnp.arange(C)
    causal = idx[:, None] >= idx[None, :]
    strict = idx[:, None] > idx[None, :]
    diff = gc[..., :, None] - gc[..., None, :]
    decay = jnp.where(causal, jnp.exp(jnp.where(causal, diff, 0.0)), 0.0)
    kb = k * beta[..., None]
    L = jnp.where(strict, jnp.einsum('nbhid,nbhjd->nbhij', kb, k) * decay, 0.0)
    rhs = jnp.concatenate([v * beta[..., None], kb * jnp.exp(gc)[..., None]], axis=-1)
    sol = lax.linalg.triangular_solve(L, rhs, left_side=True, lower=True, unit_diagonal=True)
    dv = v.shape[-1]
    u, w = sol[..., :dv], sol[..., dv:]
    attn = jnp.einsum('nbhid,nbhjd->nbhij', q, k) * decay
    q_dec = q * jnp.exp(gc)[..., None]
    k_dec = k * jnp.exp(gc[..., -1:] - gc)[..., None]
    g_last = jnp.exp(gc[..., -1])

    def step(S, inp):
        u_c, w_c, attn_c, qd_c, kd_c, gl_c = inp
        v_new = u_c - jnp.einsum('bhcd,bhde->bhce', w_c, S)
        o = jnp.einsum('bhcd,bhde->bhce', qd_c, S) + jnp.einsum('bhij,bhje->bhie', attn_c, v_new)
        S = S * gl_c[..., None, None] + jnp.einsum('bhcd,bhce->bhde', kd_c, v_new)
        return S, o

    S_T, o = lax.scan(step, S0, (u, w, attn, q_dec, k_dec, g_last))
    o = o.transpose(1, 0, 3, 2, 4).reshape(B, n_chunks * C, H, dv)[:, :T]
    return o, S_T


def gated_deltanet_mix(h, conv_prev, S0, w_in, conv_w, A_log, dt_bias, norm_w, w_out):
    B, T, _ = h.shape
    H, DK, DV = GDN_HEADS, GDN_DK, GDN_DV
    f32 = jnp.float32
    proj = h @ w_in
    qkv = proj[..., :GDN_CONV_DIM]
    z = proj[..., GDN_CONV_DIM:GDN_CONV_DIM + GDN_V].reshape(B, T, H, DV)
    b_in = proj[..., GDN_CONV_DIM + GDN_V:GDN_CONV_DIM + GDN_V + H]
    a_in = proj[..., GDN_CONV_DIM + GDN_V + H:]
    conv, conv_new = causal_depthwise_conv(qkv, conv_prev, conv_w)
    qkv = jax.nn.silu(conv).astype(f32)
    q = l2norm(qkv[..., :GDN_QK].reshape(B, T, H, DK)) * (DK ** -0.5)
    k = l2norm(qkv[..., GDN_QK:2 * GDN_QK].reshape(B, T, H, DK))
    v = qkv[..., 2 * GDN_QK:].reshape(B, T, H, DV)
    beta = jax.nn.sigmoid(b_in.astype(f32))
    g = -jnp.exp(A_log.astype(f32)) * jax.nn.softplus(a_in.astype(f32) + dt_bias.astype(f32))
    o, S_T = chunked_gated_delta_rule(q, k, v, g, beta, S0.astype(f32))
    o = o * lax.rsqrt(jnp.mean(o * o, axis=-1, keepdims=True) + RMS_EPS) * norm_w.astype(f32)
    o = o * jax.nn.silu(z.astype(f32))
    out = o.reshape(B, T, GDN_V).astype(h.dtype) @ w_out
    return out, S_T, conv_new


def trunk(x, c, rw_S, rw_shift, gd_S, gd_conv, P):
    new_rw_S, new_rw_shift, new_gd_S, new_gd_conv = [], [], [], []
    for layer in range(DEPTH):
        mod = adaln(c, P['w_ada'][layer], P['b_ada'][layer])
        h = rmsnorm(x, P['norm1_w'][layer]) * (1.0 + mod[:, 1]) + mod[:, 0]
        j = layer // N_MIXERS
        if layer % N_MIXERS == 0:
            out, s_new, sh_new = rwkv7_time_mix(
                h, rw_shift[j], rw_S[j], P['rw_mix'][j], P['rw_w_rkv'][j], P['rw_w_o'][j],
                P['rw_w0'][j], P['rw_w1'][j], P['rw_w2'][j], P['rw_a0'][j], P['rw_a1'][j], P['rw_a2'][j],
                P['rw_g1'][j], P['rw_g2'][j], P['rw_k_k'][j], P['rw_k_a'][j], P['rw_r_k'][j],
                P['rw_ln_w'][j], P['rw_ln_b'][j])
            new_rw_S.append(s_new)
            new_rw_shift.append(sh_new)
        else:
            out, s_new, cb_new = gated_deltanet_mix(
                h, gd_conv[j], gd_S[j], P['gd_w_in'][j], P['gd_conv_w'][j], P['gd_A_log'][j],
                P['gd_dt_bias'][j], P['gd_norm_w'][j], P['gd_w_out'][j])
            new_gd_S.append(s_new)
            new_gd_conv.append(cb_new)
        x = x + (mod[:, 2] * out).astype(x.dtype)
        h = rmsnorm(x, P['norm2_w'][layer]) * (1.0 + mod[:, 4]) + mod[:, 3]
        ffn = swiglu(h, P['ffn_w_gate'][layer], P['ffn_w_up'][layer], P['ffn_w_down'][layer])
        x = x + (mod[:, 5] * ffn).astype(x.dtype)
    y = rmsnorm(x, P['final_norm_w'])
    return y, jnp.stack(new_rw_S), jnp.stack(new_rw_shift), jnp.stack(new_gd_S), jnp.stack(new_gd_conv)


def setup_inputs(seed: int = 0) -> dict:
    key = jax.random.key(seed)
    k = jax.random.split(key, 40)
    D, F = D_MODEL, D_FF
    NR, NG = N_RWKV_LAYERS, N_GDN_LAYERS
    H, N = RWKV_HEADS, RWKV_HEAD
    GH, DK, DV = GDN_HEADS, GDN_DK, GDN_DV
    inv = lambda n: n ** -0.5
    dt = jnp.exp(jax.random.uniform(k[30], (NG, GH), jnp.float32, math.log(1e-3), math.log(1e-1)))
    return {
        "x_prompt": _normal(k[0], (BATCH, SEQ, D), 1.0),
        "x_sample": _normal(k[1], (DEC_BATCH, DEC_SEQ, D), 1.0),
        "state_rwkv": _normal(k[2], (NR, DEC_BATCH, H, N, N), 0.3),
        "state_rwkv_shift": _normal(k[3], (NR, DEC_BATCH, D), 1.0),
        "state_gdn": _normal(k[4], (NG, DEC_BATCH, GH, DK, DV), 0.1),
        "state_gdn_conv": _normal(k[5], (NG, DEC_BATCH, GDN_CONV - 1, GDN_CONV_DIM), 1.0),
        "c_prompt": _normal(k[6], (BATCH, D), 1.0),
        "c_sample": _normal(k[7], (DEC_BATCH, D), 1.0),
        "w_ada": _normal(k[8], (DEPTH, D, 6 * D), inv(D)),
        "b_ada": _normal(k[9], (DEPTH, 6 * D), 0.01),
        "norm1_w": 1.0 + _normal(k[10], (DEPTH, D), 0.01),
        "norm2_w": 1.0 + _normal(k[11], (DEPTH, D), 0.01),
        "rw_mix": jax.random.uniform(k[12], (NR, 6, D), jnp.float32),
        "rw_w_rkv": _normal(k[13], (NR, 3, D, D), inv(D)),
        "rw_w0": jax.random.uniform(k[14], (NR, D), jnp.float32, -6.0, -1.0),
        "rw_w1": _normal(k[15], (NR, D, RWKV_DECAY_LORA), inv(D)),
        "rw_w2": _normal(k[16], (NR, RWKV_DECAY_LORA, D), 0.1 * inv(RWKV_DECAY_LORA)),
        "rw_a0": _normal(k[17], (NR, D), 0.1),
        "rw_a1": _normal(k[18], (NR, D, RWKV_AAA_LORA), inv(D)),
        "rw_a2": _normal(k[19], (NR, RWKV_AAA_LORA, D), inv(RWKV_AAA_LORA)),
        "rw_g1": _normal(k[20], (NR, D, RWKV_GATE_LORA), inv(D)),
        "rw_g2": _normal(k[21], (NR, RWKV_GATE_LORA, D), inv(RWKV_GATE_LORA)),
        "rw_k_k": 0.85 + _normal(k[22], (NR, D), 0.02),
        "rw_k_a": 1.0 + _normal(k[23], (NR, D), 0.02),
        "rw_r_k": -0.04 + _normal(k[24], (NR, H, N), 0.02),
        "rw_ln_w": 1.0 + _normal(k[25], (NR, D), 0.01),
        "rw_ln_b": _normal(k[26], (NR, D), 0.01),
        "rw_w_o": _normal(k[27], (NR, D, D), inv(D)),
        "gd_w_in": _normal(k[28], (NG, D, GDN_IN), inv(D)),
        "gd_conv_w": _normal(k[29], (NG, GDN_CONV, GDN_CONV_DIM), inv(GDN_CONV)),
        "gd_A_log": jnp.log(jax.random.uniform(k[31], (NG, GH), jnp.float32, 1.0, 16.0)),
        "gd_dt_bias": dt + jnp.log(-jnp.expm1(-dt)),
        "gd_norm_w": 1.0 + _normal(k[32], (NG, DV), 0.01),
        "gd_w_out": _normal(k[33], (NG, GDN_V, D), inv(GDN_V)),
        "ffn_w_gate": _normal(k[34], (DEPTH, D, F), inv(D)),
        "ffn_w_up": _normal(k[35], (DEPTH, D, F), inv(D)),
        "ffn_w_down": _normal(k[36], (DEPTH, F, D), inv(F)),
        "final_norm_w": 1.0 + _normal(k[37], (D,), 0.01),
    }


def reference(x_prompt, x_sample, state_rwkv, state_rwkv_shift, state_gdn, state_gdn_conv, c_prompt, c_sample,
              w_ada, b_ada, norm1_w, norm2_w, rw_mix, rw_w_rkv, rw_w0, rw_w1, rw_w2, rw_a0, rw_a1, rw_a2,
              rw_g1, rw_g2, rw_k_k, rw_k_a, rw_r_k, rw_ln_w, rw_ln_b, rw_w_o, gd_w_in, gd_conv_w, gd_A_log,
              gd_dt_bias, gd_norm_w, gd_w_out, ffn_w_gate, ffn_w_up, ffn_w_down, final_norm_w):
    P = {
        'w_ada': w_ada, 'b_ada': b_ada, 'norm1_w': norm1_w, 'norm2_w': norm2_w,
        'rw_mix': rw_mix, 'rw_w_rkv': rw_w_rkv, 'rw_w0': rw_w0, 'rw_w1': rw_w1, 'rw_w2': rw_w2,
        'rw_a0': rw_a0, 'rw_a1': rw_a1, 'rw_a2': rw_a2, 'rw_g1': rw_g1, 'rw_g2': rw_g2,
        'rw_k_k': rw_k_k, 'rw_k_a': rw_k_a, 'rw_r_k': rw_r_k, 'rw_ln_w': rw_ln_w, 'rw_ln_b': rw_ln_b,
        'rw_w_o': rw_w_o, 'gd_w_in': gd_w_in, 'gd_conv_w': gd_conv_w, 'gd_A_log': gd_A_log,
        'gd_dt_bias': gd_dt_bias, 'gd_norm_w': gd_norm_w, 'gd_w_out': gd_w_out,
        'ffn_w_gate': ffn_w_gate, 'ffn_w_up': ffn_w_up, 'ffn_w_down': ffn_w_down,
        'final_norm_w': final_norm_w,
    }
    f32 = jnp.float32
    bp = x_prompt.shape[0]
    zero_rw = jnp.zeros((N_RWKV_LAYERS, bp, RWKV_HEADS, RWKV_HEAD, RWKV_HEAD), f32)
    zero_shift = jnp.zeros((N_RWKV_LAYERS, bp, D_MODEL), x_prompt.dtype)
    zero_gd = jnp.zeros((N_GDN_LAYERS, bp, GDN_HEADS, GDN_DK, GDN_DV), f32)
    zero_conv = jnp.zeros((N_GDN_LAYERS, bp, GDN_CONV - 1, GDN_CONV_DIM), x_prompt.dtype)
    y_prompt, p_rwkv, p_rwkv_shift, p_gdn, p_gdn_conv = trunk(
        x_prompt, c_prompt, zero_rw, zero_shift, zero_gd, zero_conv, P)
    y_sample, s_rwkv, s_rwkv_shift, s_gdn, s_gdn_conv = trunk(
        x_sample, c_sample, state_rwkv, state_rwkv_shift, state_gdn, state_gdn_conv, P)
    return (y_prompt, y_sample, p_rwkv, p_rwkv_shift, p_gdn, p_gdn_conv, s_rwkv, s_rwkv_shift, s_gdn, s_gdn_conv)
```

```python
import functools

import jax
import jax.numpy as jnp
from jax import lax
from jax.experimental import pallas as pl
from jax.experimental.pallas import tpu as pltpu

F32, BF16 = jnp.float32, jnp.bfloat16

RMS_EPS = 1e-6
RWKV_LN_EPS = 64e-5
RWKV_N = 64
GDN_DK = 128
GDN_CONV = 4
CHUNK = 64
LANES = 128
VMEM_LIMIT = 50 * 2**20


def _cparams(*sem):
    return pltpu.CompilerParams(dimension_semantics=sem, vmem_limit_bytes=VMEM_LIMIT)


def _pick_tile(n, cap, mult):
    best = None
    for t in range(mult, min(n, cap) + 1, mult):
        if n % t == 0:
            best = t
    return best or n


def _dot(a, b):
    return jnp.dot(a.astype(BF16), b.astype(BF16), preferred_element_type=F32)


def _dot_nt(a, b):
    return lax.dot_general(a.astype(BF16), b.astype(BF16), (((1,), (1,)), ((), ())),
                           preferred_element_type=F32)


def _dot_tn(a, b):
    return lax.dot_general(a.astype(BF16), b.astype(BF16), (((0,), (0,)), ((), ())),
                           preferred_element_type=F32)


def _split3(x):
    hi = x.astype(BF16)
    r1 = x - hi.astype(F32)
    mid = r1.astype(BF16)
    lo = (r1 - mid.astype(F32)).astype(BF16)
    return hi, mid, lo


def _tri(n, transpose=False):
    row = lax.broadcasted_iota(jnp.int32, (n, n), 0)
    col = lax.broadcasted_iota(jnp.int32, (n, n), 1)
    m = (row <= col) if transpose else (row >= col)
    return jnp.where(m, 1.0, 0.0).astype(BF16)


def _row_to_col(row):
    n = row.shape[1]
    eye = lax.broadcasted_iota(jnp.int32, (n, n), 0) == lax.broadcasted_iota(jnp.int32, (n, n), 1)
    return jnp.sum(jnp.where(eye, jnp.broadcast_to(row, (n, n)), 0.0), axis=1, keepdims=True)


def _col_to_row(col):
    n = col.shape[0]
    eye = lax.broadcasted_iota(jnp.int32, (n, n), 0) == lax.broadcasted_iota(jnp.int32, (n, n), 1)
    return jnp.sum(jnp.where(eye, jnp.broadcast_to(col, (n, n)), 0.0), axis=0, keepdims=True)


def _mm_kernel(*refs, n_w, has_add):
    a_ref = refs[0]
    w_refs = refs[1:1 + n_w]
    pos = 1 + n_w
    add_ref = refs[pos] if has_add else None
    pos += int(has_add)
    o_ref = refs[pos]
    wbf = refs[pos + 1:pos + 1 + n_w]

    @pl.when(pl.program_id(1) == 0)
    def _():
        for w, s in zip(w_refs, wbf):
            s[...] = w[...].astype(BF16)

    a = a_ref[...]
    y = jnp.dot(a, wbf[0][...], preferred_element_type=F32)
    if n_w == 2:
        u = jnp.dot(a, wbf[1][...], preferred_element_type=F32)
        y = y * jax.nn.sigmoid(y) * u
    if has_add:
        y = y + add_ref[...]
    o_ref[...] = y.astype(o_ref.dtype)


def pmatmul(a, w, widx=(), *, n_out, tn, k0=0, kk=None, w2=None, add=None,
            out_dtype=F32, tm_cap=832):
    m = a.shape[0]
    kk = a.shape[1] if kk is None else kk
    kb = k0 // kk
    tm = _pick_tile(m, tm_cap, 16)
    ws = [w] if w2 is None else [w, w2]
    nlead = len(widx)
    grid = (n_out // tn, m // tm)
    in_specs = [pl.BlockSpec((tm, kk), lambda j, i: (i, kb))]
    for _ in ws:
        in_specs.append(pl.BlockSpec((None,) * nlead + (kk, tn),
                                     lambda j, i: tuple(widx) + (kb, j)))
    args = [a] + ws
    if add is not None:
        in_specs.append(pl.BlockSpec((tm, tn), lambda j, i: (i, j)))
        args.append(add)
    return pl.pallas_call(
        functools.partial(_mm_kernel, n_w=len(ws), has_add=add is not None),
        grid=grid,
        in_specs=in_specs,
        out_specs=pl.BlockSpec((tm, tn), lambda j, i: (i, j)),
        out_shape=jax.ShapeDtypeStruct((m, n_out), out_dtype),
        scratch_shapes=[pltpu.VMEM((kk, tn), BF16) for _ in ws],
        compiler_params=_cparams("arbitrary", "arbitrary"),
    )(*args)


def _inv_kernel(m_ref, n_ref, *, c):
    n_ref[0:c, :] = -m_ref[0:c, :]

    def row_i(i, carry):
        base = pl.multiple_of(i * c, c)
        acc = m_ref[pl.ds(base, c), :]

        def inner(j, acc):
            coef = m_ref[pl.ds(base + j, 1), :]
            nj = n_ref[pl.ds(pl.multiple_of(j * c, c), c), :]
            return acc + coef * nj

        acc = lax.fori_loop(0, i, inner, acc)
        n_ref[pl.ds(base, c), :] = -acc
        return carry

    lax.fori_loop(1, c, row_i, 0)


def tri_inverse(mats):
    shape = mats.shape
    c = shape[-1]
    u = 1
    for s in shape[:-2]:
        u *= s
    up = -(-u // LANES) * LANES
    flat = mats.reshape(u, c * c).T
    if up != u:
        flat = jnp.pad(flat, ((0, 0), (0, up - u)))
    out = pl.pallas_call(
        functools.partial(_inv_kernel, c=c),
        grid=(up // LANES,),
        in_specs=[pl.BlockSpec((c * c, LANES), lambda g: (0, g))],
        out_specs=pl.BlockSpec((c * c, LANES), lambda g: (0, g)),
        out_shape=jax.ShapeDtypeStruct((c * c, up), F32),
        compiler_params=_cparams("arbitrary"),
    )(flat)
    return out[:, :u].T.reshape(shape)


def _rw1_kernel(lw_ref, kap_ref, r_ref, k_ref, b_ref,
                kaph_ref, rh_ref, kh_ref, bh_ref, khp_ref, bhp_ref, pc_ref, mb_ref):
    lw = lw_ref[...]
    c, lb = lw.shape
    tri = _tri(c)
    hi, mid, lo = _split3(lw)
    p = (jnp.dot(tri, hi, preferred_element_type=F32) + jnp.dot(tri, mid, preferred_element_type=F32)
         + jnp.dot(tri, lo, preferred_element_type=F32))
    pc = p[c - 1:c, :]
    en = jnp.exp(-p)
    ec = jnp.exp(pc - p)
    k = k_ref[...]
    b = b_ref[...]
    kaph = kap_ref[...] * jnp.exp(p - lw)
    bh = (b * en).astype(BF16)
    kaph_ref[...] = kaph.astype(BF16)
    rh_ref[...] = (r_ref[...] * jnp.exp(p)).astype(BF16)
    kh_ref[...] = (k * en).astype(BF16)
    bh_ref[...] = bh
    khp_ref[...] = (k * ec).astype(BF16)
    bhp_ref[...] = (b * ec).astype(BF16)
    pc_ref[...] = jnp.exp(pc)
    lane = lax.broadcasted_iota(jnp.int32, (c, LANES), 1)
    m0 = lane < RWKV_N
    rowi = lax.broadcasted_iota(jnp.int32, (2 * c, c), 0) % c
    coli = lax.broadcasted_iota(jnp.int32, (2 * c, c), 1)
    strict = rowi > coli
    for jt in range(lb // LANES):
        kp = kaph[:, jt * LANES:(jt + 1) * LANES]
        bp = bh[:, jt * LANES:(jt + 1) * LANES]
        lhs = jnp.concatenate([jnp.where(m0, kp, 0.0), jnp.where(m0, 0.0, kp)], axis=0)
        g = _dot_nt(lhs, bp)
        g = jnp.where(strict, g, 0.0)
        mb_ref[2 * jt] = g[:c]
        mb_ref[2 * jt + 1] = g[c:]


def _rw3_kernel(kaph_ref, rh_ref, kh_ref, bh_ref, khp_ref, bhp_ref, v_ref, pc_ref, n_ref,
                y_ref, sfin_ref, a_ref):
    ci = pl.program_id(1)
    c = kaph_ref.shape[0]
    npair = a_ref.shape[0]

    @pl.when(ci == 0)
    def _():
        a_ref[...] = jnp.zeros_like(a_ref)

    lane2 = lax.broadcasted_iota(jnp.int32, (2 * c, LANES), 1)
    m0_2 = lane2 < RWKV_N
    m0 = lax.broadcasted_iota(jnp.int32, (c, LANES), 1) < RWKV_N
    row = lax.broadcasted_iota(jnp.int32, (c, c), 0)
    col = lax.broadcasted_iota(jnp.int32, (c, c), 1)
    strict = row > col
    incl = row >= col
    rr = lax.broadcasted_iota(jnp.int32, (LANES, LANES), 0)
    cc = lax.broadcasted_iota(jnp.int32, (LANES, LANES), 1)
    blockdiag = (rr < RWKV_N) == (cc < RWKV_N)

    def pair(j, carry):
        ls = pl.ds(pl.multiple_of(j * LANES, LANES), LANES)
        kap = kaph_ref[:, ls]
        rh = rh_ref[:, ls]
        kh = kh_ref[:, ls]
        bh = bh_ref[:, ls]
        vv = v_ref[:, ls]
        a0 = a_ref[j]
        lhs = jnp.concatenate([kap, rh], axis=0)
        x = _dot(lhs, a0)
        ka, ra = x[:c], x[c:]
        u = jnp.zeros((c, LANES), F32)
        y = jnp.zeros((c, LANES), F32)
        lhs32 = lhs.astype(F32)
        for half in range(2):
            lhs_m = jnp.where(m0_2, lhs32, 0.0) if half == 0 else jnp.where(m0_2, 0.0, lhs32)
            gk = _dot_nt(lhs_m, kh)
            gb = _dot_nt(lhs_m[c:], bh)
            mk = jnp.where(strict, gk[:c], 0.0)
            lrk = jnp.where(incl, gk[c:], 0.0)
            lrb = jnp.where(incl, gb, 0.0)
            rhs = ka + _dot(mk, vv)
            nh = n_ref[2 * j + half]
            uh = rhs + _dot(nh, rhs)
            yh = ra + _dot(lrk, vv) - _dot(lrb, uh)
            sel = m0 if half == 0 else jnp.logical_not(m0)
            u = jnp.where(sel, uh, u)
            y = jnp.where(sel, yh, y)
        y_ref[:, ls] = y
        pcc = _row_to_col(pc_ref[:, ls])
        an = _dot_tn(khp_ref[:, ls], vv) - _dot_tn(bhp_ref[:, ls], u)
        a_ref[j] = pcc * a0 + jnp.where(blockdiag, an, 0.0)
        return carry

    lax.fori_loop(0, npair, pair, 0)

    @pl.when(ci == pl.num_programs(1) - 1)
    def _():
        sfin_ref[...] = a_ref[...]


def rwkv_prompt_scan(lw, kap, r, k, b, v):
    bsz, t, d = lw.shape
    c = CHUNK
    nc = t // c
    h = d // RWKV_N
    lb = min(d, 1024)
    tok = lambda: pl.BlockSpec((None, c, lb), lambda bi, ci, li: (bi, ci, li))
    bf = jax.ShapeDtypeStruct((bsz, t, d), BF16)
    kaph, rh, kh, bh, khp, bhp, pc, mb = pl.pallas_call(
        _rw1_kernel,
        grid=(bsz, nc, d // lb),
        in_specs=[tok() for _ in range(5)],
        out_specs=[tok() for _ in range(6)] + [
            pl.BlockSpec((None, None, 1, lb), lambda bi, ci, li: (bi, ci, 0, li)),
            pl.BlockSpec((None, None, lb // RWKV_N, c, c), lambda bi, ci, li: (bi, ci, li, 0, 0)),
        ],
        out_shape=[bf] * 6 + [jax.ShapeDtypeStruct((bsz, nc, 1, d), F32),
                              jax.ShapeDtypeStruct((bsz, nc, h, c, c), F32)],
        compiler_params=_cparams("arbitrary", "arbitrary", "arbitrary"),
    )(lw, kap, r, k, b)
    nmat = tri_inverse(mb)
    tokd = lambda: pl.BlockSpec((None, c, d), lambda bi, ci: (bi, ci, 0))
    y, sfin = pl.pallas_call(
        _rw3_kernel,
        grid=(bsz, nc),
        in_specs=[tokd() for _ in range(7)] + [
            pl.BlockSpec((None, None, 1, d), lambda bi, ci: (bi, ci, 0, 0)),
            pl.BlockSpec((None, None, h, c, c), lambda bi, ci: (bi, ci, 0, 0, 0)),
        ],
        out_specs=[tokd(), pl.BlockSpec((None, h // 2, LANES, LANES), lambda bi, ci: (bi, 0, 0, 0))],
        out_shape=[jax.ShapeDtypeStruct((bsz, t, d), F32),
                   jax.ShapeDtypeStruct((bsz, h // 2, LANES, LANES), F32)],
        scratch_shapes=[pltpu.VMEM((h // 2, LANES, LANES), F32)],
        compiler_params=_cparams("arbitrary", "arbitrary"),
    )(kaph, rh, kh, bh, khp, bhp, v.astype(BF16), pc, nmat)
    n = RWKV_N
    s_even = sfin[:, :, :n, :n]
    s_odd = sfin[:, :, n:, n:]
    s = jnp.stack([s_even, s_odd], axis=2).reshape(bsz, h, n, n)
    return y, jnp.swapaxes(s, -1, -2)


def _rws_kernel(s_ref, w_ref, kap_ref, b_ref, k_ref, v_ref, r_ref, so_ref, y_ref):
    nh = s_ref.shape[0]

    def head(j, carry):
        one = pl.ds(j, 1)
        s = s_ref[j]
        sa = -jnp.sum(s * kap_ref[one, :], axis=1, keepdims=True)
        vcol = _row_to_col(v_ref[one, :])
        sn = s * w_ref[one, :] + sa * b_ref[one, :] + vcol * k_ref[one, :]
        so_ref[j] = sn
        ycol = jnp.sum(sn * r_ref[one, :], axis=1, keepdims=True)
        y_ref[one, :] = _col_to_row(ycol)
        return carry

    lax.fori_loop(0, nh, head, 0)


def rwkv_decode_step(s0, w, kap, b, k, v, r):
    bsz, h, n, _ = s0.shape
    vec = lambda: pl.BlockSpec((None, h, n), lambda bi: (bi, 0, 0))
    st = lambda: pl.BlockSpec((None, h, n, n), lambda bi: (bi, 0, 0, 0))
    sn, y = pl.pallas_call(
        _rws_kernel,
        grid=(bsz,),
        in_specs=[st()] + [vec() for _ in range(6)],
        out_specs=[st(), vec()],
        out_shape=[jax.ShapeDtypeStruct(s0.shape, F32), jax.ShapeDtypeStruct((bsz, h, n), F32)],
        compiler_params=_cparams("arbitrary"),
    )(s0, w, kap, b, k, v, r)
    return y, sn


def _gd_head_scalars(g_ref, beta_ref, gcr_s, tri_t):
    hi, mid, lo = _split3(g_ref[...])
    gcr_s[...] = (jnp.dot(hi, tri_t, preferred_element_type=F32)
                  + jnp.dot(mid, tri_t, preferred_element_type=F32)
                  + jnp.dot(lo, tri_t, preferred_element_type=F32))


def _gd1_kernel(k_ref, g_ref, beta_ref, l_ref, gcr_s):
    c = k_ref.shape[0]
    gh = g_ref.shape[0]
    _gd_head_scalars(g_ref, beta_ref, gcr_s, _tri(c, transpose=True))
    row = lax.broadcasted_iota(jnp.int32, (c, c), 0)
    col = lax.broadcasted_iota(jnp.int32, (c, c), 1)
    strict = row > col

    def head(j, carry):
        ls = pl.ds(pl.multiple_of(j * GDN_DK, GDN_DK), GDN_DK)
        gr = gcr_s[pl.ds(j, 1), :]
        gcol = _row_to_col(gr)
        bcol = _row_to_col(beta_ref[pl.ds(j, 1), :])
        kk = k_ref[:, ls]
        gram = _dot_nt(kk * bcol, kk)
        dec = jnp.exp(jnp.where(strict, gcol - gr, 0.0))
        l_ref[j] = jnp.where(strict, gram * dec, 0.0)
        return carry

    lax.fori_loop(0, gh, head, 0)


def _gd3_kernel(q_ref, k_ref, v_ref, g_ref, beta_ref, n_ref, o_ref, sfin_ref, s_ref, gcr_s):
    ci = pl.program_id(1)
    c = k_ref.shape[0]
    gh = g_ref.shape[0]

    @pl.when(ci == 0)
    def _():
        s_ref[...] = jnp.zeros_like(s_ref)

    _gd_head_scalars(g_ref, beta_ref, gcr_s, _tri(c, transpose=True))
    row = lax.broadcasted_iota(jnp.int32, (c, c), 0)
    col = lax.broadcasted_iota(jnp.int32, (c, c), 1)
    incl = row >= col

    def head(j, carry):
        ls = pl.ds(pl.multiple_of(j * GDN_DK, GDN_DK), GDN_DK)
        gr = gcr_s[pl.ds(j, 1), :]
        gcol = _row_to_col(gr)
        bcol = _row_to_col(beta_ref[pl.ds(j, 1), :])
        glast = gr[:, c - 1:c]
        q = q_ref[:, ls]
        kk = k_ref[:, ls]
        vv = v_ref[:, ls]
        dec = jnp.exp(jnp.where(incl, gcol - gr, 0.0))
        kb = kk * bcol
        eg = jnp.exp(gcol)
        rhs = jnp.concatenate([vv * bcol, kb * eg], axis=1)
        sol = rhs + _dot(n_ref[j], rhs)
        u, w = sol[:, :GDN_DK], sol[:, GDN_DK:]
        attn = jnp.where(incl, _dot_nt(q, kk) * dec, 0.0)
        s = s_ref[j]
        v_new = u - _dot(w, s)
        o_ref[:, ls] = _dot(q * eg, s) + _dot(attn, v_new)
        kdec = kk * jnp.exp(glast - gcol)
        s_ref[j] = s * jnp.exp(glast) + _dot_tn(kdec, v_new)
        return carry

    lax.fori_loop(0, gh, head, 0)

    @pl.when(ci == pl.num_programs(1) - 1)
    def _():
        sfin_ref[...] = s_ref[...]


def gdn_prompt_scan(q, k, v, g, beta):
    bsz, t, d = q.shape
    c = CHUNK
    nc = t // c
    gh = d // GDN_DK
    rows = lambda x: jnp.swapaxes(x.reshape(bsz, nc, c, gh), -1, -2)
    g_r, beta_r = rows(g), rows(beta)
    tokd = lambda: pl.BlockSpec((None, c, d), lambda bi, ci: (bi, ci, 0))
    hrow = lambda: pl.BlockSpec((None, None, gh, c), lambda bi, ci: (bi, ci, 0, 0))
    mat = lambda: pl.BlockSpec((None, None, gh, c, c), lambda bi, ci: (bi, ci, 0, 0, 0))
    lmat = pl.pallas_call(
        _gd1_kernel,
        grid=(bsz, nc),
        in_specs=[tokd(), hrow(), hrow()],
        out_specs=mat(),
        out_shape=jax.ShapeDtypeStruct((bsz, nc, gh, c, c), F32),
        scratch_shapes=[pltpu.VMEM((gh, c), F32)],
        compiler_params=_cparams("arbitrary", "arbitrary"),
    )(k, g_r, beta_r)
    nmat = tri_inverse(lmat)
    o, sfin = pl.pallas_call(
        _gd3_kernel,
        grid=(bsz, nc),
        in_specs=[tokd(), tokd(), tokd(), hrow(), hrow(), mat()],
        out_specs=[tokd(), pl.BlockSpec((None, gh, GDN_DK, GDN_DK), lambda bi, ci: (bi, 0, 0, 0))],
        out_shape=[jax.ShapeDtypeStruct((bsz, t, d), F32),
                   jax.ShapeDtypeStruct((bsz, gh, GDN_DK, GDN_DK), F32)],
        scratch_shapes=[pltpu.VMEM((gh, GDN_DK, GDN_DK), F32), pltpu.VMEM((gh, c), F32)],
        compiler_params=_cparams("arbitrary", "arbitrary"),
    )(q, k, v, g_r, beta_r, nmat)
    return o, sfin


def _gds_kernel(s_ref, q_ref, k_ref, v_ref, beta_ref, eg_ref, so_ref, o_ref):
    gh = s_ref.shape[0]

    def head(j, carry):
        ls = pl.ds(pl.multiple_of(j * GDN_DK, GDN_DK), GDN_DK)
        s = s_ref[j]
        q, kk, vv = q_ref[:, ls], k_ref[:, ls], v_ref[:, ls]
        beta, eg = beta_ref[:, ls], eg_ref[:, ls]
        kcol = _row_to_col(kk)
        qcol = _row_to_col(q)
        ks = jnp.sum(kcol * s, axis=0, keepdims=True)
        v_new = beta * (vv - eg * ks)
        sn = s * eg + kcol * v_new
        so_ref[j] = sn
        o_ref[:, ls] = jnp.sum(qcol * sn, axis=0, keepdims=True)
        return carry

    lax.fori_loop(0, gh, head, 0)


def gdn_decode_step(s0, q, k, v, beta, g):
    bsz, gh, dk, _ = s0.shape
    d = gh * dk
    wide = lambda x: jnp.repeat(x, dk, axis=-1).reshape(bsz, 1, d)
    vec = lambda: pl.BlockSpec((None, 1, d), lambda bi: (bi, 0, 0))
    st = lambda: pl.BlockSpec((None, gh, dk, dk), lambda bi: (bi, 0, 0, 0))
    sn, o = pl.pallas_call(
        _gds_kernel,
        grid=(bsz,),
        in_specs=[st()] + [vec() for _ in range(5)],
        out_specs=[st(), vec()],
        out_shape=[jax.ShapeDtypeStruct(s0.shape, F32), jax.ShapeDtypeStruct((bsz, 1, d), F32)],
        compiler_params=_cparams("arbitrary"),
    )(s0, q.reshape(bsz, 1, d), k.reshape(bsz, 1, d), v.reshape(bsz, 1, d), wide(beta), wide(jnp.exp(g)))
    return o.reshape(bsz, d), sn


def _rmsnorm(x, w):
    return x * lax.rsqrt(jnp.mean(x * x, axis=-1, keepdims=True) + RMS_EPS) * w


def _l2norm(x):
    return x * lax.rsqrt(jnp.sum(x * x, axis=-1, keepdims=True) + 1e-6)


def _merge(pair):
    p, s = pair
    kdim = p.shape[-1]
    return jnp.concatenate([p.reshape(-1, kdim), s.reshape(-1, kdim)], axis=0).astype(BF16)


def _split(y, like):
    p, s = like
    mp = p.shape[0] * p.shape[1]
    return y[:mp].reshape(p.shape[0], p.shape[1], -1), y[mp:].reshape(s.shape[0], s.shape[1], -1)


def _dense(pair, w, widx, n_out, tn, **kw):
    return _split(pmatmul(_merge(pair), w, widx, n_out=n_out, tn=tn, **kw), pair)


def _both(f, *pairs):
    return tuple(f(*xs) for xs in zip(*pairs))


def _rwkv_mix(h, shift_prev, s0, j, P):
    hp, hs = h
    bp, t, d = hp.shape
    nh = d // RWKV_N
    prev = (jnp.concatenate([jnp.zeros((bp, 1, d), F32), hp[:, :-1]], axis=1), shift_prev[:, None, :])
    xx = _both(lambda a, b: a - b, prev, h)
    mix = P['rw_mix'][j]
    xs = [_both(lambda a, b: a + b * mix[i], h, xx) for i in range(6)]
    tn = min(d, 512)
    r = _dense(xs[0], P['rw_w_rkv'], (j, 0), d, tn)
    k = _dense(xs[1], P['rw_w_rkv'], (j, 1), d, tn)
    v = _dense(xs[2], P['rw_w_rkv'], (j, 2), d, tn)
    lora = P['rw_w1'].shape[-1]
    wl = _dense(_both(jnp.tanh, _dense(xs[3], P['rw_w1'], (j,), lora, lora)), P['rw_w2'], (j,), d, tn)
    al = _dense(_dense(xs[4], P['rw_a1'], (j,), lora, lora), P['rw_a2'], (j,), d, tn)
    gl = P['rw_g1'].shape[-1]
    glp = -(-gl // LANES) * LANES
    g1 = jnp.pad(P['rw_g1'][j], ((0, 0), (0, glp - gl)))
    g2 = jnp.pad(P['rw_g2'][j], ((0, glp - gl), (0, 0)))
    g = _dense(_both(jax.nn.sigmoid, _dense(xs[5], g1, (), glp, glp)), g2, (), d, tn)

    def prep(r, k, v, wl, al):
        w_log = -jax.nn.softplus(-(P['rw_w0'][j] + wl)) - 0.5
        lw = -jnp.exp(w_log)
        a = jax.nn.sigmoid(P['rw_a0'][j] + al)
        shp = k.shape[:-1] + (nh, RWKV_N)
        kk = _l2norm((k * P['rw_k_k'][j]).reshape(shp)).reshape(k.shape)
        k_mod = k * (1.0 + (a - 1.0) * P['rw_k_a'][j])
        return lw, kk, k_mod, kk * a

    lw, kk, k_mod, b = zip(*[prep(*xs_) for xs_ in zip(r, k, v, wl, al)])
    yp, sp = rwkv_prompt_scan(lw[0], kk[0], r[0], k_mod[0], b[0], v[0])
    bs = hs.shape[0]
    hv = lambda x: x.reshape(bs, nh, RWKV_N)
    ys, ss = rwkv_decode_step(s0, hv(jnp.exp(lw[1])), hv(kk[1]), hv(b[1]), hv(k_mod[1]), hv(v[1]), hv(r[1]))
    y = (yp, ys.reshape(bs, 1, d))

    def post(y, r, k_mod, v, g):
        shp = y.shape[:-1] + (nh, RWKV_N)
        yh = y.reshape(shp)
        mu = jnp.mean(yh, axis=-1, keepdims=True)
        var = jnp.mean(jnp.square(yh - mu), axis=-1, keepdims=True)
        yn = ((yh - mu) * lax.rsqrt(var + RWKV_LN_EPS)).reshape(y.shape) * P['rw_ln_w'][j] + P['rw_ln_b'][j]
        bonus = jnp.sum(r.reshape(shp) * k_mod.reshape(shp) * P['rw_r_k'][j], axis=-1, keepdims=True) * v.reshape(shp)
        return (yn + bonus.reshape(y.shape)) * g

    yg = _both(post, y, r, k_mod, v, g)
    out = _dense(yg, P['rw_w_o'], (j,), d, tn)
    return out, (sp, ss), (hp[:, -1], hs[:, -1])


def _gdn_mix(h, conv_prev, s0, j, P):
    hp, hs = h
    bp, t, d = hp.shape
    bs = hs.shape[0]
    gh = d // GDN_DK
    cdim = 3 * d
    tn = min(d, 512)
    w_in = P['gd_w_in']
    main = _dense(h, w_in, (j,), 4 * d, tn)
    w_tail = jnp.pad(w_in[j][:, 4 * d:], ((0, 0), (0, LANES - 2 * gh)))
    tail = _dense(h, w_tail, (), LANES, LANES)
    conv_w = P['gd_conv_w'][j]

    def conv(qkv, buf):
        tt = qkv.shape[1]
        xp = jnp.concatenate([buf, qkv], axis=1)
        y = xp[:, 0:tt] * conv_w[0]
        for i in range(1, GDN_CONV):
            y = y + xp[:, i:i + tt] * conv_w[i]
        return y, xp[:, tt:]

    bufs = (jnp.zeros((bp, GDN_CONV - 1, cdim), F32), conv_prev)
    outs = []
    convs = []
    for grp in range(2):
        m, tl = main[grp], tail[grp]
        y, cnew = conv(m[..., :cdim], bufs[grp])
        convs.append(cnew)
        qkv = jax.nn.silu(y)
        lead = qkv.shape[:2]
        q = _l2norm(qkv[..., :d].reshape(lead + (gh, GDN_DK))).reshape(lead + (d,)) * (GDN_DK ** -0.5)
        k = _l2norm(qkv[..., d:2 * d].reshape(lead + (gh, GDN_DK))).reshape(lead + (d,))
        v = qkv[..., 2 * d:]
        beta = jax.nn.sigmoid(tl[..., :gh])
        g = -jnp.exp(P['gd_A_log'][j]) * jax.nn.softplus(tl[..., gh:2 * gh] + P['gd_dt_bias'][j])
        outs.append((q, k, v, g, beta, m[..., cdim:]))
    qp, kp, vp, gp, betap, zp = outs[0]
    qs, ks, vs, gs, betas, zs = outs[1]
    op, sp = gdn_prompt_scan(qp, kp, vp, gp, betap)
    os_, ss = gdn_decode_step(s0, qs[:, 0], ks[:, 0], vs[:, 0], betas[:, 0], gs[:, 0])
    o = (op, os_.reshape(bs, 1, d))

    def post(o, z):
        shp = o.shape[:-1] + (gh, GDN_DK)
        oh = o.reshape(shp)
        oh = oh * lax.rsqrt(jnp.mean(oh * oh, axis=-1, keepdims=True) + RMS_EPS) * P['gd_norm_w'][j]
        return oh.reshape(o.shape) * jax.nn.silu(z)

    og = _both(post, o, (zp, zs))
    out = _dense(og, P['gd_w_out'], (j,), d, tn)
    return out, (sp, ss), tuple(convs)


def kernel(x_prompt, x_sample, state_rwkv, state_rwkv_shift, state_gdn, state_gdn_conv, c_prompt, c_sample,
           w_ada, b_ada, norm1_w, norm2_w, rw_mix, rw_w_rkv, rw_w0, rw_w1, rw_w2, rw_a0, rw_a1, rw_a2,
           rw_g1, rw_g2, rw_k_k, rw_k_a, rw_r_k, rw_ln_w, rw_ln_b, rw_w_o, gd_w_in, gd_conv_w, gd_A_log,
           gd_dt_bias, gd_norm_w, gd_w_out, ffn_w_gate, ffn_w_up, ffn_w_down, final_norm_w):
    P = {
        'rw_mix': rw_mix, 'rw_w_rkv': rw_w_rkv, 'rw_w0': rw_w0, 'rw_w1': rw_w1, 'rw_w2': rw_w2,
        'rw_a0': rw_a0, 'rw_a1': rw_a1, 'rw_a2': rw_a2, 'rw_g1': rw_g1, 'rw_g2': rw_g2,
        'rw_k_k': rw_k_k, 'rw_k_a': rw_k_a, 'rw_r_k': rw_r_k.reshape(rw_r_k.shape[0], 1, -1, RWKV_N),
        'rw_ln_w': rw_ln_w, 'rw_ln_b': rw_ln_b,
        'rw_w_o': rw_w_o, 'gd_w_in': gd_w_in, 'gd_conv_w': gd_conv_w, 'gd_A_log': gd_A_log,
        'gd_dt_bias': gd_dt_bias, 'gd_norm_w': gd_norm_w, 'gd_w_out': gd_w_out,
    }
    bp, t, d = x_prompt.shape
    bs = x_sample.shape[0]
    depth = w_ada.shape[0]
    dff = ffn_w_gate.shape[-1]
    assert t % CHUNK == 0 and d % (2 * LANES) == 0

    c_all = jnp.concatenate([c_prompt, c_sample], axis=0)
    nb = c_all.shape[0]
    nbp = -(-nb // 16) * 16
    c_act = jnp.pad(jax.nn.silu(c_all), ((0, nbp - nb), (0, 0))).astype(BF16)

    x = (x_prompt, x_sample)
    new_rw_s, new_rw_shift, new_gd_s, new_gd_conv = [], [], [], []
    tn_d = min(d, 512)
    tn_f = _pick_tile(dff, 256, LANES)
    for layer in range(depth):
        mod = pmatmul(c_act, w_ada, (layer,), n_out=6 * d, tn=tn_d)[:nb] + b_ada[layer]
        mod = mod.reshape(nb, 6, 1, d)
        mods = (mod[:bp], mod[bp:])
        h = _both(lambda xx, m: _rmsnorm(xx, norm1_w[layer]) * (1.0 + m[:, 1]) + m[:, 0], x, mods)
        j = layer // 2
        if layer % 2 == 0:
            out, s_new, sh_new = _rwkv_mix(h, state_rwkv_shift[j], state_rwkv[j], j, P)
            new_rw_s.append(s_new)
            new_rw_shift.append(sh_new)
        else:
            out, s_new, cb_new = _gdn_mix(h, state_gdn_conv[j], state_gdn[j], j, P)
            new_gd_s.append(s_new)
            new_gd_conv.append(cb_new)
        x = _both(lambda xx, m, o: xx + m[:, 2] * o, x, mods, out)
        h = _both(lambda xx, m: _rmsnorm(xx, norm2_w[layer]) * (1.0 + m[:, 4]) + m[:, 3], x, mods)
        hm = _merge(h)
        act = pmatmul(hm, ffn_w_gate, (layer,), n_out=dff, tn=tn_f, w2=ffn_w_up, out_dtype=BF16)
        kh = dff // 2
        tn_o = min(d, 256)
        part = pmatmul(act, ffn_w_down, (layer,), n_out=d, tn=tn_o, k0=0, kk=kh)
        ffn = _split(pmatmul(act, ffn_w_down, (layer,), n_out=d, tn=tn_o, k0=kh, kk=kh, add=part), h)
        x = _both(lambda xx, m, o: xx + m[:, 5] * o, x, mods, ffn)
    y = _both(lambda xx: _rmsnorm(xx, final_norm_w), x)
    grp = lambda lst, i: jnp.stack([e[i] for e in lst])
    return (y[0], y[1],
            grp(new_rw_s, 0), grp(new_rw_shift, 0), grp(new_gd_s, 0), grp(new_gd_conv, 0),
            grp(new_rw_s, 1), grp(new_rw_shift, 1), grp(new_gd_s, 1), grp(new_gd_conv, 1))
```

```python
import functools

import jax
import jax.numpy as jnp
from jax import lax
from jax.experimental import pallas as pl
from jax.experimental.pallas import tpu as pltpu

F32, BF16 = jnp.float32, jnp.bfloat16

RMS_EPS = 1e-6
RWKV_LN_EPS = 64e-5
RWKV_N = 64
GDN_DK = 128
GDN_CONV = 4
CHUNK = 64
LANES = 128
SUBLANES = 8
GROUP = 4
VMEM_LIMIT = 50 * 2**20


def _cparams(*sem):
    return pltpu.CompilerParams(dimension_semantics=sem, vmem_limit_bytes=VMEM_LIMIT)


def _pick_tile(n, cap, mult):
    best = None
    for t in range(mult, min(n, cap) + 1, mult):
        if n % t == 0:
            best = t
    return best or n


def _group(n):
    g = GROUP
    while n % g:
        g //= 2
    return g


def _dot(a, b):
    return jnp.dot(a.astype(BF16), b.astype(BF16), preferred_element_type=F32)


def _dot_nt(a, b):
    return lax.dot_general(a.astype(BF16), b.astype(BF16), (((1,), (1,)), ((), ())),
                           preferred_element_type=F32)


def _dot_tn(a, b):
    return lax.dot_general(a.astype(BF16), b.astype(BF16), (((0,), (0,)), ((), ())),
                           preferred_element_type=F32)


def _split3(x):
    hi = x.astype(BF16)
    r1 = x - hi.astype(F32)
    mid = r1.astype(BF16)
    lo = (r1 - mid.astype(F32)).astype(BF16)
    return hi, mid, lo


def _tri(n, transpose=False):
    row = lax.broadcasted_iota(jnp.int32, (n, n), 0)
    col = lax.broadcasted_iota(jnp.int32, (n, n), 1)
    m = (row <= col) if transpose else (row >= col)
    return jnp.where(m, 1.0, 0.0).astype(BF16)


def _row_to_col(row):
    n = row.shape[1]
    eye = lax.broadcasted_iota(jnp.int32, (n, n), 0) == lax.broadcasted_iota(jnp.int32, (n, n), 1)
    return jnp.sum(jnp.where(eye, jnp.broadcast_to(row, (n, n)), 0.0), axis=1, keepdims=True)


def _col_to_row(col):
    n = col.shape[0]
    eye = lax.broadcasted_iota(jnp.int32, (n, n), 0) == lax.broadcasted_iota(jnp.int32, (n, n), 1)
    return jnp.sum(jnp.where(eye, jnp.broadcast_to(col, (n, n)), 0.0), axis=0, keepdims=True)


def _mm_kernel(*refs, n_w, has_add):
    a_ref = refs[0]
    w_refs = refs[1:1 + n_w]
    pos = 1 + n_w
    add_ref = refs[pos] if has_add else None
    pos += int(has_add)
    o_ref = refs[pos]
    wbf = refs[pos + 1:pos + 1 + n_w]

    @pl.when(pl.program_id(1) == 0)
    def _():
        for w, s in zip(w_refs, wbf):
            s[...] = w[...].astype(BF16)

    a = a_ref[...]
    y = jnp.dot(a, wbf[0][...], preferred_element_type=F32)
    if n_w == 2:
        u = jnp.dot(a, wbf[1][...], preferred_element_type=F32)
        y = y * jax.nn.sigmoid(y) * u
    if has_add:
        y = y + add_ref[...]
    o_ref[...] = y.astype(o_ref.dtype)


def pmatmul(a, w, widx=(), *, n_out, tn, k0=0, kk=None, w2=None, add=None,
            out_dtype=F32, tm_cap=832):
    m = a.shape[0]
    kk = a.shape[1] if kk is None else kk
    kb = k0 // kk
    tm = _pick_tile(m, tm_cap, 16)
    ws = [w] if w2 is None else [w, w2]
    nlead = len(widx)
    grid = (n_out // tn, m // tm)
    in_specs = [pl.BlockSpec((tm, kk), lambda j, i: (i, kb))]
    for _ in ws:
        in_specs.append(pl.BlockSpec((None,) * nlead + (kk, tn),
                                     lambda j, i: tuple(widx) + (kb, j)))
    args = [a] + ws
    if add is not None:
        in_specs.append(pl.BlockSpec((tm, tn), lambda j, i: (i, j)))
        args.append(add)
    return pl.pallas_call(
        functools.partial(_mm_kernel, n_w=len(ws), has_add=add is not None),
        grid=grid,
        in_specs=in_specs,
        out_specs=pl.BlockSpec((tm, tn), lambda j, i: (i, j)),
        out_shape=jax.ShapeDtypeStruct((m, n_out), out_dtype),
        scratch_shapes=[pltpu.VMEM((kk, tn), BF16) for _ in ws],
        compiler_params=_cparams("arbitrary", "arbitrary"),
        name=f"mm_m{m}_k{kk}_n{n_out}" + ("_glu" if w2 is not None else "") + ("_add" if add is not None else ""),
    )(*args)


def _inv_kernel(m_ref, n_ref, *, c):
    nblk = c // SUBLANES
    zeros = jnp.zeros((SUBLANES * c, LANES), F32)
    for ib in range(nblk):
        n_ref[ib * SUBLANES * c:(ib + 1) * SUBLANES * c, :] = zeros

        def row(ii, carry, ib=ib):
            base = pl.multiple_of((ib * SUBLANES + ii) * c, c)
            acc = [m_ref[pl.ds(base + SUBLANES * k, SUBLANES), :] for k in range(ib + 1)]
            for j in range((ib + 1) * SUBLANES):
                coef = m_ref[pl.ds(base + j, 1), :]
                for k in range(j // SUBLANES + 1):
                    acc[k] = acc[k] + coef * n_ref[j * c + SUBLANES * k:j * c + SUBLANES * (k + 1), :]
            for k in range(ib + 1):
                n_ref[pl.ds(base + SUBLANES * k, SUBLANES), :] = -acc[k]
            return carry

        lax.fori_loop(0, SUBLANES, row, 0)


def tri_inverse(mats):
    shape = mats.shape
    c = shape[-1]
    u = 1
    for s in shape[:-2]:
        u *= s
    up = -(-u // LANES) * LANES
    flat = mats.reshape(u, c * c).T
    if up != u:
        flat = jnp.pad(flat, ((0, 0), (0, up - u)))
    out = pl.pallas_call(
        functools.partial(_inv_kernel, c=c),
        grid=(up // LANES,),
        in_specs=[pl.BlockSpec((c * c, LANES), lambda g: (0, g))],
        out_specs=pl.BlockSpec((c * c, LANES), lambda g: (0, g)),
        out_shape=jax.ShapeDtypeStruct((c * c, up), F32),
        compiler_params=_cparams("arbitrary"),
        name="tri_inverse",
    )(flat)
    return out[:, :u].T.reshape(shape)


def _rw1_kernel(lw_ref, kap_ref, r_ref, k_ref, b_ref,
                kaph_ref, rh_ref, kh_ref, bh_ref, khp_ref, bhp_ref, pc_ref, mb_ref):
    lw = lw_ref[...]
    c, lb = lw.shape
    tri = _tri(c)
    hi, mid, lo = _split3(lw)
    p = (jnp.dot(tri, hi, preferred_element_type=F32) + jnp.dot(tri, mid, preferred_element_type=F32)
         + jnp.dot(tri, lo, preferred_element_type=F32))
    pc = p[c - 1:c, :]
    en = jnp.exp(-p)
    ec = jnp.exp(pc - p)
    k = k_ref[...]
    b = b_ref[...]
    kaph = kap_ref[...] * jnp.exp(p - lw)
    bh = (b * en).astype(BF16)
    kaph_ref[...] = kaph.astype(BF16)
    rh_ref[...] = (r_ref[...] * jnp.exp(p)).astype(BF16)
    kh_ref[...] = (k * en).astype(BF16)
    bh_ref[...] = bh
    khp_ref[...] = (k * ec).astype(BF16)
    bhp_ref[...] = (b * ec).astype(BF16)
    pc_ref[...] = jnp.exp(pc)
    lane = lax.broadcasted_iota(jnp.int32, (c, LANES), 1)
    m0 = lane < RWKV_N
    rowi = lax.broadcasted_iota(jnp.int32, (2 * c, c), 0) % c
    coli = lax.broadcasted_iota(jnp.int32, (2 * c, c), 1)
    strict = rowi > coli
    for jt in range(lb // LANES):
        kp = kaph[:, jt * LANES:(jt + 1) * LANES]
        bp = bh[:, jt * LANES:(jt + 1) * LANES]
        lhs = jnp.concatenate([jnp.where(m0, kp, 0.0), jnp.where(m0, 0.0, kp)], axis=0)
        g = _dot_nt(lhs, bp)
        g = jnp.where(strict, g, 0.0)
        mb_ref[2 * jt] = g[:c]
        mb_ref[2 * jt + 1] = g[c:]


def _rw3_kernel(kaph_ref, rh_ref, kh_ref, bh_ref, khp_ref, bhp_ref, v_ref, pc_ref, n_ref,
                y_ref, sfin_ref, a_ref):
    ci = pl.program_id(1)
    c = kaph_ref.shape[0]
    npair = a_ref.shape[0]

    @pl.when(ci == 0)
    def _():
        a_ref[...] = jnp.zeros_like(a_ref)

    lane2 = lax.broadcasted_iota(jnp.int32, (2 * c, LANES), 1)
    m0_2 = lane2 < RWKV_N
    m0 = lax.broadcasted_iota(jnp.int32, (c, LANES), 1) < RWKV_N
    row = lax.broadcasted_iota(jnp.int32, (c, c), 0)
    col = lax.broadcasted_iota(jnp.int32, (c, c), 1)
    strict = row > col
    incl = row >= col
    rr = lax.broadcasted_iota(jnp.int32, (LANES, LANES), 0)
    cc = lax.broadcasted_iota(jnp.int32, (LANES, LANES), 1)
    blockdiag = (rr < RWKV_N) == (cc < RWKV_N)

    ng = _group(npair)
    halves = (0, 1)

    def group(gi, carry):
        js = [gi * ng + g for g in range(ng)]
        lss = [pl.ds(pl.multiple_of(j * LANES, LANES), LANES) for j in js]
        lhs = [jnp.concatenate([kaph_ref[:, ls], rh_ref[:, ls]], axis=0) for ls in lss]
        vv = [v_ref[:, ls] for ls in lss]
        a0 = [a_ref[j] for j in js]
        x = [_dot(l, a) for l, a in zip(lhs, a0)]
        lhs_m = []
        for l in lhs:
            l32 = l.astype(F32)
            lhs_m.append((jnp.where(m0_2, l32, 0.0).astype(BF16), jnp.where(m0_2, 0.0, l32).astype(BF16)))
        gk = [[_dot_nt(lhs_m[g][h], kh_ref[:, lss[g]]) for h in halves] for g in range(ng)]
        gb = [[_dot_nt(lhs_m[g][h][c:], bh_ref[:, lss[g]]) for h in halves] for g in range(ng)]
        mk = [[jnp.where(strict, gk[g][h][:c], 0.0).astype(BF16) for h in halves] for g in range(ng)]
        lrk = [[jnp.where(incl, gk[g][h][c:], 0.0).astype(BF16) for h in halves] for g in range(ng)]
        lrb = [[jnp.where(incl, gb[g][h], 0.0).astype(BF16) for h in halves] for g in range(ng)]
        mkv = [[_dot(mk[g][h], vv[g]) for h in halves] for g in range(ng)]
        lrkv = [[_dot(lrk[g][h], vv[g]) for h in halves] for g in range(ng)]
        kv = [_dot_tn(khp_ref[:, lss[g]], vv[g]) for g in range(ng)]
        rhs = [[x[g][:c] + mkv[g][h] for h in halves] for g in range(ng)]
        nr = [[_dot(n_ref[2 * js[g] + h], rhs[g][h]) for h in halves] for g in range(ng)]
        uh = [[rhs[g][h] + nr[g][h] for h in halves] for g in range(ng)]
        u = [jnp.where(m0, uh[g][0], uh[g][1]) for g in range(ng)]
        lu = [[_dot(lrb[g][h], uh[g][h]) for h in halves] for g in range(ng)]
        bu = [_dot_tn(bhp_ref[:, lss[g]], u[g]) for g in range(ng)]
        for g in range(ng):
            ra = x[g][c:]
            y_ref[:, lss[g]] = ra + jnp.where(m0, lrkv[g][0] - lu[g][0], lrkv[g][1] - lu[g][1])
            pcc = _row_to_col(pc_ref[:, lss[g]])
            a_ref[js[g]] = pcc * a0[g] + jnp.where(blockdiag, kv[g] - bu[g], 0.0)
        return carry

    lax.fori_loop(0, npair // ng, group, 0)

    @pl.when(ci == pl.num_programs(1) - 1)
    def _():
        sfin_ref[...] = a_ref[...]


def rwkv_prompt_scan(lw, kap, r, k, b, v):
    bsz, t, d = lw.shape
    c = CHUNK
    nc = t // c
    h = d // RWKV_N
    lb = min(d, 1024)
    tok = lambda: pl.BlockSpec((None, c, lb), lambda bi, ci, li: (bi, ci, li))
    bf = jax.ShapeDtypeStruct((bsz, t, d), BF16)
    kaph, rh, kh, bh, khp, bhp, pc, mb = pl.pallas_call(
        _rw1_kernel,
        grid=(bsz, nc, d // lb),
        in_specs=[tok() for _ in range(5)],
        out_specs=[tok() for _ in range(6)] + [
            pl.BlockSpec((None, None, 1, lb), lambda bi, ci, li: (bi, ci, 0, li)),
            pl.BlockSpec((None, None, lb // RWKV_N, c, c), lambda bi, ci, li: (bi, ci, li, 0, 0)),
        ],
        out_shape=[bf] * 6 + [jax.ShapeDtypeStruct((bsz, nc, 1, d), F32),
                              jax.ShapeDtypeStruct((bsz, nc, h, c, c), F32)],
        compiler_params=_cparams("arbitrary", "arbitrary", "arbitrary"),
        name="rwkv_chunk_prep",
    )(lw, kap, r, k, b)
    nmat = tri_inverse(mb)
    tokd = lambda: pl.BlockSpec((None, c, d), lambda bi, ci: (bi, ci, 0))
    y, sfin = pl.pallas_call(
        _rw3_kernel,
        grid=(bsz, nc),
        in_specs=[tokd() for _ in range(7)] + [
            pl.BlockSpec((None, None, 1, d), lambda bi, ci: (bi, ci, 0, 0)),
            pl.BlockSpec((None, None, h, c, c), lambda bi, ci: (bi, ci, 0, 0, 0)),
        ],
        out_specs=[tokd(), pl.BlockSpec((None, h // 2, LANES, LANES), lambda bi, ci: (bi, 0, 0, 0))],
        out_shape=[jax.ShapeDtypeStruct((bsz, t, d), F32),
                   jax.ShapeDtypeStruct((bsz, h // 2, LANES, LANES), F32)],
        scratch_shapes=[pltpu.VMEM((h // 2, LANES, LANES), F32)],
        compiler_params=_cparams("arbitrary", "arbitrary"),
        name="rwkv_chunk_scan",
    )(kaph, rh, kh, bh, khp, bhp, v.astype(BF16), pc, nmat)
    n = RWKV_N
    s_even = sfin[:, :, :n, :n]
    s_odd = sfin[:, :, n:, n:]
    s = jnp.stack([s_even, s_odd], axis=2).reshape(bsz, h, n, n)
    return y, jnp.swapaxes(s, -1, -2)


def _rws_kernel(s_ref, w_ref, kap_ref, b_ref, k_ref, v_ref, r_ref, so_ref, y_ref):
    s = s_ref[...]
    n = s.shape[-1]
    eye = (lax.broadcasted_iota(jnp.int32, (n, n), 0) == lax.broadcasted_iota(jnp.int32, (n, n), 1))[None]
    sa = -jnp.sum(s * kap_ref[...], axis=2, keepdims=True)
    vcol = jnp.sum(jnp.where(eye, v_ref[...], 0.0), axis=2, keepdims=True)
    sn = s * w_ref[...] + sa * b_ref[...] + vcol * k_ref[...]
    so_ref[...] = sn
    ycol = jnp.sum(sn * r_ref[...], axis=2, keepdims=True)
    y_ref[...] = jnp.sum(jnp.where(eye, ycol, 0.0), axis=1, keepdims=True)


def rwkv_decode_step(s0, w, kap, b, k, v, r):
    bsz, h, n, _ = s0.shape
    vec = lambda: pl.BlockSpec((None, h, 1, n), lambda bi: (bi, 0, 0, 0))
    st = lambda: pl.BlockSpec((None, h, n, n), lambda bi: (bi, 0, 0, 0))
    rows = lambda x: x.reshape(bsz, h, 1, n)
    sn, y = pl.pallas_call(
        _rws_kernel,
        grid=(bsz,),
        in_specs=[st()] + [vec() for _ in range(6)],
        out_specs=[st(), vec()],
        out_shape=[jax.ShapeDtypeStruct(s0.shape, F32), jax.ShapeDtypeStruct((bsz, h, 1, n), F32)],
        compiler_params=_cparams("arbitrary"),
        name="rwkv_decode",
    )(s0, rows(w), rows(kap), rows(b), rows(k), rows(v), rows(r))
    return y.reshape(bsz, h, n), sn


def _gd_head_scalars(g_ref, beta_ref, gcr_s, tri_t):
    hi, mid, lo = _split3(g_ref[...])
    gcr_s[...] = (jnp.dot(hi, tri_t, preferred_element_type=F32)
                  + jnp.dot(mid, tri_t, preferred_element_type=F32)
                  + jnp.dot(lo, tri_t, preferred_element_type=F32))


def _gd1_kernel(k_ref, g_ref, beta_ref, l_ref, gcr_s):
    c = k_ref.shape[0]
    gh = g_ref.shape[0]
    _gd_head_scalars(g_ref, beta_ref, gcr_s, _tri(c, transpose=True))
    row = lax.broadcasted_iota(jnp.int32, (c, c), 0)
    col = lax.broadcasted_iota(jnp.int32, (c, c), 1)
    strict = row > col

    ng = _group(gh)

    def group(gi, carry):
        js = [gi * ng + g for g in range(ng)]
        gr = [gcr_s[pl.ds(j, 1), :] for j in js]
        gcol = [_row_to_col(x) for x in gr]
        bcol = [_row_to_col(beta_ref[pl.ds(j, 1), :]) for j in js]
        kk = [k_ref[:, pl.ds(pl.multiple_of(j * GDN_DK, GDN_DK), GDN_DK)] for j in js]
        gram = [_dot_nt(kk[g] * bcol[g], kk[g]) for g in range(ng)]
        for g in range(ng):
            dec = jnp.exp(jnp.where(strict, gcol[g] - gr[g], 0.0))
            l_ref[js[g]] = jnp.where(strict, gram[g] * dec, 0.0)
        return carry

    lax.fori_loop(0, gh // ng, group, 0)


def _gd3_kernel(q_ref, k_ref, v_ref, g_ref, beta_ref, n_ref, o_ref, sfin_ref, s_ref, gcr_s):
    ci = pl.program_id(1)
    c = k_ref.shape[0]
    gh = g_ref.shape[0]

    @pl.when(ci == 0)
    def _():
        s_ref[...] = jnp.zeros_like(s_ref)

    _gd_head_scalars(g_ref, beta_ref, gcr_s, _tri(c, transpose=True))
    row = lax.broadcasted_iota(jnp.int32, (c, c), 0)
    col = lax.broadcasted_iota(jnp.int32, (c, c), 1)
    incl = row >= col

    ng = _group(gh)
    rng = range(ng)

    def group(gi, carry):
        js = [gi * ng + g for g in rng]
        lss = [pl.ds(pl.multiple_of(j * GDN_DK, GDN_DK), GDN_DK) for j in js]
        gr = [gcr_s[pl.ds(j, 1), :] for j in js]
        gcol = [_row_to_col(x) for x in gr]
        bcol = [_row_to_col(beta_ref[pl.ds(j, 1), :]) for j in js]
        glast = [x[:, c - 1:c] for x in gr]
        q = [q_ref[:, ls] for ls in lss]
        kk = [k_ref[:, ls] for ls in lss]
        vv = [v_ref[:, ls] for ls in lss]
        s = [s_ref[j] for j in js]
        eg = [jnp.exp(x) for x in gcol]
        rhs = [jnp.concatenate([vv[g] * bcol[g], kk[g] * bcol[g] * eg[g]], axis=1) for g in rng]
        qk = [_dot_nt(q[g], kk[g]) for g in rng]
        nr = [_dot(n_ref[js[g]], rhs[g]) for g in rng]
        qs = [_dot(q[g] * eg[g], s[g]) for g in rng]
        sol = [rhs[g] + nr[g] for g in rng]
        ws = [_dot(sol[g][:, GDN_DK:], s[g]) for g in rng]
        v_new = [sol[g][:, :GDN_DK] - ws[g] for g in rng]
        attn = [jnp.where(incl, qk[g] * jnp.exp(jnp.where(incl, gcol[g] - gr[g], 0.0)), 0.0) for g in rng]
        av = [_dot(attn[g], v_new[g]) for g in rng]
        kv = [_dot_tn(kk[g] * jnp.exp(glast[g] - gcol[g]), v_new[g]) for g in rng]
        for g in rng:
            o_ref[:, lss[g]] = qs[g] + av[g]
            s_ref[js[g]] = s[g] * jnp.exp(glast[g]) + kv[g]
        return carry

    lax.fori_loop(0, gh // ng, group, 0)

    @pl.when(ci == pl.num_programs(1) - 1)
    def _():
        sfin_ref[...] = s_ref[...]


def gdn_prompt_scan(q, k, v, g, beta):
    bsz, t, d = q.shape
    c = CHUNK
    nc = t // c
    gh = d // GDN_DK
    rows = lambda x: jnp.swapaxes(x.reshape(bsz, nc, c, gh), -1, -2)
    g_r, beta_r = rows(g), rows(beta)
    tokd = lambda: pl.BlockSpec((None, c, d), lambda bi, ci: (bi, ci, 0))
    hrow = lambda: pl.BlockSpec((None, None, gh, c), lambda bi, ci: (bi, ci, 0, 0))
    mat = lambda: pl.BlockSpec((None, None, gh, c, c), lambda bi, ci: (bi, ci, 0, 0, 0))
    lmat = pl.pallas_call(
        _gd1_kernel,
        grid=(bsz, nc),
        in_specs=[tokd(), hrow(), hrow()],
        out_specs=mat(),
        out_shape=jax.ShapeDtypeStruct((bsz, nc, gh, c, c), F32),
        scratch_shapes=[pltpu.VMEM((gh, c), F32)],
        compiler_params=_cparams("arbitrary", "arbitrary"),
        name="gdn_chunk_prep",
    )(k, g_r, beta_r)
    nmat = tri_inverse(lmat)
    o, sfin = pl.pallas_call(
        _gd3_kernel,
        grid=(bsz, nc),
        in_specs=[tokd(), tokd(), tokd(), hrow(), hrow(), mat()],
        out_specs=[tokd(), pl.BlockSpec((None, gh, GDN_DK, GDN_DK), lambda bi, ci: (bi, 0, 0, 0))],
        out_shape=[jax.ShapeDtypeStruct((bsz, t, d), F32),
                   jax.ShapeDtypeStruct((bsz, gh, GDN_DK, GDN_DK), F32)],
        scratch_shapes=[pltpu.VMEM((gh, GDN_DK, GDN_DK), F32), pltpu.VMEM((gh, c), F32)],
        compiler_params=_cparams("arbitrary", "arbitrary"),
        name="gdn_chunk_scan",
    )(q, k, v, g_r, beta_r, nmat)
    return o, sfin


def _gds_kernel(s_ref, q_ref, k_ref, v_ref, beta_ref, eg_ref, so_ref, o_ref):
    s = s_ref[...]
    n = s.shape[-1]
    eye = (lax.broadcasted_iota(jnp.int32, (n, n), 0) == lax.broadcasted_iota(jnp.int32, (n, n), 1))[None]
    kcol = jnp.sum(jnp.where(eye, k_ref[...], 0.0), axis=2, keepdims=True)
    qcol = jnp.sum(jnp.where(eye, q_ref[...], 0.0), axis=2, keepdims=True)
    eg = eg_ref[...]
    ks = jnp.sum(kcol * s, axis=1, keepdims=True)
    v_new = beta_ref[...] * (v_ref[...] - eg * ks)
    sn = s * eg + kcol * v_new
    so_ref[...] = sn
    o_ref[...] = jnp.sum(qcol * sn, axis=1, keepdims=True)


def gdn_decode_step(s0, q, k, v, beta, g):
    bsz, gh, dk, _ = s0.shape
    d = gh * dk
    rows = lambda x: x.reshape(bsz, gh, 1, dk)
    wide = lambda x: jnp.broadcast_to(x[:, :, None, None], (bsz, gh, 1, dk))
    vec = lambda: pl.BlockSpec((None, gh, 1, dk), lambda bi: (bi, 0, 0, 0))
    st = lambda: pl.BlockSpec((None, gh, dk, dk), lambda bi: (bi, 0, 0, 0))
    sn, o = pl.pallas_call(
        _gds_kernel,
        grid=(bsz,),
        in_specs=[st()] + [vec() for _ in range(5)],
        out_specs=[st(), vec()],
        out_shape=[jax.ShapeDtypeStruct(s0.shape, F32), jax.ShapeDtypeStruct((bsz, gh, 1, dk), F32)],
        compiler_params=_cparams("arbitrary"),
        name="gdn_decode",
    )(s0, rows(q), rows(k), rows(v), wide(beta), wide(jnp.exp(g)))
    return o.reshape(bsz, d), sn


def _rmsnorm(x, w):
    return x * lax.rsqrt(jnp.mean(x * x, axis=-1, keepdims=True) + RMS_EPS) * w


def _l2norm(x):
    return x * lax.rsqrt(jnp.sum(x * x, axis=-1, keepdims=True) + 1e-6)


def _merge(pair):
    p, s = pair
    kdim = p.shape[-1]
    return jnp.concatenate([p.reshape(-1, kdim), s.reshape(-1, kdim)], axis=0).astype(BF16)


def _split(y, like):
    p, s = like
    mp = p.shape[0] * p.shape[1]
    return y[:mp].reshape(p.shape[0], p.shape[1], -1), y[mp:].reshape(s.shape[0], s.shape[1], -1)


def _dense(pair, w, widx, n_out, tn, **kw):
    return _split(pmatmul(_merge(pair), w, widx, n_out=n_out, tn=tn, **kw), pair)


def _both(f, *pairs):
    return tuple(f(*xs) for xs in zip(*pairs))


def _rwkv_mix(h, shift_prev, s0, j, P):
    hp, hs = h
    bp, t, d = hp.shape
    nh = d // RWKV_N
    prev = (jnp.concatenate([jnp.zeros((bp, 1, d), F32), hp[:, :-1]], axis=1), shift_prev[:, None, :])
    xx = _both(lambda a, b: a - b, prev, h)
    mix = P['rw_mix'][j]
    xs = [_both(lambda a, b: a + b * mix[i], h, xx) for i in range(6)]
    tn = min(d, 512)
    r = _dense(xs[0], P['rw_w_rkv'], (j, 0), d, tn)
    k = _dense(xs[1], P['rw_w_rkv'], (j, 1), d, tn)
    v = _dense(xs[2], P['rw_w_rkv'], (j, 2), d, tn)
    lora = P['rw_w1'].shape[-1]
    wl = _dense(_both(jnp.tanh, _dense(xs[3], P['rw_w1'], (j,), lora, lora)), P['rw_w2'], (j,), d, tn)
    al = _dense(_dense(xs[4], P['rw_a1'], (j,), lora, lora), P['rw_a2'], (j,), d, tn)
    gl = P['rw_g1'].shape[-1]
    glp = -(-gl // LANES) * LANES
    g1 = jnp.pad(P['rw_g1'][j], ((0, 0), (0, glp - gl)))
    g2 = jnp.pad(P['rw_g2'][j], ((0, glp - gl), (0, 0)))
    g = _dense(_both(jax.nn.sigmoid, _dense(xs[5], g1, (), glp, glp)), g2, (), d, tn)

    def prep(r, k, v, wl, al):
        w_log = -jax.nn.softplus(-(P['rw_w0'][j] + wl)) - 0.5
        lw = -jnp.exp(w_log)
        a = jax.nn.sigmoid(P['rw_a0'][j] + al)
        shp = k.shape[:-1] + (nh, RWKV_N)
        kk = _l2norm((k * P['rw_k_k'][j]).reshape(shp)).reshape(k.shape)
        k_mod = k * (1.0 + (a - 1.0) * P['rw_k_a'][j])
        return lw, kk, k_mod, kk * a

    lw, kk, k_mod, b = zip(*[prep(*xs_) for xs_ in zip(r, k, v, wl, al)])
    yp, sp = rwkv_prompt_scan(lw[0], kk[0], r[0], k_mod[0], b[0], v[0])
    bs = hs.shape[0]
    hv = lambda x: x.reshape(bs, nh, RWKV_N)
    ys, ss = rwkv_decode_step(s0, hv(jnp.exp(lw[1])), hv(kk[1]), hv(b[1]), hv(k_mod[1]), hv(v[1]), hv(r[1]))
    y = (yp, ys.reshape(bs, 1, d))

    def post(y, r, k_mod, v, g):
        shp = y.shape[:-1] + (nh, RWKV_N)
        yh = y.reshape(shp)
        mu = jnp.mean(yh, axis=-1, keepdims=True)
        var = jnp.mean(jnp.square(yh - mu), axis=-1, keepdims=True)
        yn = ((yh - mu) * lax.rsqrt(var + RWKV_LN_EPS)).reshape(y.shape) * P['rw_ln_w'][j] + P['rw_ln_b'][j]
        bonus = jnp.sum(r.reshape(shp) * k_mod.reshape(shp) * P['rw_r_k'][j], axis=-1, keepdims=True) * v.reshape(shp)
        return (yn + bonus.reshape(y.shape)) * g

    yg = _both(post, y, r, k_mod, v, g)
    out = _dense(yg, P['rw_w_o'], (j,), d, tn)
    return out, (sp, ss), (hp[:, -1], hs[:, -1])


def _gdn_mix(h, conv_prev, s0, j, P):
    hp, hs = h
    bp, t, d = hp.shape
    bs = hs.shape[0]
    gh = d // GDN_DK
    cdim = 3 * d
    tn = min(d, 512)
    w_in = P['gd_w_in']
    main = _dense(h, w_in, (j,), 4 * d, tn)
    w_tail = jnp.pad(w_in[j][:, 4 * d:], ((0, 0), (0, LANES - 2 * gh)))
    tail = _dense(h, w_tail, (), LANES, LANES)
    conv_w = P['gd_conv_w'][j]

    def conv(qkv, buf):
        tt = qkv.shape[1]
        xp = jnp.concatenate([buf, qkv], axis=1)
        y = xp[:, 0:tt] * conv_w[0]
        for i in range(1, GDN_CONV):
            y = y + xp[:, i:i + tt] * conv_w[i]
        return y, xp[:, tt:]

    bufs = (jnp.zeros((bp, GDN_CONV - 1, cdim), F32), conv_prev)
    outs = []
    convs = []
    for grp in range(2):
        m, tl = main[grp], tail[grp]
        y, cnew = conv(m[..., :cdim], bufs[grp])
        convs.append(cnew)
        qkv = jax.nn.silu(y)
        lead = qkv.shape[:2]
        q = _l2norm(qkv[..., :d].reshape(lead + (gh, GDN_DK))).reshape(lead + (d,)) * (GDN_DK ** -0.5)
        k = _l2norm(qkv[..., d:2 * d].reshape(lead + (gh, GDN_DK))).reshape(lead + (d,))
        v = qkv[..., 2 * d:]
        beta = jax.nn.sigmoid(tl[..., :gh])
        g = -jnp.exp(P['gd_A_log'][j]) * jax.nn.softplus(tl[..., gh:2 * gh] + P['gd_dt_bias'][j])
        outs.append((q, k, v, g, beta, m[..., cdim:]))
    qp, kp, vp, gp, betap, zp = outs[0]
    qs, ks, vs, gs, betas, zs = outs[1]
    op, sp = gdn_prompt_scan(qp, kp, vp, gp, betap)
    os_, ss = gdn_decode_step(s0, qs[:, 0], ks[:, 0], vs[:, 0], betas[:, 0], gs[:, 0])
    o = (op, os_.reshape(bs, 1, d))

    def post(o, z):
        shp = o.shape[:-1] + (gh, GDN_DK)
        oh = o.reshape(shp)
        oh = oh * lax.rsqrt(jnp.mean(oh * oh, axis=-1, keepdims=True) + RMS_EPS) * P['gd_norm_w'][j]
        return oh.reshape(o.shape) * jax.nn.silu(z)

    og = _both(post, o, (zp, zs))
    out = _dense(og, P['gd_w_out'], (j,), d, tn)
    return out, (sp, ss), tuple(convs)


def kernel(x_prompt, x_sample, state_rwkv, state_rwkv_shift, state_gdn, state_gdn_conv, c_prompt, c_sample,
           w_ada, b_ada, norm1_w, norm2_w, rw_mix, rw_w_rkv, rw_w0, rw_w1, rw_w2, rw_a0, rw_a1, rw_a2,
           rw_g1, rw_g2, rw_k_k, rw_k_a, rw_r_k, rw_ln_w, rw_ln_b, rw_w_o, gd_w_in, gd_conv_w, gd_A_log,
           gd_dt_bias, gd_norm_w, gd_w_out, ffn_w_gate, ffn_w_up, ffn_w_down, final_norm_w):
    P = {
        'rw_mix': rw_mix, 'rw_w_rkv': rw_w_rkv, 'rw_w0': rw_w0, 'rw_w1': rw_w1, 'rw_w2': rw_w2,
        'rw_a0': rw_a0, 'rw_a1': rw_a1, 'rw_a2': rw_a2, 'rw_g1': rw_g1, 'rw_g2': rw_g2,
        'rw_k_k': rw_k_k, 'rw_k_a': rw_k_a, 'rw_r_k': rw_r_k.reshape(rw_r_k.shape[0], 1, -1, RWKV_N),
        'rw_ln_w': rw_ln_w, 'rw_ln_b': rw_ln_b,
        'rw_w_o': rw_w_o, 'gd_w_in': gd_w_in, 'gd_conv_w': gd_conv_w, 'gd_A_log': gd_A_log,
        'gd_dt_bias': gd_dt_bias, 'gd_norm_w': gd_norm_w, 'gd_w_out': gd_w_out,
    }
    bp, t, d = x_prompt.shape
    bs = x_sample.shape[0]
    depth = w_ada.shape[0]
    dff = ffn_w_gate.shape[-1]
    assert t % CHUNK == 0 and d % (2 * LANES) == 0

    c_all = jnp.concatenate([c_prompt, c_sample], axis=0)
    nb = c_all.shape[0]
    nbp = -(-nb // 16) * 16
    c_act = jnp.pad(jax.nn.silu(c_all), ((0, nbp - nb), (0, 0))).astype(BF16)

    x = (x_prompt, x_sample)
    new_rw_s, new_rw_shift, new_gd_s, new_gd_conv = [], [], [], []
    tn_d = min(d, 512)
    tn_f = _pick_tile(dff, 256, LANES)
    for layer in range(depth):
        mod = pmatmul(c_act, w_ada, (layer,), n_out=6 * d, tn=tn_d)[:nb] + b_ada[layer]
        mod = mod.reshape(nb, 6, 1, d)
        mods = (mod[:bp], mod[bp:])
        h = _both(lambda xx, m: _rmsnorm(xx, norm1_w[layer]) * (1.0 + m[:, 1]) + m[:, 0], x, mods)
        j = layer // 2
        if layer % 2 == 0:
            out, s_new, sh_new = _rwkv_mix(h, state_rwkv_shift[j], state_rwkv[j], j, P)
            new_rw_s.append(s_new)
            new_rw_shift.append(sh_new)
        else:
            out, s_new, cb_new = _gdn_mix(h, state_gdn_conv[j], state_gdn[j], j, P)
            new_gd_s.append(s_new)
            new_gd_conv.append(cb_new)
        x = _both(lambda xx, m, o: xx + m[:, 2] * o, x, mods, out)
        h = _both(lambda xx, m: _rmsnorm(xx, norm2_w[layer]) * (1.0 + m[:, 4]) + m[:, 3], x, mods)
        hm = _merge(h)
        act = pmatmul(hm, ffn_w_gate, (layer,), n_out=dff, tn=tn_f, w2=ffn_w_up, out_dtype=BF16)
        kh = dff // 2
        tn_o = min(d, 256)
        part = pmatmul(act, ffn_w_down, (layer,), n_out=d, tn=tn_o, k0=0, kk=kh)
        ffn = _split(pmatmul(act, ffn_w_down, (layer,), n_out=d, tn=tn_o, k0=kh, kk=kh, add=part), h)
        x = _both(lambda xx, m, o: xx + m[:, 5] * o, x, mods, ffn)
    y = _both(lambda xx: _rmsnorm(xx, final_norm_w), x)
    grp = lambda lst, i: jnp.stack([e[i] for e in lst])
    return (y[0], y[1],
            grp(new_rw_s, 0), grp(new_rw_shift, 0), grp(new_gd_s, 0), grp(new_gd_conv, 0),
            grp(new_rw_s, 1), grp(new_rw_shift, 1), grp(new_gd_s, 1), grp(new_gd_conv, 1))
```

```python
import functools

import jax
import jax.numpy as jnp
from jax import lax
from jax.experimental import pallas as pl
from jax.experimental.pallas import tpu as pltpu

F32, BF16 = jnp.float32, jnp.bfloat16

RMS_EPS = 1e-6
RWKV_LN_EPS = 64e-5
RWKV_N = 64
GDN_DK = 128
GDN_CONV = 4
CHUNK = 64
LANES = 128
SUBLANES = 8
GROUP = 4
VMEM_LIMIT = 50 * 2**20


def _cparams(*sem):
    return pltpu.CompilerParams(dimension_semantics=sem, vmem_limit_bytes=VMEM_LIMIT)


def _pick_tile(n, cap, mult):
    best = None
    for t in range(mult, min(n, cap) + 1, mult):
        if n % t == 0:
            best = t
    return best or n


def _group(n):
    g = GROUP
    while n % g:
        g //= 2
    return g


def _dot(a, b):
    return jnp.dot(a.astype(BF16), b.astype(BF16), preferred_element_type=F32)


def _dot_nt(a, b):
    return lax.dot_general(a.astype(BF16), b.astype(BF16), (((1,), (1,)), ((), ())),
                           preferred_element_type=F32)


def _dot_tn(a, b):
    return lax.dot_general(a.astype(BF16), b.astype(BF16), (((0,), (0,)), ((), ())),
                           preferred_element_type=F32)


def _split3(x):
    hi = x.astype(BF16)
    r1 = x - hi.astype(F32)
    mid = r1.astype(BF16)
    lo = (r1 - mid.astype(F32)).astype(BF16)
    return hi, mid, lo


def _tri(n, transpose=False):
    row = lax.broadcasted_iota(jnp.int32, (n, n), 0)
    col = lax.broadcasted_iota(jnp.int32, (n, n), 1)
    m = (row <= col) if transpose else (row >= col)
    return jnp.where(m, 1.0, 0.0).astype(BF16)


def _row_to_col(row):
    n = row.shape[1]
    eye = lax.broadcasted_iota(jnp.int32, (n, n), 0) == lax.broadcasted_iota(jnp.int32, (n, n), 1)
    return jnp.sum(jnp.where(eye, jnp.broadcast_to(row, (n, n)), 0.0), axis=1, keepdims=True)


def _segsum(x, width, exact):
    l = x.shape[1]
    gi = lax.broadcasted_iota(jnp.int32, (LANES, LANES), 0) // width
    gj = lax.broadcasted_iota(jnp.int32, (LANES, LANES), 1) // width
    bd = jnp.where(gi == gj, 1.0, 0.0).astype(BF16)
    hi = x.astype(BF16)
    lo = (x - hi.astype(F32)).astype(BF16) if exact else None
    outs = []
    for t in range(l // LANES):
        sl = slice(t * LANES, (t + 1) * LANES)
        s = jnp.dot(hi[:, sl], bd, preferred_element_type=F32)
        if exact:
            s = s + jnp.dot(lo[:, sl], bd, preferred_element_type=F32)
        outs.append(s)
    return outs[0] if len(outs) == 1 else jnp.concatenate(outs, axis=1)


def _modnorm(x, nw, scale, shift):
    y = x * lax.rsqrt(jnp.mean(x * x, axis=-1, keepdims=True) + RMS_EPS) * nw
    return y * (1.0 + scale) + shift


def _mm_kernel(*refs, n_w, has_add, act):
    a_ref = refs[0]
    w_refs = refs[1:1 + n_w]
    pos = 1 + n_w
    add_ref = refs[pos] if has_add else None
    pos += int(has_add)
    o_ref = refs[pos]
    wbf = refs[pos + 1:pos + 1 + n_w]

    @pl.when(pl.program_id(1) == 0)
    def _():
        for w, s in zip(w_refs, wbf):
            s[...] = w[...].astype(BF16)

    a = a_ref[...]
    y = jnp.dot(a, wbf[0][...], preferred_element_type=F32)
    if n_w == 2:
        u = jnp.dot(a, wbf[1][...], preferred_element_type=F32)
        y = y * jax.nn.sigmoid(y) * u
    if has_add:
        y = y + add_ref[...]
    if act == "tanh":
        y = jnp.tanh(y)
    elif act == "sigmoid":
        y = jax.nn.sigmoid(y)
    o_ref[...] = y.astype(o_ref.dtype)


def pmatmul(a, w, widx=(), *, n_out, tn, k0=0, kk=None, w2=None, add=None, act=None,
            out_dtype=F32, tm_cap=832):
    m = a.shape[0]
    kk = a.shape[1] if kk is None else kk
    kb = k0 // kk
    tm = _pick_tile(m, tm_cap, 16)
    ws = [w] if w2 is None else [w, w2]
    nlead = len(widx)
    grid = (n_out // tn, m // tm)
    in_specs = [pl.BlockSpec((tm, kk), lambda j, i: (i, kb))]
    for _ in ws:
        in_specs.append(pl.BlockSpec((None,) * nlead + (kk, tn),
                                     lambda j, i: tuple(widx) + (kb, j)))
    args = [a] + ws
    if add is not None:
        in_specs.append(pl.BlockSpec((tm, tn), lambda j, i: (i, j)))
        args.append(add)
    return pl.pallas_call(
        functools.partial(_mm_kernel, n_w=len(ws), has_add=add is not None, act=act),
        grid=grid,
        in_specs=in_specs,
        out_specs=pl.BlockSpec((tm, tn), lambda j, i: (i, j)),
        out_shape=jax.ShapeDtypeStruct((m, n_out), out_dtype),
        scratch_shapes=[pltpu.VMEM((kk, tn), BF16) for _ in ws],
        compiler_params=_cparams("arbitrary", "arbitrary"),
        name=f"mm_m{m}_k{kk}_n{n_out}" + ("_glu" if w2 is not None else "") + ("_add" if add is not None else ""),
    )(*args)


def _tile_specs(tr, d, np_tiles, tps, nbp):
    tok = lambda: pl.BlockSpec((tr, d), lambda i: (i, 0))
    seq = lambda n: pl.BlockSpec((None, n, d), lambda i: (jnp.minimum(i // tps, nbp - 1), 0, 0))
    samp = lambda: pl.BlockSpec((tr, d), lambda i: (0, 0))
    row = lambda n=1: pl.BlockSpec((n, d), lambda i: (0, 0))
    return tok, seq, samp, row


def _pre_rwkv_kernel(x_ref, mp_ref, scs_ref, shs_ref, nw_ref, mix_ref, prev_s_ref,
                     o0, o1, o2, o3, o4, o5, hlast_ref, hs_ref, carry_ref, *, np_tiles, tps):
    i = pl.program_id(0)
    is_s = i == np_tiles
    shift = jnp.where(is_s, shs_ref[...], mp_ref[0:1, :])
    scale = jnp.where(is_s, scs_ref[...], mp_ref[1:2, :])
    h = _modnorm(x_ref[...], nw_ref[...], scale, shift)
    tr = h.shape[0]

    @pl.when(i % tps == 0)
    def _():
        carry_ref[...] = jnp.zeros_like(carry_ref)

    rowi = lax.broadcasted_iota(jnp.int32, h.shape, 0)
    prev_p = jnp.where(rowi == 0, carry_ref[...], pltpu.roll(h, 1, axis=0))
    prev = jnp.where(is_s, prev_s_ref[...], prev_p)
    xx = prev - h
    for n, o in enumerate((o0, o1, o2, o3, o4, o5)):
        o[...] = (h + xx * mix_ref[n:n + 1, :]).astype(BF16)
    last = h[tr - 1:tr, :]
    carry_ref[...] = last

    @pl.when(jnp.logical_not(is_s))
    def _():
        hlast_ref[...] = last

    @pl.when(is_s)
    def _():
        hs_ref[...] = h


def pre_rwkv(x, modp, scale_s, shift_s, nw, mix, prev_s, t):
    m, d = x.shape
    tr = scale_s.shape[0]
    nbp = modp.shape[0]
    tps = t // tr
    np_tiles = nbp * tps
    tok, seq, samp, row = _tile_specs(tr, d, np_tiles, tps, nbp)
    outs = pl.pallas_call(
        functools.partial(_pre_rwkv_kernel, np_tiles=np_tiles, tps=tps),
        grid=(np_tiles + 1,),
        in_specs=[tok(), seq(2), samp(), samp(), row(), row(6), samp()],
        out_specs=[tok() for _ in range(6)] + [seq(1), samp()],
        out_shape=[jax.ShapeDtypeStruct((m, d), BF16)] * 6 + [jax.ShapeDtypeStruct((nbp, 1, d), F32),
                                                              jax.ShapeDtypeStruct((tr, d), F32)],
        scratch_shapes=[pltpu.VMEM((1, d), F32)],
        compiler_params=_cparams("arbitrary"),
        name="pre_rwkv",
    )(x, modp, scale_s, shift_s, nw.reshape(1, d), mix, prev_s)
    return outs[:6], outs[6], outs[7]


def _resnorm_kernel(x_ref, dl_ref, mp_ref, gs_ref, scs_ref, shs_ref, nw_ref, xo_ref, *h_refs, np_tiles):
    is_s = pl.program_id(0) == np_tiles
    gate = jnp.where(is_s, gs_ref[...], mp_ref[0:1, :])
    xn = x_ref[...] + gate * dl_ref[...]
    xo_ref[...] = xn
    if h_refs:
        scale = jnp.where(is_s, scs_ref[...], mp_ref[1:2, :])
        shift = jnp.where(is_s, shs_ref[...], mp_ref[2:3, :])
        h_refs[0][...] = _modnorm(xn, nw_ref[...], scale, shift).astype(BF16)


def resnorm(x, delta, modp, gate_s, scale_s, shift_s, nw, t, emit_h=True):
    m, d = x.shape
    tr = gate_s.shape[0]
    nbp = modp.shape[0]
    tps = t // tr
    np_tiles = nbp * tps
    tok, seq, samp, row = _tile_specs(tr, d, np_tiles, tps, nbp)
    outs = pl.pallas_call(
        functools.partial(_resnorm_kernel, np_tiles=np_tiles),
        grid=(np_tiles + 1,),
        in_specs=[tok(), tok(), seq(3), samp(), samp(), samp(), row()],
        out_specs=[tok()] + ([tok()] if emit_h else []),
        out_shape=[jax.ShapeDtypeStruct((m, d), F32)] + ([jax.ShapeDtypeStruct((m, d), BF16)] if emit_h else []),
        compiler_params=_cparams("arbitrary"),
        name="resnorm",
    )(x, delta, modp, gate_s, scale_s, shift_s, nw.reshape(1, d))
    return (outs[0], outs[1]) if emit_h else (outs[0], None)


def _final_kernel(x_ref, dl_ref, mp_ref, gs_ref, nw_ref, yp_ref, ys_ref, *, np_tiles):
    is_s = pl.program_id(0) == np_tiles
    gate = jnp.where(is_s, gs_ref[...], mp_ref[0:1, :])
    xn = x_ref[...] + gate * dl_ref[...]
    y = xn * lax.rsqrt(jnp.mean(xn * xn, axis=-1, keepdims=True) + RMS_EPS) * nw_ref[...]

    @pl.when(jnp.logical_not(is_s))
    def _():
        yp_ref[...] = y

    @pl.when(is_s)
    def _():
        ys_ref[...] = y


def final_norm(x, delta, modp, gate_s, nw, t):
    m, d = x.shape
    tr = gate_s.shape[0]
    nbp = modp.shape[0]
    tps = t // tr
    np_tiles = nbp * tps
    tok, seq, samp, row = _tile_specs(tr, d, np_tiles, tps, nbp)
    return pl.pallas_call(
        functools.partial(_final_kernel, np_tiles=np_tiles),
        grid=(np_tiles + 1,),
        in_specs=[tok(), tok(), seq(1), samp(), row()],
        out_specs=[pl.BlockSpec((tr, d), lambda i: (jnp.minimum(i, np_tiles - 1), 0)), samp()],
        out_shape=[jax.ShapeDtypeStruct((np_tiles * tr, d), F32), jax.ShapeDtypeStruct((tr, d), F32)],
        compiler_params=_cparams("arbitrary"),
        name="final_norm",
    )(x, delta, modp, gate_s, nw.reshape(1, d))


def _gdn_conv_kernel(x_ref, w_ref, buf_ref, o_ref, prev_ref, *, np_tiles, tps):
    sec = pl.program_id(0)
    i = pl.program_id(1)
    is_s = i == np_tiles

    @pl.when(i % tps == 0)
    def _():
        prev_ref[...] = jnp.zeros_like(prev_ref)

    x = x_ref[...]
    prev = prev_ref[...]
    rowi = lax.broadcasted_iota(jnp.int32, x.shape, 0)

    def back(k):
        same_seq = jnp.where(rowi >= k, pltpu.roll(x, k, axis=0), pltpu.roll(prev, k, axis=0))
        return jnp.where(is_s, buf_ref[GDN_CONV - 1 - k], same_seq)

    y = x * w_ref[GDN_CONV - 1:GDN_CONV, :]
    for k in range(1, GDN_CONV):
        y = y + back(k) * w_ref[GDN_CONV - 1 - k:GDN_CONV - k, :]
    prev_ref[...] = x
    a = y * jax.nn.sigmoid(y)
    nrm = a * lax.rsqrt(_segsum(a * a, GDN_DK, exact=False) + 1e-6)
    o_ref[...] = jnp.where(sec == 0, nrm * (GDN_DK ** -0.5), jnp.where(sec == 1, nrm, a))


def gdn_conv(main, conv_w, bufs, nbp, t):
    m = main.shape[0]
    d = main.shape[1] // 4
    tr = bufs.shape[1]
    tps = t // tr
    np_tiles = nbp * tps
    return pl.pallas_call(
        functools.partial(_gdn_conv_kernel, np_tiles=np_tiles, tps=tps),
        grid=(3, np_tiles + 1),
        in_specs=[pl.BlockSpec((tr, d), lambda s, i: (i, s)),
                  pl.BlockSpec((GDN_CONV, d), lambda s, i: (0, s)),
                  pl.BlockSpec((GDN_CONV - 1, tr, d), lambda s, i: (0, 0, s))],
        out_specs=pl.BlockSpec((None, tr, d), lambda s, i: (s, i, 0)),
        out_shape=jax.ShapeDtypeStruct((3, m, d), F32),
        scratch_shapes=[pltpu.VMEM((tr, d), F32)],
        compiler_params=_cparams("arbitrary", "arbitrary"),
        name="gdn_conv",
    )(main, conv_w, bufs)


def _inv_kernel(m_ref, n_ref, *, c):
    nblk = c // SUBLANES
    zeros = jnp.zeros((SUBLANES * c, LANES), F32)
    for ib in range(nblk):
        n_ref[ib * SUBLANES * c:(ib + 1) * SUBLANES * c, :] = zeros

        def row(ii, carry, ib=ib):
            base = pl.multiple_of((ib * SUBLANES + ii) * c, c)
            acc = [m_ref[pl.ds(base + SUBLANES * k, SUBLANES), :] for k in range(ib + 1)]
            for j in range((ib + 1) * SUBLANES):
                coef = m_ref[pl.ds(base + j, 1), :]
                for k in range(j // SUBLANES + 1):
                    acc[k] = acc[k] + coef * n_ref[j * c + SUBLANES * k:j * c + SUBLANES * (k + 1), :]
            for k in range(ib + 1):
                n_ref[pl.ds(base + SUBLANES * k, SUBLANES), :] = -acc[k]
            return carry

        lax.fori_loop(0, SUBLANES, row, 0)


def tri_inverse(mats):
    shape = mats.shape
    c = shape[-1]
    u = 1
    for s in shape[:-2]:
        u *= s
    up = -(-u // LANES) * LANES
    flat = mats.reshape(u, c * c).T
    if up != u:
        flat = jnp.pad(flat, ((0, 0), (0, up - u)))
    out = pl.pallas_call(
        functools.partial(_inv_kernel, c=c),
        grid=(up // LANES,),
        in_specs=[pl.BlockSpec((c * c, LANES), lambda g: (0, g))],
        out_specs=pl.BlockSpec((c * c, LANES), lambda g: (0, g)),
        out_shape=jax.ShapeDtypeStruct((c * c, up), F32),
        compiler_params=_cparams("arbitrary"),
        name="tri_inverse",
    )(flat)
    return out[:, :u].T.reshape(shape)


def _rwkv_token_prep(r, k, wl, al, w0, a0, k_k, k_a, r_k):
    w_log = -jax.nn.softplus(-(w0 + wl)) - 0.5
    lw = -jnp.exp(w_log)
    a = jax.nn.sigmoid(a0 + al)
    kx = k * k_k
    kap = kx * lax.rsqrt(_segsum(kx * kx, RWKV_N, exact=False) + 1e-6)
    k_mod = k * (1.0 + (a - 1.0) * k_a)
    rk = _segsum(r * k_mod * r_k, RWKV_N, exact=True)
    return lw, kap, k_mod, kap * a, rk


def _rw1_kernel(r_ref, k_ref, v_ref, wl_ref, al_ref, w0_ref, a0_ref, kk_ref, ka_ref, rk_ref,
                kaph_ref, rh_ref, kh_ref, bh_ref, khp_ref, bhp_ref, vb_ref, bonus_ref, pc_ref, mb_ref):
    v = v_ref[...]
    lw, kap, k, b, rk = _rwkv_token_prep(r_ref[...], k_ref[...], wl_ref[...], al_ref[...], w0_ref[...],
                                         a0_ref[...], kk_ref[...], ka_ref[...], rk_ref[...])
    bonus_ref[...] = rk * v
    vb_ref[...] = v.astype(BF16)
    c, lb = lw.shape
    tri = _tri(c)
    hi, mid, lo = _split3(lw)
    p = (jnp.dot(tri, hi, preferred_element_type=F32) + jnp.dot(tri, mid, preferred_element_type=F32)
         + jnp.dot(tri, lo, preferred_element_type=F32))
    pc = p[c - 1:c, :]
    en = jnp.exp(-p)
    ec = jnp.exp(pc - p)
    kaph = kap * jnp.exp(p - lw)
    bh = (b * en).astype(BF16)
    kaph_ref[...] = kaph.astype(BF16)
    rh_ref[...] = (r_ref[...] * jnp.exp(p)).astype(BF16)
    kh_ref[...] = (k * en).astype(BF16)
    bh_ref[...] = bh
    khp_ref[...] = (k * ec).astype(BF16)
    bhp_ref[...] = (b * ec).astype(BF16)
    pc_ref[...] = jnp.exp(pc)
    lane = lax.broadcasted_iota(jnp.int32, (c, LANES), 1)
    m0 = lane < RWKV_N
    rowi = lax.broadcasted_iota(jnp.int32, (2 * c, c), 0) % c
    coli = lax.broadcasted_iota(jnp.int32, (2 * c, c), 1)
    strict = rowi > coli
    for jt in range(lb // LANES):
        kp = kaph[:, jt * LANES:(jt + 1) * LANES]
        bp = bh[:, jt * LANES:(jt + 1) * LANES]
        lhs = jnp.concatenate([jnp.where(m0, kp, 0.0), jnp.where(m0, 0.0, kp)], axis=0)
        g = _dot_nt(lhs, bp)
        g = jnp.where(strict, g, 0.0)
        mb_ref[2 * jt] = g[:c]
        mb_ref[2 * jt + 1] = g[c:]


def _rw3_kernel(kaph_ref, rh_ref, kh_ref, bh_ref, khp_ref, bhp_ref, v_ref, pc_ref, n_ref,
                bonus_ref, g_ref, lnw_ref, lnb_ref, yg_ref, sfin_ref, a_ref):
    ci = pl.program_id(1)
    c = kaph_ref.shape[0]
    npair = a_ref.shape[0]

    @pl.when(ci == 0)
    def _():
        a_ref[...] = jnp.zeros_like(a_ref)

    lane2 = lax.broadcasted_iota(jnp.int32, (2 * c, LANES), 1)
    m0_2 = lane2 < RWKV_N
    m0 = lax.broadcasted_iota(jnp.int32, (c, LANES), 1) < RWKV_N
    row = lax.broadcasted_iota(jnp.int32, (c, c), 0)
    col = lax.broadcasted_iota(jnp.int32, (c, c), 1)
    strict = row > col
    incl = row >= col
    rr = lax.broadcasted_iota(jnp.int32, (LANES, LANES), 0)
    cc = lax.broadcasted_iota(jnp.int32, (LANES, LANES), 1)
    blockdiag = (rr < RWKV_N) == (cc < RWKV_N)
    ng = _group(npair)
    halves = (0, 1)

    def group(gi, carry):
        js = [gi * ng + g for g in range(ng)]
        lss = [pl.ds(pl.multiple_of(j * LANES, LANES), LANES) for j in js]
        lhs = [jnp.concatenate([kaph_ref[:, ls], rh_ref[:, ls]], axis=0) for ls in lss]
        vv = [v_ref[:, ls] for ls in lss]
        a0 = [a_ref[j] for j in js]
        x = [_dot(l, a) for l, a in zip(lhs, a0)]
        lhs_m = []
        for l in lhs:
            l32 = l.astype(F32)
            lhs_m.append((jnp.where(m0_2, l32, 0.0).astype(BF16), jnp.where(m0_2, 0.0, l32).astype(BF16)))
        gk = [[_dot_nt(lhs_m[g][h], kh_ref[:, lss[g]]) for h in halves] for g in range(ng)]
        gb = [[_dot_nt(lhs_m[g][h][c:], bh_ref[:, lss[g]]) for h in halves] for g in range(ng)]
        mk = [[jnp.where(strict, gk[g][h][:c], 0.0).astype(BF16) for h in halves] for g in range(ng)]
        lrk = [[jnp.where(incl, gk[g][h][c:], 0.0).astype(BF16) for h in halves] for g in range(ng)]
        lrb = [[jnp.where(incl, gb[g][h], 0.0).astype(BF16) for h in halves] for g in range(ng)]
        mkv = [[_dot(mk[g][h], vv[g]) for h in halves] for g in range(ng)]
        lrkv = [[_dot(lrk[g][h], vv[g]) for h in halves] for g in range(ng)]
        kv = [_dot_tn(khp_ref[:, lss[g]], vv[g]) for g in range(ng)]
        rhs = [[x[g][:c] + mkv[g][h] for h in halves] for g in range(ng)]
        nr = [[_dot(n_ref[2 * js[g] + h], rhs[g][h]) for h in halves] for g in range(ng)]
        uh = [[rhs[g][h] + nr[g][h] for h in halves] for g in range(ng)]
        u = [jnp.where(m0, uh[g][0], uh[g][1]) for g in range(ng)]
        lu = [[_dot(lrb[g][h], uh[g][h]) for h in halves] for g in range(ng)]
        bu = [_dot_tn(bhp_ref[:, lss[g]], u[g]) for g in range(ng)]
        y = [x[g][c:] + jnp.where(m0, lrkv[g][0] - lu[g][0], lrkv[g][1] - lu[g][1]) for g in range(ng)]
        mu = [_segsum(y[g], RWKV_N, exact=True) * (1.0 / RWKV_N) for g in range(ng)]
        yc = [y[g] - mu[g] for g in range(ng)]
        var = [_segsum(yc[g] * yc[g], RWKV_N, exact=False) * (1.0 / RWKV_N) for g in range(ng)]
        for g in range(ng):
            ls = lss[g]
            yn = yc[g] * lax.rsqrt(var[g] + RWKV_LN_EPS) * lnw_ref[:, ls] + lnb_ref[:, ls]
            yg_ref[:, ls] = ((yn + bonus_ref[:, ls]) * g_ref[:, ls]).astype(BF16)
            pcc = _row_to_col(pc_ref[:, ls])
            a_ref[js[g]] = pcc * a0[g] + jnp.where(blockdiag, kv[g] - bu[g], 0.0)
        return carry

    lax.fori_loop(0, npair // ng, group, 0)

    @pl.when(ci == pl.num_programs(1) - 1)
    def _():
        sfin_ref[...] = a_ref[...]


def rwkv_prompt_scan(r, k, v, wl, al, g, w0, a0, k_k, k_a, r_k, ln_w, ln_b, bsz, t):
    d = r.shape[1]
    c = CHUNK
    nc = t // c
    h = d // RWKV_N
    lb = min(d, 1024)
    row2 = lambda x: x.reshape(1, d)
    tok2 = lambda: pl.BlockSpec((c, lb), lambda bi, ci, li: (bi * nc + ci, li))
    prm = lambda: pl.BlockSpec((1, lb), lambda bi, ci, li: (0, li))
    tok = lambda: pl.BlockSpec((None, c, lb), lambda bi, ci, li: (bi, ci, li))
    bf = jax.ShapeDtypeStruct((bsz, t, d), BF16)
    kaph, rh, kh, bh, khp, bhp, vb, bonus, pc, mb = pl.pallas_call(
        _rw1_kernel,
        grid=(bsz, nc, d // lb),
        in_specs=[tok2() for _ in range(5)] + [prm() for _ in range(5)],
        out_specs=[tok() for _ in range(8)] + [
            pl.BlockSpec((None, None, 1, lb), lambda bi, ci, li: (bi, ci, 0, li)),
            pl.BlockSpec((None, None, lb // RWKV_N, c, c), lambda bi, ci, li: (bi, ci, li, 0, 0)),
        ],
        out_shape=[bf] * 7 + [jax.ShapeDtypeStruct((bsz, t, d), F32),
                              jax.ShapeDtypeStruct((bsz, nc, 1, d), F32),
                              jax.ShapeDtypeStruct((bsz, nc, h, c, c), F32)],
        compiler_params=_cparams("arbitrary", "arbitrary", "arbitrary"),
        name="rwkv_chunk_prep",
    )(r, k, v, wl, al, row2(w0), row2(a0), row2(k_k), row2(k_a), row2(r_k))
    nmat = tri_inverse(mb)
    tokd = lambda: pl.BlockSpec((None, c, d), lambda bi, ci: (bi, ci, 0))
    tok2d = lambda: pl.BlockSpec((c, d), lambda bi, ci: (bi * nc + ci, 0))
    prmd = lambda: pl.BlockSpec((1, d), lambda bi, ci: (0, 0))
    yg, sfin = pl.pallas_call(
        _rw3_kernel,
        grid=(bsz, nc),
        in_specs=[tokd() for _ in range(7)] + [
            pl.BlockSpec((None, None, 1, d), lambda bi, ci: (bi, ci, 0, 0)),
            pl.BlockSpec((None, None, h, c, c), lambda bi, ci: (bi, ci, 0, 0, 0)),
            tokd(), tok2d(), prmd(), prmd(),
        ],
        out_specs=[tok2d(), pl.BlockSpec((None, h // 2, LANES, LANES), lambda bi, ci: (bi, 0, 0, 0))],
        out_shape=[jax.ShapeDtypeStruct((bsz * t, d), BF16),
                   jax.ShapeDtypeStruct((bsz, h // 2, LANES, LANES), F32)],
        scratch_shapes=[pltpu.VMEM((h // 2, LANES, LANES), F32)],
        compiler_params=_cparams("arbitrary", "arbitrary"),
        name="rwkv_chunk_scan",
    )(kaph, rh, kh, bh, khp, bhp, vb, pc, nmat, bonus, g, row2(ln_w), row2(ln_b))
    n = RWKV_N
    s_even = sfin[:, :, :n, :n]
    s_odd = sfin[:, :, n:, n:]
    s = jnp.stack([s_even, s_odd], axis=2).reshape(bsz, h, n, n)
    return yg, jnp.swapaxes(s, -1, -2)


def _rws_kernel(s_ref, w_ref, kap_ref, b_ref, k_ref, v_ref, r_ref, so_ref, y_ref):
    s = s_ref[...]
    n = s.shape[-1]
    eye = (lax.broadcasted_iota(jnp.int32, (n, n), 0) == lax.broadcasted_iota(jnp.int32, (n, n), 1))[None]
    sa = -jnp.sum(s * kap_ref[...], axis=2, keepdims=True)
    vcol = jnp.sum(jnp.where(eye, v_ref[...], 0.0), axis=2, keepdims=True)
    sn = s * w_ref[...] + sa * b_ref[...] + vcol * k_ref[...]
    so_ref[...] = sn
    ycol = jnp.sum(sn * r_ref[...], axis=2, keepdims=True)
    y_ref[...] = jnp.sum(jnp.where(eye, ycol, 0.0), axis=1, keepdims=True)


def rwkv_decode_step(s0, w, kap, b, k, v, r):
    bsz, h, n, _ = s0.shape
    vec = lambda: pl.BlockSpec((None, h, 1, n), lambda bi: (bi, 0, 0, 0))
    st = lambda: pl.BlockSpec((None, h, n, n), lambda bi: (bi, 0, 0, 0))
    rows = lambda x: x.reshape(bsz, h, 1, n)
    sn, y = pl.pallas_call(
        _rws_kernel,
        grid=(bsz,),
        in_specs=[st()] + [vec() for _ in range(6)],
        out_specs=[st(), vec()],
        out_shape=[jax.ShapeDtypeStruct(s0.shape, F32), jax.ShapeDtypeStruct((bsz, h, 1, n), F32)],
        compiler_params=_cparams("arbitrary"),
        name="rwkv_decode",
    )(s0, rows(w), rows(kap), rows(b), rows(k), rows(v), rows(r))
    return y.reshape(bsz, h, n), sn


def _gd_head_scalars(g_ref, gcr_s, tri_t):
    hi, mid, lo = _split3(g_ref[...])
    gcr_s[...] = (jnp.dot(hi, tri_t, preferred_element_type=F32)
                  + jnp.dot(mid, tri_t, preferred_element_type=F32)
                  + jnp.dot(lo, tri_t, preferred_element_type=F32))


def _gd1_kernel(k_ref, g_ref, beta_ref, l_ref, gcr_s):
    c = k_ref.shape[0]
    gh = g_ref.shape[0]
    _gd_head_scalars(g_ref, gcr_s, _tri(c, transpose=True))
    row = lax.broadcasted_iota(jnp.int32, (c, c), 0)
    col = lax.broadcasted_iota(jnp.int32, (c, c), 1)
    strict = row > col
    ng = _group(gh)

    def group(gi, carry):
        js = [gi * ng + g for g in range(ng)]
        gr = [gcr_s[pl.ds(j, 1), :] for j in js]
        gcol = [_row_to_col(x) for x in gr]
        bcol = [_row_to_col(beta_ref[pl.ds(j, 1), :]) for j in js]
        kk = [k_ref[:, pl.ds(pl.multiple_of(j * GDN_DK, GDN_DK), GDN_DK)] for j in js]
        gram = [_dot_nt(kk[g] * bcol[g], kk[g]) for g in range(ng)]
        for g in range(ng):
            dec = jnp.exp(jnp.where(strict, gcol[g] - gr[g], 0.0))
            l_ref[js[g]] = jnp.where(strict, gram[g] * dec, 0.0)
        return carry

    lax.fori_loop(0, gh // ng, group, 0)


def _gd3_kernel(q_ref, k_ref, v_ref, g_ref, beta_ref, n_ref, z_ref, nw_ref, og_ref, sfin_ref, s_ref, gcr_s):
    ci = pl.program_id(1)
    c = k_ref.shape[0]
    gh = g_ref.shape[0]

    @pl.when(ci == 0)
    def _():
        s_ref[...] = jnp.zeros_like(s_ref)

    _gd_head_scalars(g_ref, gcr_s, _tri(c, transpose=True))
    row = lax.broadcasted_iota(jnp.int32, (c, c), 0)
    col = lax.broadcasted_iota(jnp.int32, (c, c), 1)
    incl = row >= col
    ng = _group(gh)
    rng = range(ng)

    def group(gi, carry):
        js = [gi * ng + g for g in rng]
        lss = [pl.ds(pl.multiple_of(j * GDN_DK, GDN_DK), GDN_DK) for j in js]
        gr = [gcr_s[pl.ds(j, 1), :] for j in js]
        gcol = [_row_to_col(x) for x in gr]
        bcol = [_row_to_col(beta_ref[pl.ds(j, 1), :]) for j in js]
        glast = [x[:, c - 1:c] for x in gr]
        q = [q_ref[:, ls] for ls in lss]
        kk = [k_ref[:, ls] for ls in lss]
        vv = [v_ref[:, ls] for ls in lss]
        s = [s_ref[j] for j in js]
        eg = [jnp.exp(x) for x in gcol]
        rhs = [jnp.concatenate([vv[g] * bcol[g], kk[g] * bcol[g] * eg[g]], axis=1) for g in rng]
        qk = [_dot_nt(q[g], kk[g]) for g in rng]
        nr = [_dot(n_ref[js[g]], rhs[g]) for g in rng]
        qs = [_dot(q[g] * eg[g], s[g]) for g in rng]
        sol = [rhs[g] + nr[g] for g in rng]
        ws = [_dot(sol[g][:, GDN_DK:], s[g]) for g in rng]
        v_new = [sol[g][:, :GDN_DK] - ws[g] for g in rng]
        attn = [jnp.where(incl, qk[g] * jnp.exp(jnp.where(incl, gcol[g] - gr[g], 0.0)), 0.0) for g in rng]
        av = [_dot(attn[g], v_new[g]) for g in rng]
        kv = [_dot_tn(kk[g] * jnp.exp(glast[g] - gcol[g]), v_new[g]) for g in rng]
        for g in rng:
            o = qs[g] + av[g]
            on = o * lax.rsqrt(jnp.mean(o * o, axis=-1, keepdims=True) + RMS_EPS) * nw_ref[...]
            z = z_ref[:, lss[g]]
            og_ref[:, lss[g]] = (on * (z * jax.nn.sigmoid(z))).astype(BF16)
            s_ref[js[g]] = s[g] * jnp.exp(glast[g]) + kv[g]
        return carry

    lax.fori_loop(0, gh // ng, group, 0)

    @pl.when(ci == pl.num_programs(1) - 1)
    def _():
        sfin_ref[...] = s_ref[...]


def gdn_prompt_scan(qkv, main, g, beta, norm_w, bsz, t):
    d = qkv.shape[2]
    c = CHUNK
    nc = t // c
    gh = d // GDN_DK
    rows = lambda x: jnp.swapaxes(x.reshape(bsz, nc, c, gh), -1, -2)
    g_r, beta_r = rows(g), rows(beta)
    sect = lambda s: pl.BlockSpec((None, c, d), lambda bi, ci: (s, bi * nc + ci, 0))
    hrow = lambda: pl.BlockSpec((None, None, gh, c), lambda bi, ci: (bi, ci, 0, 0))
    mat = lambda: pl.BlockSpec((None, None, gh, c, c), lambda bi, ci: (bi, ci, 0, 0, 0))
    lmat = pl.pallas_call(
        _gd1_kernel,
        grid=(bsz, nc),
        in_specs=[sect(1), hrow(), hrow()],
        out_specs=mat(),
        out_shape=jax.ShapeDtypeStruct((bsz, nc, gh, c, c), F32),
        scratch_shapes=[pltpu.VMEM((gh, c), F32)],
        compiler_params=_cparams("arbitrary", "arbitrary"),
        name="gdn_chunk_prep",
    )(qkv, g_r, beta_r)
    nmat = tri_inverse(lmat)
    og, sfin = pl.pallas_call(
        _gd3_kernel,
        grid=(bsz, nc),
        in_specs=[sect(0), sect(1), sect(2), hrow(), hrow(), mat(),
                  pl.BlockSpec((c, d), lambda bi, ci: (bi * nc + ci, 3)),
                  pl.BlockSpec((1, GDN_DK), lambda bi, ci: (0, 0))],
        out_specs=[pl.BlockSpec((c, d), lambda bi, ci: (bi * nc + ci, 0)),
                   pl.BlockSpec((None, gh, GDN_DK, GDN_DK), lambda bi, ci: (bi, 0, 0, 0))],
        out_shape=[jax.ShapeDtypeStruct((bsz * t, d), BF16),
                   jax.ShapeDtypeStruct((bsz, gh, GDN_DK, GDN_DK), F32)],
        scratch_shapes=[pltpu.VMEM((gh, GDN_DK, GDN_DK), F32), pltpu.VMEM((gh, c), F32)],
        compiler_params=_cparams("arbitrary", "arbitrary"),
        name="gdn_chunk_scan",
    )(qkv, qkv, qkv, g_r, beta_r, nmat, main, norm_w.reshape(1, GDN_DK))
    return og, sfin


def _gds_kernel(s_ref, q_ref, k_ref, v_ref, beta_ref, eg_ref, so_ref, o_ref):
    s = s_ref[...]
    n = s.shape[-1]
    eye = (lax.broadcasted_iota(jnp.int32, (n, n), 0) == lax.broadcasted_iota(jnp.int32, (n, n), 1))[None]
    kcol = jnp.sum(jnp.where(eye, k_ref[...], 0.0), axis=2, keepdims=True)
    qcol = jnp.sum(jnp.where(eye, q_ref[...], 0.0), axis=2, keepdims=True)
    eg = eg_ref[...]
    ks = jnp.sum(kcol * s, axis=1, keepdims=True)
    v_new = beta_ref[...] * (v_ref[...] - eg * ks)
    sn = s * eg + kcol * v_new
    so_ref[...] = sn
    o_ref[...] = jnp.sum(qcol * sn, axis=1, keepdims=True)


def gdn_decode_step(s0, q, k, v, beta, g):
    bsz, gh, dk, _ = s0.shape
    d = gh * dk
    rows = lambda x: x.reshape(bsz, gh, 1, dk)
    wide = lambda x: jnp.broadcast_to(x[:, :, None, None], (bsz, gh, 1, dk))
    vec = lambda: pl.BlockSpec((None, gh, 1, dk), lambda bi: (bi, 0, 0, 0))
    st = lambda: pl.BlockSpec((None, gh, dk, dk), lambda bi: (bi, 0, 0, 0))
    sn, o = pl.pallas_call(
        _gds_kernel,
        grid=(bsz,),
        in_specs=[st()] + [vec() for _ in range(5)],
        out_specs=[st(), vec()],
        out_shape=[jax.ShapeDtypeStruct(s0.shape, F32), jax.ShapeDtypeStruct((bsz, gh, 1, dk), F32)],
        compiler_params=_cparams("arbitrary"),
        name="gdn_decode",
    )(s0, rows(q), rows(k), rows(v), wide(beta), wide(jnp.exp(g)))
    return o.reshape(bsz, d), sn


def _rwkv_layer(x, modp, mods, nw, j, P, bp, t, shift_prev, s0):
    m, d = x.shape
    mp = bp * t
    bs = m - mp
    nh = d // RWKV_N
    xs, hlast, h_s = pre_rwkv(x, modp[:, 0:2], mods[1], mods[0], nw, P['rw_mix'][j], shift_prev, t)
    tn = min(d, 512)
    mm = lambda a, w, widx, n, tnn, **kw: pmatmul(a, w, widx, n_out=n, tn=tnn, **kw)
    r = mm(xs[0], P['rw_w_rkv'], (j, 0), d, tn)
    k = mm(xs[1], P['rw_w_rkv'], (j, 1), d, tn)
    v = mm(xs[2], P['rw_w_rkv'], (j, 2), d, tn)
    lora = P['rw_w1'].shape[-1]
    wl = mm(mm(xs[3], P['rw_w1'], (j,), lora, lora, act="tanh", out_dtype=BF16), P['rw_w2'], (j,), d, tn)
    al = mm(mm(xs[4], P['rw_a1'], (j,), lora, lora, out_dtype=BF16), P['rw_a2'], (j,), d, tn)
    gl = P['rw_g1'].shape[-1]
    glp = -(-gl // LANES) * LANES
    g1 = jnp.pad(P['rw_g1'][j], ((0, 0), (0, glp - gl)))
    g2 = jnp.pad(P['rw_g2'][j], ((0, glp - gl), (0, 0)))
    g = mm(mm(xs[5], g1, (), glp, glp, act="sigmoid", out_dtype=BF16), g2, (), d, tn)
    prm = [P[n][j] for n in ('rw_w0', 'rw_a0', 'rw_k_k', 'rw_k_a')] + [P['rw_r_k'][j].reshape(d)]
    ln_w, ln_b = P['rw_ln_w'][j], P['rw_ln_b'][j]
    yg_p, sp = rwkv_prompt_scan(r, k, v, wl, al, g, *prm, ln_w, ln_b, bp, t)

    rs, ks, vs, wls, als, gs = (a[mp:] for a in (r, k, v, wl, al, g))
    w_log = -jax.nn.softplus(-(prm[0] + wls)) - 0.5
    dec = jnp.exp(-jnp.exp(w_log))
    a = jax.nn.sigmoid(prm[1] + als)
    hv = lambda z: z.reshape(bs, nh, RWKV_N)
    kx = hv(ks * prm[2])
    kap = kx * lax.rsqrt(jnp.sum(kx * kx, axis=-1, keepdims=True) + 1e-6)
    k_mod = ks * (1.0 + (a - 1.0) * prm[3])
    ys, ss = rwkv_decode_step(s0, hv(dec), kap, kap * hv(a), hv(k_mod), hv(vs), hv(rs))
    mu = jnp.mean(ys, axis=-1, keepdims=True)
    var = jnp.mean(jnp.square(ys - mu), axis=-1, keepdims=True)
    yn = ((ys - mu) * lax.rsqrt(var + RWKV_LN_EPS)).reshape(bs, d) * ln_w + ln_b
    bonus = jnp.sum(hv(rs) * hv(k_mod) * prm[4].reshape(nh, RWKV_N), axis=-1, keepdims=True) * hv(vs)
    yg_s = ((yn + bonus.reshape(bs, d)) * gs).astype(BF16)

    out = mm(jnp.concatenate([yg_p, yg_s], axis=0), P['rw_w_o'], (j,), d, tn)
    return out, (sp, ss), (hlast[:, 0], h_s)


def _gdn_layer(h, j, P, bp, t, conv_prev, s0):
    m, d = h.shape
    mp = bp * t
    bs = m - mp
    gh = d // GDN_DK
    cdim = 3 * d
    tn = min(d, 512)
    w_in = P['gd_w_in']
    main = pmatmul(h, w_in, (j,), n_out=4 * d, tn=tn)
    w_tail = jnp.pad(w_in[j][:, 4 * d:], ((0, 0), (0, LANES - 2 * gh)))
    tail = pmatmul(h, w_tail, (), n_out=LANES, tn=LANES)
    qkv = gdn_conv(main, P['gd_conv_w'][j], jnp.swapaxes(conv_prev, 0, 1), bp, t)
    beta = jax.nn.sigmoid(tail[:, :gh])
    g = -jnp.exp(P['gd_A_log'][j]) * jax.nn.softplus(tail[:, gh:2 * gh] + P['gd_dt_bias'][j])
    norm_w = P['gd_norm_w'][j]
    og_p, sp = gdn_prompt_scan(qkv, main, g[:mp].reshape(bp, t, gh), beta[:mp].reshape(bp, t, gh), norm_w, bp, t)

    o_s, ss = gdn_decode_step(s0, qkv[0, mp:], qkv[1, mp:], qkv[2, mp:], beta[mp:], g[mp:])
    oh = o_s.reshape(bs, gh, GDN_DK)
    oh = oh * lax.rsqrt(jnp.mean(oh * oh, axis=-1, keepdims=True) + RMS_EPS) * norm_w
    og_s = (oh.reshape(bs, d) * jax.nn.silu(main[mp:, cdim:])).astype(BF16)

    out = pmatmul(jnp.concatenate([og_p, og_s], axis=0), P['gd_w_out'], (j,), n_out=d, tn=tn)
    pre = main[:, :cdim]
    conv_p = pre[:mp].reshape(bp, t, cdim)[:, t - (GDN_CONV - 1):]
    conv_s = jnp.concatenate([conv_prev[:, 1:], pre[mp:, None, :]], axis=1)
    return out, (sp, ss), (conv_p, conv_s)


def kernel(x_prompt, x_sample, state_rwkv, state_rwkv_shift, state_gdn, state_gdn_conv, c_prompt, c_sample,
           w_ada, b_ada, norm1_w, norm2_w, rw_mix, rw_w_rkv, rw_w0, rw_w1, rw_w2, rw_a0, rw_a1, rw_a2,
           rw_g1, rw_g2, rw_k_k, rw_k_a, rw_r_k, rw_ln_w, rw_ln_b, rw_w_o, gd_w_in, gd_conv_w, gd_A_log,
           gd_dt_bias, gd_norm_w, gd_w_out, ffn_w_gate, ffn_w_up, ffn_w_down, final_norm_w):
    P = {
        'rw_mix': rw_mix, 'rw_w_rkv': rw_w_rkv, 'rw_w0': rw_w0, 'rw_w1': rw_w1, 'rw_w2': rw_w2,
        'rw_a0': rw_a0, 'rw_a1': rw_a1, 'rw_a2': rw_a2, 'rw_g1': rw_g1, 'rw_g2': rw_g2,
        'rw_k_k': rw_k_k, 'rw_k_a': rw_k_a, 'rw_r_k': rw_r_k, 'rw_ln_w': rw_ln_w, 'rw_ln_b': rw_ln_b,
        'rw_w_o': rw_w_o, 'gd_w_in': gd_w_in, 'gd_conv_w': gd_conv_w, 'gd_A_log': gd_A_log,
        'gd_dt_bias': gd_dt_bias, 'gd_norm_w': gd_norm_w, 'gd_w_out': gd_w_out,
    }
    bp, t, d = x_prompt.shape
    bs = x_sample.shape[0]
    depth = w_ada.shape[0]
    dff = ffn_w_gate.shape[-1]
    mp = bp * t
    assert x_sample.shape[1] == 1 and t % CHUNK == 0 and t % bs == 0 and bs % 16 == 0 and d % (2 * LANES) == 0

    x = jnp.concatenate([x_prompt.reshape(mp, d), x_sample.reshape(bs, d)], axis=0)
    c_all = jnp.concatenate([c_prompt, c_sample], axis=0)
    nb = bp + bs
    nbp = -(-nb // 16) * 16
    c_act = jnp.pad(jax.nn.silu(c_all), ((0, nbp - nb), (0, 0))).astype(BF16)
    tn_d = min(d, 512)
    tn_f = _pick_tile(dff, 256, LANES)
    tn_o = min(d, 256)
    modp, mods = [], []
    for layer in range(depth):
        mod = (pmatmul(c_act, w_ada, (layer,), n_out=6 * d, tn=tn_d)[:nb] + b_ada[layer]).reshape(nb, 6, d)
        modp.append(mod[:bp])
        mods.append(jnp.swapaxes(mod[bp:], 0, 1))

    new_rw_s, new_rw_shift, new_gd_s, new_gd_conv = [], [], [], []
    h = None
    for layer in range(depth):
        j = layer // 2
        mpl, msl = modp[layer], mods[layer]
        if layer % 2 == 0:
            out, s_new, sh_new = _rwkv_layer(x, mpl, msl, norm1_w[layer], j, P, bp, t,
                                             state_rwkv_shift[j], state_rwkv[j])
            new_rw_s.append(s_new)
            new_rw_shift.append(sh_new)
        else:
            if h is None:
                _, h = resnorm(x, jnp.zeros_like(x), jnp.stack([mpl[:, 2], mpl[:, 1], mpl[:, 0]], axis=1),
                               msl[2], msl[1], msl[0], norm1_w[layer], t)
            out, s_new, cb_new = _gdn_layer(h, j, P, bp, t, state_gdn_conv[j], state_gdn[j])
            new_gd_s.append(s_new)
            new_gd_conv.append(cb_new)
        x, h = resnorm(x, out, jnp.stack([mpl[:, 2], mpl[:, 4], mpl[:, 3]], axis=1),
                       msl[2], msl[4], msl[3], norm2_w[layer], t)
        act = pmatmul(h, ffn_w_gate, (layer,), n_out=dff, tn=tn_f, w2=ffn_w_up, out_dtype=BF16)
        kh = dff // 2
        part = pmatmul(act, ffn_w_down, (layer,), n_out=d, tn=tn_o, k0=0, kk=kh)
        ffn = pmatmul(act, ffn_w_down, (layer,), n_out=d, tn=tn_o, k0=kh, kk=kh, add=part)
        if layer + 1 < depth:
            nxt_p, nxt_s = modp[layer + 1], mods[layer + 1]
            x, h = resnorm(x, ffn, jnp.stack([mpl[:, 5], nxt_p[:, 1], nxt_p[:, 0]], axis=1),
                           msl[5], nxt_s[1], nxt_s[0], norm1_w[layer + 1], t, emit_h=(layer + 1) % 2 == 1)
        else:
            y_p, y_s = final_norm(x, ffn, mpl[:, 5:6], msl[5], final_norm_w, t)
    grp = lambda lst, i: jnp.stack([e[i] for e in lst])
    return (y_p.reshape(bp, t, d), y_s.reshape(bs, 1, d),
            grp(new_rw_s, 0), grp(new_rw_shift, 0), grp(new_gd_s, 0), grp(new_gd_conv, 0),
            grp(new_rw_s, 1), grp(new_rw_shift, 1), grp(new_gd_s, 1), grp(new_gd_conv, 1))
```

```python
import functools

import jax
import jax.numpy as jnp
from jax import lax
from jax.experimental import pallas as pl
from jax.experimental.pallas import tpu as pltpu

F32, BF16 = jnp.float32, jnp.bfloat16

RMS_EPS = 1e-6
RWKV_LN_EPS = 64e-5
RWKV_N = 64
GDN_DK = 128
GDN_CONV = 4
CHUNK = 64
LANES = 128
SUBLANES = 8
GROUP = 16
VMEM_LIMIT = 50 * 2**20


def _cparams(*sem):
    return pltpu.CompilerParams(dimension_semantics=sem, vmem_limit_bytes=VMEM_LIMIT)


def _pick_tile(n, cap, mult):
    best = None
    for t in range(mult, min(n, cap) + 1, mult):
        if n % t == 0:
            best = t
    return best or n


def _group(n):
    g = GROUP
    while n % g:
        g //= 2
    return g


def _dot(a, b):
    return jnp.dot(a.astype(BF16), b.astype(BF16), preferred_element_type=F32)


def _dot_nt(a, b):
    return lax.dot_general(a.astype(BF16), b.astype(BF16), (((1,), (1,)), ((), ())),
                           preferred_element_type=F32)


def _dot_tn(a, b):
    return lax.dot_general(a.astype(BF16), b.astype(BF16), (((0,), (0,)), ((), ())),
                           preferred_element_type=F32)


def _split3(x):
    hi = x.astype(BF16)
    r1 = x - hi.astype(F32)
    mid = r1.astype(BF16)
    lo = (r1 - mid.astype(F32)).astype(BF16)
    return hi, mid, lo


def _tri(n, transpose=False):
    row = lax.broadcasted_iota(jnp.int32, (n, n), 0)
    col = lax.broadcasted_iota(jnp.int32, (n, n), 1)
    m = (row <= col) if transpose else (row >= col)
    return jnp.where(m, 1.0, 0.0).astype(BF16)


def _row_to_col(row):
    n = row.shape[1]
    eye = lax.broadcasted_iota(jnp.int32, (n, n), 0) == lax.broadcasted_iota(jnp.int32, (n, n), 1)
    return jnp.sum(jnp.where(eye, jnp.broadcast_to(row, (n, n)), 0.0), axis=1, keepdims=True)


def _segsum(x, width, exact):
    l = x.shape[1]
    gi = lax.broadcasted_iota(jnp.int32, (LANES, LANES), 0) // width
    gj = lax.broadcasted_iota(jnp.int32, (LANES, LANES), 1) // width
    bd = jnp.where(gi == gj, 1.0, 0.0).astype(BF16)
    hi = x.astype(BF16)
    lo = (x - hi.astype(F32)).astype(BF16) if exact else None
    outs = []
    for t in range(l // LANES):
        sl = slice(t * LANES, (t + 1) * LANES)
        s = jnp.dot(hi[:, sl], bd, preferred_element_type=F32)
        if exact:
            s = s + jnp.dot(lo[:, sl], bd, preferred_element_type=F32)
        outs.append(s)
    return outs[0] if len(outs) == 1 else jnp.concatenate(outs, axis=1)


def _modnorm(x, nw, scale, shift):
    y = x * lax.rsqrt(jnp.mean(x * x, axis=-1, keepdims=True) + RMS_EPS) * nw
    return y * (1.0 + scale) + shift


def _mm_kernel(*refs, n_w, has_add, act):
    a_ref = refs[0]
    w_refs = refs[1:1 + n_w]
    pos = 1 + n_w
    add_ref = refs[pos] if has_add else None
    pos += int(has_add)
    o_ref = refs[pos]
    wbf = refs[pos + 1:pos + 1 + n_w]

    @pl.when(pl.program_id(1) == 0)
    def _():
        for w, s in zip(w_refs, wbf):
            s[...] = w[...].astype(BF16)

    a = a_ref[...]
    y = jnp.dot(a, wbf[0][...], preferred_element_type=F32)
    if n_w == 2:
        u = jnp.dot(a, wbf[1][...], preferred_element_type=F32)
        y = y * jax.nn.sigmoid(y) * u
    if has_add:
        y = y + add_ref[...]
    if act == "tanh":
        y = jnp.tanh(y)
    elif act == "sigmoid":
        y = jax.nn.sigmoid(y)
    o_ref[...] = y.astype(o_ref.dtype)


def pmatmul(a, w, widx=(), *, n_out, tn, k0=0, kk=None, w2=None, add=None, act=None,
            out_dtype=F32, tm_cap=832, single_buffer_w=False):
    m = a.shape[0]
    kk = a.shape[1] if kk is None else kk
    kb = k0 // kk
    tm = _pick_tile(m, tm_cap, 16)
    ws = [w] if w2 is None else [w, w2]
    nlead = len(widx)
    grid = (n_out // tn, m // tm)
    in_specs = [pl.BlockSpec((tm, kk), lambda j, i: (i, kb))]
    wmode = dict(pipeline_mode=pl.Buffered(1)) if single_buffer_w else {}
    for _ in ws:
        in_specs.append(pl.BlockSpec((None,) * nlead + (kk, tn),
                                     lambda j, i: tuple(widx) + (kb, j), **wmode))
    args = [a] + ws
    if add is not None:
        in_specs.append(pl.BlockSpec((tm, tn), lambda j, i: (i, j)))
        args.append(add)
    return pl.pallas_call(
        functools.partial(_mm_kernel, n_w=len(ws), has_add=add is not None, act=act),
        grid=grid,
        in_specs=in_specs,
        out_specs=pl.BlockSpec((tm, tn), lambda j, i: (i, j)),
        out_shape=jax.ShapeDtypeStruct((m, n_out), out_dtype),
        scratch_shapes=[pltpu.VMEM((kk, tn), BF16) for _ in ws],
        compiler_params=_cparams("arbitrary", "arbitrary"),
        name=f"mm_m{m}_k{kk}_n{n_out}" + ("_glu" if w2 is not None else "") + ("_add" if add is not None else ""),
    )(*args)


def _tile_specs(tr, d, np_tiles, tps, nbp):
    tok = lambda: pl.BlockSpec((tr, d), lambda i: (i, 0))
    seq = lambda n: pl.BlockSpec((None, n, d), lambda i: (jnp.minimum(i // tps, nbp - 1), 0, 0))
    samp = lambda: pl.BlockSpec((tr, d), lambda i: (0, 0))
    row = lambda n=1: pl.BlockSpec((n, d), lambda i: (0, 0))
    return tok, seq, samp, row


def _pre_rwkv_kernel(x_ref, mp_ref, scs_ref, shs_ref, nw_ref, mix_ref, prev_s_ref,
                     o0, o1, o2, o3, o4, o5, hlast_ref, hs_ref, carry_ref, *, np_tiles, tps):
    i = pl.program_id(0)
    is_s = i == np_tiles
    shift = jnp.where(is_s, shs_ref[...], mp_ref[0:1, :])
    scale = jnp.where(is_s, scs_ref[...], mp_ref[1:2, :])
    h = _modnorm(x_ref[...], nw_ref[...], scale, shift)
    tr = h.shape[0]

    @pl.when(i % tps == 0)
    def _():
        carry_ref[...] = jnp.zeros_like(carry_ref)

    rowi = lax.broadcasted_iota(jnp.int32, h.shape, 0)
    prev_p = jnp.where(rowi == 0, carry_ref[...], pltpu.roll(h, 1, axis=0))
    prev = jnp.where(is_s, prev_s_ref[...], prev_p)
    xx = prev - h
    for n, o in enumerate((o0, o1, o2, o3, o4, o5)):
        o[...] = (h + xx * mix_ref[n:n + 1, :]).astype(BF16)
    last = h[tr - 1:tr, :]
    carry_ref[...] = last

    @pl.when(jnp.logical_not(is_s))
    def _():
        hlast_ref[...] = last

    @pl.when(is_s)
    def _():
        hs_ref[...] = h


def pre_rwkv(x, modp, scale_s, shift_s, nw, mix, prev_s, t):
    m, d = x.shape
    tr = scale_s.shape[0]
    nbp = modp.shape[0]
    tps = t // tr
    np_tiles = nbp * tps
    tok, seq, samp, row = _tile_specs(tr, d, np_tiles, tps, nbp)
    outs = pl.pallas_call(
        functools.partial(_pre_rwkv_kernel, np_tiles=np_tiles, tps=tps),
        grid=(np_tiles + 1,),
        in_specs=[tok(), seq(2), samp(), samp(), row(), row(6), samp()],
        out_specs=[tok() for _ in range(6)] + [seq(1), samp()],
        out_shape=[jax.ShapeDtypeStruct((m, d), BF16)] * 6 + [jax.ShapeDtypeStruct((nbp, 1, d), F32),
                                                              jax.ShapeDtypeStruct((tr, d), F32)],
        scratch_shapes=[pltpu.VMEM((1, d), F32)],
        compiler_params=_cparams("arbitrary"),
        name="pre_rwkv",
    )(x, modp, scale_s, shift_s, nw.reshape(1, d), mix, prev_s)
    return outs[:6], outs[6], outs[7]


def _resnorm_kernel(x_ref, dl_ref, mp_ref, gs_ref, scs_ref, shs_ref, nw_ref, xo_ref, *h_refs, np_tiles):
    is_s = pl.program_id(0) == np_tiles
    gate = jnp.where(is_s, gs_ref[...], mp_ref[0:1, :])
    xn = x_ref[...] + gate * dl_ref[...]
    xo_ref[...] = xn
    if h_refs:
        scale = jnp.where(is_s, scs_ref[...], mp_ref[1:2, :])
        shift = jnp.where(is_s, shs_ref[...], mp_ref[2:3, :])
        h_refs[0][...] = _modnorm(xn, nw_ref[...], scale, shift).astype(BF16)


def resnorm(x, delta, modp, gate_s, scale_s, shift_s, nw, t, emit_h=True):
    m, d = x.shape
    tr = gate_s.shape[0]
    nbp = modp.shape[0]
    tps = t // tr
    np_tiles = nbp * tps
    tok, seq, samp, row = _tile_specs(tr, d, np_tiles, tps, nbp)
    outs = pl.pallas_call(
        functools.partial(_resnorm_kernel, np_tiles=np_tiles),
        grid=(np_tiles + 1,),
        in_specs=[tok(), tok(), seq(3), samp(), samp(), samp(), row()],
        out_specs=[tok()] + ([tok()] if emit_h else []),
        out_shape=[jax.ShapeDtypeStruct((m, d), F32)] + ([jax.ShapeDtypeStruct((m, d), BF16)] if emit_h else []),
        compiler_params=_cparams("arbitrary"),
        name="resnorm",
    )(x, delta, modp, gate_s, scale_s, shift_s, nw.reshape(1, d))
    return (outs[0], outs[1]) if emit_h else (outs[0], None)


def _final_kernel(x_ref, dl_ref, mp_ref, gs_ref, nw_ref, yp_ref, ys_ref, *, np_tiles):
    is_s = pl.program_id(0) == np_tiles
    gate = jnp.where(is_s, gs_ref[...], mp_ref[0:1, :])
    xn = x_ref[...] + gate * dl_ref[...]
    y = xn * lax.rsqrt(jnp.mean(xn * xn, axis=-1, keepdims=True) + RMS_EPS) * nw_ref[...]

    @pl.when(jnp.logical_not(is_s))
    def _():
        yp_ref[...] = y

    @pl.when(is_s)
    def _():
        ys_ref[...] = y


def final_norm(x, delta, modp, gate_s, nw, t):
    m, d = x.shape
    tr = gate_s.shape[0]
    nbp = modp.shape[0]
    tps = t // tr
    np_tiles = nbp * tps
    tok, seq, samp, row = _tile_specs(tr, d, np_tiles, tps, nbp)
    return pl.pallas_call(
        functools.partial(_final_kernel, np_tiles=np_tiles),
        grid=(np_tiles + 1,),
        in_specs=[tok(), tok(), seq(1), samp(), row()],
        out_specs=[pl.BlockSpec((tr, d), lambda i: (jnp.minimum(i, np_tiles - 1), 0)), samp()],
        out_shape=[jax.ShapeDtypeStruct((np_tiles * tr, d), F32), jax.ShapeDtypeStruct((tr, d), F32)],
        compiler_params=_cparams("arbitrary"),
        name="final_norm",
    )(x, delta, modp, gate_s, nw.reshape(1, d))


def _gdn_conv_kernel(x_ref, w_ref, buf_ref, o_ref, prev_ref, *, np_tiles, tps):
    sec = pl.program_id(0)
    i = pl.program_id(1)
    is_s = i == np_tiles

    x = x_ref[...]

    @pl.when(i % tps == 0)
    def _():
        prev_ref[...] = jnp.zeros_like(prev_ref)

    def tap(k):
        return w_ref[GDN_CONV - 1 - k:GDN_CONV - k, :]

    def finish(y):
        a = y * jax.nn.sigmoid(y)
        qscale = jnp.where(sec == 0, GDN_DK ** -0.5, 1.0)
        nrm = a * (lax.rsqrt(_segsum(a * a, GDN_DK, exact=False) + 1e-6) * qscale)
        o_ref[...] = jnp.where(sec == 2, a, nrm)

    @pl.when(is_s)
    def _():
        y = x * tap(0)
        for k in range(1, GDN_CONV):
            y = y + buf_ref[GDN_CONV - 1 - k] * tap(k)
        finish(y)

    @pl.when(jnp.logical_not(is_s))
    def _():
        prev = prev_ref[...]
        rowi = lax.broadcasted_iota(jnp.int32, x.shape, 0)
        y = x * tap(0)
        for k in range(1, GDN_CONV):
            back = jnp.where(rowi >= k, pltpu.roll(x, k, axis=0), pltpu.roll(prev, k, axis=0))
            y = y + back * tap(k)
        prev_ref[...] = x
        finish(y)


def gdn_conv(main, conv_w, bufs, nbp, t):
    m = main.shape[0]
    d = main.shape[1] // 4
    tr = bufs.shape[1]
    tps = t // tr
    np_tiles = nbp * tps
    return pl.pallas_call(
        functools.partial(_gdn_conv_kernel, np_tiles=np_tiles, tps=tps),
        grid=(3, np_tiles + 1),
        in_specs=[pl.BlockSpec((tr, d), lambda s, i: (i, s)),
                  pl.BlockSpec((GDN_CONV, d), lambda s, i: (0, s)),
                  pl.BlockSpec((GDN_CONV - 1, tr, d), lambda s, i: (0, 0, s))],
        out_specs=pl.BlockSpec((None, tr, d), lambda s, i: (s, i, 0)),
        out_shape=jax.ShapeDtypeStruct((3, m, d), F32),
        scratch_shapes=[pltpu.VMEM((tr, d), F32)],
        compiler_params=_cparams("arbitrary", "arbitrary"),
        name="gdn_conv",
    )(main, conv_w, bufs)


def _inv_kernel(m_ref, n_ref, *, c):
    nblk = c // SUBLANES
    zeros = jnp.zeros((SUBLANES * c, LANES), F32)
    for ib in range(nblk):
        n_ref[ib * SUBLANES * c:(ib + 1) * SUBLANES * c, :] = zeros

        def row(ii, carry, ib=ib):
            base = pl.multiple_of((ib * SUBLANES + ii) * c, c)
            acc = [m_ref[pl.ds(base + SUBLANES * k, SUBLANES), :] for k in range(ib + 1)]
            for j in range((ib + 1) * SUBLANES):
                coef = m_ref[pl.ds(base + j, 1), :]
                for k in range(j // SUBLANES + 1):
                    acc[k] = acc[k] + coef * n_ref[j * c + SUBLANES * k:j * c + SUBLANES * (k + 1), :]
            for k in range(ib + 1):
                n_ref[pl.ds(base + SUBLANES * k, SUBLANES), :] = -acc[k]
            return carry

        lax.fori_loop(0, SUBLANES, row, 0)


def tri_inverse(mats):
    shape = mats.shape
    c = shape[-1]
    u = 1
    for s in shape[:-2]:
        u *= s
    up = -(-u // LANES) * LANES
    flat = jnp.transpose(mats.reshape(u, c, c), (1, 2, 0)).reshape(c * c, u)
    if up != u:
        flat = jnp.pad(flat, ((0, 0), (0, up - u)))
    out = pl.pallas_call(
        functools.partial(_inv_kernel, c=c),
        grid=(up // LANES,),
        in_specs=[pl.BlockSpec((c * c, LANES), lambda g: (0, g))],
        out_specs=pl.BlockSpec((c * c, LANES), lambda g: (0, g)),
        out_shape=jax.ShapeDtypeStruct((c * c, up), F32),
        compiler_params=_cparams("arbitrary"),
        name="tri_inverse",
    )(flat)
    return jnp.transpose(out[:, :u].reshape(c, c, u), (2, 0, 1)).reshape(shape)


def _rwkv_token_prep(r, k, wl, al, w0, a0, k_k, k_a, r_k):
    w_log = -jax.nn.softplus(-(w0 + wl)) - 0.5
    lw = -jnp.exp(w_log)
    a = jax.nn.sigmoid(a0 + al)
    kx = k * k_k
    kap = kx * lax.rsqrt(_segsum(kx * kx, RWKV_N, exact=False) + 1e-6)
    k_mod = k * (1.0 + (a - 1.0) * k_a)
    rk = _segsum(r * k_mod * r_k, RWKV_N, exact=True)
    return lw, kap, k_mod, kap * a, rk


def _rw1_kernel(r_ref, k_ref, v_ref, wl_ref, al_ref, w0_ref, a0_ref, kk_ref, ka_ref, rk_ref,
                kaph_ref, rh_ref, kh_ref, bh_ref, khp_ref, bhp_ref, vb_ref, bonus_ref, pc_ref, mb_ref):
    v = v_ref[...]
    lw, kap, k, b, rk = _rwkv_token_prep(r_ref[...], k_ref[...], wl_ref[...], al_ref[...], w0_ref[...],
                                         a0_ref[...], kk_ref[...], ka_ref[...], rk_ref[...])
    bonus_ref[...] = rk * v
    vb_ref[...] = v.astype(BF16)
    c, lb = lw.shape
    tri = _tri(c)
    hi, mid, lo = _split3(lw)
    p = (jnp.dot(tri, hi, preferred_element_type=F32) + jnp.dot(tri, mid, preferred_element_type=F32)
         + jnp.dot(tri, lo, preferred_element_type=F32))
    pc = p[c - 1:c, :]
    en = jnp.exp(-p)
    ec = jnp.exp(pc - p)
    kaph = kap * jnp.exp(p - lw)
    bh = (b * en).astype(BF16)
    kaph_ref[...] = kaph.astype(BF16)
    rh_ref[...] = (r_ref[...] * jnp.exp(p)).astype(BF16)
    kh_ref[...] = (k * en).astype(BF16)
    bh_ref[...] = bh
    khp_ref[...] = (k * ec).astype(BF16)
    bhp_ref[...] = (b * ec).astype(BF16)
    pc_ref[...] = jnp.exp(pc)
    lane = lax.broadcasted_iota(jnp.int32, (c, LANES), 1)
    m0 = lane < RWKV_N
    rowi = lax.broadcasted_iota(jnp.int32, (2 * c, c), 0) % c
    coli = lax.broadcasted_iota(jnp.int32, (2 * c, c), 1)
    strict = rowi > coli
    for jt in range(lb // LANES):
        kp = kaph[:, jt * LANES:(jt + 1) * LANES]
        bp = bh[:, jt * LANES:(jt + 1) * LANES]
        lhs = jnp.concatenate([jnp.where(m0, kp, 0.0), jnp.where(m0, 0.0, kp)], axis=0)
        g = _dot_nt(lhs, bp)
        g = jnp.where(strict, g, 0.0)
        mb_ref[2 * jt] = g[:c]
        mb_ref[2 * jt + 1] = g[c:]


def _rw3_kernel(kaph_ref, rh_ref, kh_ref, bh_ref, khp_ref, bhp_ref, v_ref, pc_ref, n_ref,
                bonus_ref, g_ref, lnw_ref, lnb_ref, yg_ref, sfin_ref, a_ref):
    ci = pl.program_id(1)
    c = kaph_ref.shape[0]
    npair = a_ref.shape[0]

    @pl.when(ci == 0)
    def _():
        a_ref[...] = jnp.zeros_like(a_ref)

    lane2 = lax.broadcasted_iota(jnp.int32, (2 * c, LANES), 1)
    m0_2 = lane2 < RWKV_N
    m0 = lax.broadcasted_iota(jnp.int32, (c, LANES), 1) < RWKV_N
    row = lax.broadcasted_iota(jnp.int32, (c, c), 0)
    col = lax.broadcasted_iota(jnp.int32, (c, c), 1)
    strict = row > col
    incl = row >= col
    rr = lax.broadcasted_iota(jnp.int32, (LANES, LANES), 0)
    cc = lax.broadcasted_iota(jnp.int32, (LANES, LANES), 1)
    blockdiag = (rr < RWKV_N) == (cc < RWKV_N)
    ng = _group(npair)
    halves = (0, 1)

    def group(gi, carry):
        js = [gi * ng + g for g in range(ng)]
        lss = [pl.ds(pl.multiple_of(j * LANES, LANES), LANES) for j in js]
        lhs = [jnp.concatenate([kaph_ref[:, ls], rh_ref[:, ls]], axis=0) for ls in lss]
        vv = [v_ref[:, ls] for ls in lss]
        a0 = [a_ref[j] for j in js]
        x = [_dot(l, a) for l, a in zip(lhs, a0)]
        lhs_m = []
        for l in lhs:
            l32 = l.astype(F32)
            lhs_m.append((jnp.where(m0_2, l32, 0.0).astype(BF16), jnp.where(m0_2, 0.0, l32).astype(BF16)))
        gk = [[_dot_nt(lhs_m[g][h], kh_ref[:, lss[g]]) for h in halves] for g in range(ng)]
        gb = [[_dot_nt(lhs_m[g][h][c:], bh_ref[:, lss[g]]) for h in halves] for g in range(ng)]
        mk = [[jnp.where(strict, gk[g][h][:c], 0.0).astype(BF16) for h in halves] for g in range(ng)]
        lrk = [[jnp.where(incl, gk[g][h][c:], 0.0).astype(BF16) for h in halves] for g in range(ng)]
        lrb = [[jnp.where(incl, gb[g][h], 0.0).astype(BF16) for h in halves] for g in range(ng)]
        mkv = [[_dot(mk[g][h], vv[g]) for h in halves] for g in range(ng)]
        lrkv = [[_dot(lrk[g][h], vv[g]) for h in halves] for g in range(ng)]
        kv = [_dot_tn(khp_ref[:, lss[g]], vv[g]) for g in range(ng)]
        rhs = [[x[g][:c] + mkv[g][h] for h in halves] for g in range(ng)]
        nr = [[_dot(n_ref[2 * js[g] + h], rhs[g][h]) for h in halves] for g in range(ng)]
        uh = [[rhs[g][h] + nr[g][h] for h in halves] for g in range(ng)]
        u = [jnp.where(m0, uh[g][0], uh[g][1]) for g in range(ng)]
        lu = [[_dot(lrb[g][h], uh[g][h]) for h in halves] for g in range(ng)]
        bu = [_dot_tn(bhp_ref[:, lss[g]], u[g]) for g in range(ng)]
        y = [x[g][c:] + jnp.where(m0, lrkv[g][0] - lu[g][0], lrkv[g][1] - lu[g][1]) for g in range(ng)]
        mu = [_segsum(y[g], RWKV_N, exact=True) * (1.0 / RWKV_N) for g in range(ng)]
        yc = [y[g] - mu[g] for g in range(ng)]
        var = [_segsum(yc[g] * yc[g], RWKV_N, exact=False) * (1.0 / RWKV_N) for g in range(ng)]
        for g in range(ng):
            ls = lss[g]
            yn = yc[g] * lax.rsqrt(var[g] + RWKV_LN_EPS) * lnw_ref[:, ls] + lnb_ref[:, ls]
            yg_ref[:, ls] = ((yn + bonus_ref[:, ls]) * g_ref[:, ls]).astype(BF16)
            pcc = _row_to_col(pc_ref[:, ls])
            a_ref[js[g]] = pcc * a0[g] + jnp.where(blockdiag, kv[g] - bu[g], 0.0)
        return carry

    lax.fori_loop(0, npair // ng, group, 0)

    @pl.when(ci == pl.num_programs(1) - 1)
    def _():
        sfin_ref[...] = a_ref[...]


def rwkv_prompt_scan(r, k, v, wl, al, g, w0, a0, k_k, k_a, r_k, ln_w, ln_b, bsz, t):
    d = r.shape[1]
    c = CHUNK
    nc = t // c
    h = d // RWKV_N
    lb = min(d, 4096)
    row2 = lambda x: x.reshape(1, d)
    tok2 = lambda: pl.BlockSpec((c, lb), lambda bi, ci, li: (bi * nc + ci, li))
    prm = lambda: pl.BlockSpec((1, lb), lambda bi, ci, li: (0, li))
    tok = lambda: pl.BlockSpec((None, c, lb), lambda bi, ci, li: (bi, ci, li))
    bf = jax.ShapeDtypeStruct((bsz, t, d), BF16)
    kaph, rh, kh, bh, khp, bhp, vb, bonus, pc, mb = pl.pallas_call(
        _rw1_kernel,
        grid=(bsz, nc, d // lb),
        in_specs=[tok2() for _ in range(5)] + [prm() for _ in range(5)],
        out_specs=[tok() for _ in range(8)] + [
            pl.BlockSpec((None, None, 1, lb), lambda bi, ci, li: (bi, ci, 0, li)),
            pl.BlockSpec((None, None, lb // RWKV_N, c, c), lambda bi, ci, li: (bi, ci, li, 0, 0)),
        ],
        out_shape=[bf] * 7 + [jax.ShapeDtypeStruct((bsz, t, d), F32),
                              jax.ShapeDtypeStruct((bsz, nc, 1, d), F32),
                              jax.ShapeDtypeStruct((bsz, nc, h, c, c), F32)],
        compiler_params=_cparams("arbitrary", "arbitrary", "arbitrary"),
        name="rwkv_chunk_prep",
    )(r, k, v, wl, al, row2(w0), row2(a0), row2(k_k), row2(k_a), row2(r_k))
    nmat = tri_inverse(mb)
    tokd = lambda: pl.BlockSpec((None, c, d), lambda bi, ci: (bi, ci, 0))
    tok2d = lambda: pl.BlockSpec((c, d), lambda bi, ci: (bi * nc + ci, 0))
    prmd = lambda: pl.BlockSpec((1, d), lambda bi, ci: (0, 0))
    yg, sfin = pl.pallas_call(
        _rw3_kernel,
        grid=(bsz, nc),
        in_specs=[tokd() for _ in range(7)] + [
            pl.BlockSpec((None, None, 1, d), lambda bi, ci: (bi, ci, 0, 0)),
            pl.BlockSpec((None, None, h, c, c), lambda bi, ci: (bi, ci, 0, 0, 0)),
            tokd(), tok2d(), prmd(), prmd(),
        ],
        out_specs=[tok2d(), pl.BlockSpec((None, h // 2, LANES, LANES), lambda bi, ci: (bi, 0, 0, 0))],
        out_shape=[jax.ShapeDtypeStruct((bsz * t, d), BF16),
                   jax.ShapeDtypeStruct((bsz, h // 2, LANES, LANES), F32)],
        scratch_shapes=[pltpu.VMEM((h // 2, LANES, LANES), F32)],
        compiler_params=_cparams("arbitrary", "arbitrary"),
        name="rwkv_chunk_scan",
    )(kaph, rh, kh, bh, khp, bhp, vb, pc, nmat, bonus, g, row2(ln_w), row2(ln_b))
    n = RWKV_N
    s_even = sfin[:, :, :n, :n]
    s_odd = sfin[:, :, n:, n:]
    s = jnp.stack([s_even, s_odd], axis=2).reshape(bsz, h, n, n)
    return yg, jnp.swapaxes(s, -1, -2)


def _rws_kernel(s_ref, w_ref, kap_ref, b_ref, k_ref, v_ref, r_ref, so_ref, y_ref):
    s = s_ref[...]
    n = s.shape[-1]
    eye = (lax.broadcasted_iota(jnp.int32, (n, n), 0) == lax.broadcasted_iota(jnp.int32, (n, n), 1))[None]
    sa = -jnp.sum(s * kap_ref[...], axis=2, keepdims=True)
    vcol = jnp.sum(jnp.where(eye, v_ref[...], 0.0), axis=2, keepdims=True)
    sn = s * w_ref[...] + sa * b_ref[...] + vcol * k_ref[...]
    so_ref[...] = sn
    ycol = jnp.sum(sn * r_ref[...], axis=2, keepdims=True)
    y_ref[...] = jnp.sum(jnp.where(eye, ycol, 0.0), axis=1, keepdims=True)


def rwkv_decode_step(s0, w, kap, b, k, v, r):
    bsz, h, n, _ = s0.shape
    vec = lambda: pl.BlockSpec((None, h, 1, n), lambda bi: (bi, 0, 0, 0))
    st = lambda: pl.BlockSpec((None, h, n, n), lambda bi: (bi, 0, 0, 0))
    rows = lambda x: x.reshape(bsz, h, 1, n)
    sn, y = pl.pallas_call(
        _rws_kernel,
        grid=(bsz,),
        in_specs=[st()] + [vec() for _ in range(6)],
        out_specs=[st(), vec()],
        out_shape=[jax.ShapeDtypeStruct(s0.shape, F32), jax.ShapeDtypeStruct((bsz, h, 1, n), F32)],
        compiler_params=_cparams("arbitrary"),
        name="rwkv_decode",
    )(s0, rows(w), rows(kap), rows(b), rows(k), rows(v), rows(r))
    return y.reshape(bsz, h, n), sn


def _gd_head_scalars(g_ref, gcr_s, tri_t):
    hi, mid, lo = _split3(g_ref[...])
    gcr_s[...] = (jnp.dot(hi, tri_t, preferred_element_type=F32)
                  + jnp.dot(mid, tri_t, preferred_element_type=F32)
                  + jnp.dot(lo, tri_t, preferred_element_type=F32))


def _gd1_kernel(k_ref, g_ref, beta_ref, l_ref, gcr_s):
    c = k_ref.shape[0]
    gh = g_ref.shape[0]
    _gd_head_scalars(g_ref, gcr_s, _tri(c, transpose=True))
    row = lax.broadcasted_iota(jnp.int32, (c, c), 0)
    col = lax.broadcasted_iota(jnp.int32, (c, c), 1)
    strict = row > col
    ng = _group(gh)

    def group(gi, carry):
        js = [gi * ng + g for g in range(ng)]
        gr = [gcr_s[pl.ds(j, 1), :] for j in js]
        gcol = [_row_to_col(x) for x in gr]
        bcol = [_row_to_col(beta_ref[pl.ds(j, 1), :]) for j in js]
        kk = [k_ref[:, pl.ds(pl.multiple_of(j * GDN_DK, GDN_DK), GDN_DK)] for j in js]
        gram = [_dot_nt(kk[g] * bcol[g], kk[g]) for g in range(ng)]
        for g in range(ng):
            dec = jnp.exp(jnp.where(strict, gcol[g] - gr[g], 0.0))
            l_ref[js[g]] = jnp.where(strict, gram[g] * dec, 0.0)
        return carry

    lax.fori_loop(0, gh // ng, group, 0)


def _gd3_kernel(q_ref, k_ref, v_ref, g_ref, beta_ref, n_ref, z_ref, nw_ref, og_ref, sfin_ref, s_ref, gcr_s):
    ci = pl.program_id(1)
    c = k_ref.shape[0]
    gh = g_ref.shape[0]

    @pl.when(ci == 0)
    def _():
        s_ref[...] = jnp.zeros_like(s_ref)

    _gd_head_scalars(g_ref, gcr_s, _tri(c, transpose=True))
    row = lax.broadcasted_iota(jnp.int32, (c, c), 0)
    col = lax.broadcasted_iota(jnp.int32, (c, c), 1)
    incl = row >= col
    ng = _group(gh)
    rng = range(ng)

    def group(gi, carry):
        js = [gi * ng + g for g in rng]
        lss = [pl.ds(pl.multiple_of(j * GDN_DK, GDN_DK), GDN_DK) for j in js]
        gr = [gcr_s[pl.ds(j, 1), :] for j in js]
        gcol = [_row_to_col(x) for x in gr]
        bcol = [_row_to_col(beta_ref[pl.ds(j, 1), :]) for j in js]
        glast = [x[:, c - 1:c] for x in gr]
        q = [q_ref[:, ls] for ls in lss]
        kk = [k_ref[:, ls] for ls in lss]
        vv = [v_ref[:, ls] for ls in lss]
        s = [s_ref[j] for j in js]
        eg = [jnp.exp(x) for x in gcol]
        rhs = [jnp.concatenate([vv[g] * bcol[g], kk[g] * bcol[g] * eg[g]], axis=1) for g in rng]
        qk = [_dot_nt(q[g], kk[g]) for g in rng]
        nr = [_dot(n_ref[js[g]], rhs[g]) for g in rng]
        qs = [_dot(q[g] * eg[g], s[g]) for g in rng]
        sol = [rhs[g] + nr[g] for g in rng]
        ws = [_dot(sol[g][:, GDN_DK:], s[g]) for g in rng]
        v_new = [sol[g][:, :GDN_DK] - ws[g] for g in rng]
        attn = [jnp.where(incl, qk[g] * jnp.exp(jnp.where(incl, gcol[g] - gr[g], 0.0)), 0.0) for g in rng]
        av = [_dot(attn[g], v_new[g]) for g in rng]
        kv = [_dot_tn(kk[g] * jnp.exp(glast[g] - gcol[g]), v_new[g]) for g in rng]
        for g in rng:
            o = qs[g] + av[g]
            on = o * lax.rsqrt(jnp.mean(o * o, axis=-1, keepdims=True) + RMS_EPS) * nw_ref[...]
            z = z_ref[:, lss[g]]
            og_ref[:, lss[g]] = (on * (z * jax.nn.sigmoid(z))).astype(BF16)
            s_ref[js[g]] = s[g] * jnp.exp(glast[g]) + kv[g]
        return carry

    lax.fori_loop(0, gh // ng, group, 0)

    @pl.when(ci == pl.num_programs(1) - 1)
    def _():
        sfin_ref[...] = s_ref[...]


def gdn_prompt_scan(qkv, main, g, beta, norm_w, bsz, t):
    d = qkv.shape[2]
    c = CHUNK
    nc = t // c
    gh = d // GDN_DK
    rows = lambda x: jnp.swapaxes(x.reshape(bsz, nc, c, gh), -1, -2)
    g_r, beta_r = rows(g), rows(beta)
    sect = lambda s: pl.BlockSpec((None, c, d), lambda bi, ci: (s, bi * nc + ci, 0))
    hrow = lambda: pl.BlockSpec((None, None, gh, c), lambda bi, ci: (bi, ci, 0, 0))
    mat = lambda: pl.BlockSpec((None, None, gh, c, c), lambda bi, ci: (bi, ci, 0, 0, 0))
    lmat = pl.pallas_call(
        _gd1_kernel,
        grid=(bsz, nc),
        in_specs=[sect(1), hrow(), hrow()],
        out_specs=mat(),
        out_shape=jax.ShapeDtypeStruct((bsz, nc, gh, c, c), F32),
        scratch_shapes=[pltpu.VMEM((gh, c), F32)],
        compiler_params=_cparams("arbitrary", "arbitrary"),
        name="gdn_chunk_prep",
    )(qkv, g_r, beta_r)
    nmat = tri_inverse(lmat)
    og, sfin = pl.pallas_call(
        _gd3_kernel,
        grid=(bsz, nc),
        in_specs=[sect(0), sect(1), sect(2), hrow(), hrow(), mat(),
                  pl.BlockSpec((c, d), lambda bi, ci: (bi * nc + ci, 3)),
                  pl.BlockSpec((1, GDN_DK), lambda bi, ci: (0, 0))],
        out_specs=[pl.BlockSpec((c, d), lambda bi, ci: (bi * nc + ci, 0)),
                   pl.BlockSpec((None, gh, GDN_DK, GDN_DK), lambda bi, ci: (bi, 0, 0, 0))],
        out_shape=[jax.ShapeDtypeStruct((bsz * t, d), BF16),
                   jax.ShapeDtypeStruct((bsz, gh, GDN_DK, GDN_DK), F32)],
        scratch_shapes=[pltpu.VMEM((gh, GDN_DK, GDN_DK), F32), pltpu.VMEM((gh, c), F32)],
        compiler_params=_cparams("arbitrary", "arbitrary"),
        name="gdn_chunk_scan",
    )(qkv, qkv, qkv, g_r, beta_r, nmat, main, norm_w.reshape(1, GDN_DK))
    return og, sfin


def _gds_kernel(s_ref, q_ref, k_ref, v_ref, beta_ref, eg_ref, so_ref, o_ref):
    s = s_ref[...]
    n = s.shape[-1]
    eye = (lax.broadcasted_iota(jnp.int32, (n, n), 0) == lax.broadcasted_iota(jnp.int32, (n, n), 1))[None]
    kcol = jnp.sum(jnp.where(eye, k_ref[...], 0.0), axis=2, keepdims=True)
    qcol = jnp.sum(jnp.where(eye, q_ref[...], 0.0), axis=2, keepdims=True)
    eg = eg_ref[...]
    ks = jnp.sum(kcol * s, axis=1, keepdims=True)
    v_new = beta_ref[...] * (v_ref[...] - eg * ks)
    sn = s * eg + kcol * v_new
    so_ref[...] = sn
    o_ref[...] = jnp.sum(qcol * sn, axis=1, keepdims=True)


def gdn_decode_step(s0, q, k, v, beta, g):
    bsz, gh, dk, _ = s0.shape
    d = gh * dk
    rows = lambda x: x.reshape(bsz, gh, 1, dk)
    wide = lambda x: jnp.broadcast_to(x[:, :, None, None], (bsz, gh, 1, dk))
    vec = lambda: pl.BlockSpec((None, gh, 1, dk), lambda bi: (bi, 0, 0, 0))
    st = lambda: pl.BlockSpec((None, gh, dk, dk), lambda bi: (bi, 0, 0, 0))
    sn, o = pl.pallas_call(
        _gds_kernel,
        grid=(bsz,),
        in_specs=[st()] + [vec() for _ in range(5)],
        out_specs=[st(), vec()],
        out_shape=[jax.ShapeDtypeStruct(s0.shape, F32), jax.ShapeDtypeStruct((bsz, gh, 1, dk), F32)],
        compiler_params=_cparams("arbitrary"),
        name="gdn_decode",
    )(s0, rows(q), rows(k), rows(v), wide(beta), wide(jnp.exp(g)))
    return o.reshape(bsz, d), sn


def _rwkv_layer(x, modp, mods, nw, j, P, bp, t, shift_prev, s0):
    m, d = x.shape
    mp = bp * t
    bs = m - mp
    nh = d // RWKV_N
    xs, hlast, h_s = pre_rwkv(x, modp[:, 0:2], mods[1], mods[0], nw, P['rw_mix'][j], shift_prev, t)
    tn = min(d, 512)
    mm = lambda a, w, widx, n, tnn, **kw: pmatmul(a, w, widx, n_out=n, tn=tnn, **kw)
    r = mm(xs[0], P['rw_w_rkv'], (j, 0), d, tn)
    k = mm(xs[1], P['rw_w_rkv'], (j, 1), d, tn)
    v = mm(xs[2], P['rw_w_rkv'], (j, 2), d, tn)
    lora = P['rw_w1'].shape[-1]
    wl = mm(mm(xs[3], P['rw_w1'], (j,), lora, lora, act="tanh", out_dtype=BF16), P['rw_w2'], (j,), d, tn)
    al = mm(mm(xs[4], P['rw_a1'], (j,), lora, lora, out_dtype=BF16), P['rw_a2'], (j,), d, tn)
    gl = P['rw_g1'].shape[-1]
    glp = -(-gl // LANES) * LANES
    g1 = jnp.pad(P['rw_g1'][j], ((0, 0), (0, glp - gl)))
    g2 = jnp.pad(P['rw_g2'][j], ((0, glp - gl), (0, 0)))
    g = mm(mm(xs[5], g1, (), glp, glp, act="sigmoid", out_dtype=BF16), g2, (), d, tn)
    prm = [P[n][j] for n in ('rw_w0', 'rw_a0', 'rw_k_k', 'rw_k_a')] + [P['rw_r_k'][j].reshape(d)]
    ln_w, ln_b = P['rw_ln_w'][j], P['rw_ln_b'][j]
    yg_p, sp = rwkv_prompt_scan(r, k, v, wl, al, g, *prm, ln_w, ln_b, bp, t)

    rs, ks, vs, wls, als, gs = (a[mp:] for a in (r, k, v, wl, al, g))
    w_log = -jax.nn.softplus(-(prm[0] + wls)) - 0.5
    dec = jnp.exp(-jnp.exp(w_log))
    a = jax.nn.sigmoid(prm[1] + als)
    hv = lambda z: z.reshape(bs, nh, RWKV_N)
    kx = hv(ks * prm[2])
    kap = kx * lax.rsqrt(jnp.sum(kx * kx, axis=-1, keepdims=True) + 1e-6)
    k_mod = ks * (1.0 + (a - 1.0) * prm[3])
    ys, ss = rwkv_decode_step(s0, hv(dec), kap, kap * hv(a), hv(k_mod), hv(vs), hv(rs))
    mu = jnp.mean(ys, axis=-1, keepdims=True)
    var = jnp.mean(jnp.square(ys - mu), axis=-1, keepdims=True)
    yn = ((ys - mu) * lax.rsqrt(var + RWKV_LN_EPS)).reshape(bs, d) * ln_w + ln_b
    bonus = jnp.sum(hv(rs) * hv(k_mod) * prm[4].reshape(nh, RWKV_N), axis=-1, keepdims=True) * hv(vs)
    yg_s = ((yn + bonus.reshape(bs, d)) * gs).astype(BF16)

    out = mm(jnp.concatenate([yg_p, yg_s], axis=0), P['rw_w_o'], (j,), d, tn)
    return out, (sp, ss), (hlast[:, 0], h_s)


def _gdn_layer(h, j, P, bp, t, conv_prev, s0):
    m, d = h.shape
    mp = bp * t
    bs = m - mp
    gh = d // GDN_DK
    cdim = 3 * d
    tn = min(d, 512)
    w_in = P['gd_w_in']
    main = pmatmul(h, w_in, (j,), n_out=4 * d, tn=tn)
    w_tail = jnp.pad(w_in[j][:, 4 * d:], ((0, 0), (0, LANES - 2 * gh)))
    tail = pmatmul(h, w_tail, (), n_out=LANES, tn=LANES)
    qkv = gdn_conv(main, P['gd_conv_w'][j], jnp.swapaxes(conv_prev, 0, 1), bp, t)
    beta = jax.nn.sigmoid(tail[:, :gh])
    g = -jnp.exp(P['gd_A_log'][j]) * jax.nn.softplus(tail[:, gh:2 * gh] + P['gd_dt_bias'][j])
    norm_w = P['gd_norm_w'][j]
    og_p, sp = gdn_prompt_scan(qkv, main, g[:mp].reshape(bp, t, gh), beta[:mp].reshape(bp, t, gh), norm_w, bp, t)

    o_s, ss = gdn_decode_step(s0, qkv[0, mp:], qkv[1, mp:], qkv[2, mp:], beta[mp:], g[mp:])
    oh = o_s.reshape(bs, gh, GDN_DK)
    oh = oh * lax.rsqrt(jnp.mean(oh * oh, axis=-1, keepdims=True) + RMS_EPS) * norm_w
    og_s = (oh.reshape(bs, d) * jax.nn.silu(main[mp:, cdim:])).astype(BF16)

    out = pmatmul(jnp.concatenate([og_p, og_s], axis=0), P['gd_w_out'], (j,), n_out=d, tn=tn)
    conv_p = jnp.stack([main[(b + 1) * t - (GDN_CONV - 1):(b + 1) * t, :cdim] for b in range(bp)])
    conv_s = jnp.concatenate([conv_prev[:, 1:], main[mp:, None, :cdim]], axis=1)
    return out, (sp, ss), (conv_p, conv_s)


def kernel(x_prompt, x_sample, state_rwkv, state_rwkv_shift, state_gdn, state_gdn_conv, c_prompt, c_sample,
           w_ada, b_ada, norm1_w, norm2_w, rw_mix, rw_w_rkv, rw_w0, rw_w1, rw_w2, rw_a0, rw_a1, rw_a2,
           rw_g1, rw_g2, rw_k_k, rw_k_a, rw_r_k, rw_ln_w, rw_ln_b, rw_w_o, gd_w_in, gd_conv_w, gd_A_log,
           gd_dt_bias, gd_norm_w, gd_w_out, ffn_w_gate, ffn_w_up, ffn_w_down, final_norm_w):
    P = {
        'rw_mix': rw_mix, 'rw_w_rkv': rw_w_rkv, 'rw_w0': rw_w0, 'rw_w1': rw_w1, 'rw_w2': rw_w2,
        'rw_a0': rw_a0, 'rw_a1': rw_a1, 'rw_a2': rw_a2, 'rw_g1': rw_g1, 'rw_g2': rw_g2,
        'rw_k_k': rw_k_k, 'rw_k_a': rw_k_a, 'rw_r_k': rw_r_k, 'rw_ln_w': rw_ln_w, 'rw_ln_b': rw_ln_b,
        'rw_w_o': rw_w_o, 'gd_w_in': gd_w_in, 'gd_conv_w': gd_conv_w, 'gd_A_log': gd_A_log,
        'gd_dt_bias': gd_dt_bias, 'gd_norm_w': gd_norm_w, 'gd_w_out': gd_w_out,
    }
    bp, t, d = x_prompt.shape
    bs = x_sample.shape[0]
    depth = w_ada.shape[0]
    dff = ffn_w_gate.shape[-1]
    mp = bp * t
    assert x_sample.shape[1] == 1 and t % CHUNK == 0 and t % bs == 0 and bs % 16 == 0 and d % (2 * LANES) == 0

    x = jnp.concatenate([x_prompt.reshape(mp, d), x_sample.reshape(bs, d)], axis=0)
    c_all = jnp.concatenate([c_prompt, c_sample], axis=0)
    nb = bp + bs
    nbp = -(-nb // 16) * 16
    c_act = jnp.pad(jax.nn.silu(c_all), ((0, nbp - nb), (0, 0))).astype(BF16)
    tn_d = min(d, 512)
    tn_f = _pick_tile(dff, 256, LANES)
    tn_o = min(d, 512)
    modp, mods = [], []
    for layer in range(depth):
        mod = (pmatmul(c_act, w_ada, (layer,), n_out=6 * d, tn=tn_d)[:nb] + b_ada[layer]).reshape(nb, 6, d)
        modp.append(mod[:bp])
        mods.append(jnp.swapaxes(mod[bp:], 0, 1))

    new_rw_s, new_rw_shift, new_gd_s, new_gd_conv = [], [], [], []
    h = None
    for layer in range(depth):
        j = layer // 2
        mpl, msl = modp[layer], mods[layer]
        if layer % 2 == 0:
            out, s_new, sh_new = _rwkv_layer(x, mpl, msl, norm1_w[layer], j, P, bp, t,
                                             state_rwkv_shift[j], state_rwkv[j])
            new_rw_s.append(s_new)
            new_rw_shift.append(sh_new)
        else:
            if h is None:
                _, h = resnorm(x, jnp.zeros_like(x), jnp.stack([mpl[:, 2], mpl[:, 1], mpl[:, 0]], axis=1),
                               msl[2], msl[1], msl[0], norm1_w[layer], t)
            out, s_new, cb_new = _gdn_layer(h, j, P, bp, t, state_gdn_conv[j], state_gdn[j])
            new_gd_s.append(s_new)
            new_gd_conv.append(cb_new)
        x, h = resnorm(x, out, jnp.stack([mpl[:, 2], mpl[:, 4], mpl[:, 3]], axis=1),
                       msl[2], msl[4], msl[3], norm2_w[layer], t)
        act = pmatmul(h, ffn_w_gate, (layer,), n_out=dff, tn=tn_f, w2=ffn_w_up, out_dtype=BF16)
        kh = dff // 2
        part = pmatmul(act, ffn_w_down, (layer,), n_out=d, tn=tn_o, k0=0, kk=kh, single_buffer_w=True)
        ffn = pmatmul(act, ffn_w_down, (layer,), n_out=d, tn=tn_o, k0=kh, kk=kh, add=part, single_buffer_w=True)
        if layer + 1 < depth:
            nxt_p, nxt_s = modp[layer + 1], mods[layer + 1]
            x, h = resnorm(x, ffn, jnp.stack([mpl[:, 5], nxt_p[:, 1], nxt_p[:, 0]], axis=1),
                           msl[5], nxt_s[1], nxt_s[0], norm1_w[layer + 1], t, emit_h=(layer + 1) % 2 == 1)
        else:
            y_p, y_s = final_norm(x, ffn, mpl[:, 5:6], msl[5], final_norm_w, t)
    grp = lambda lst, i: jnp.stack([e[i] for e in lst])
    return (y_p.reshape(bp, t, d), y_s.reshape(bs, 1, d),
            grp(new_rw_s, 0), grp(new_rw_shift, 0), grp(new_gd_s, 0), grp(new_gd_conv, 0),
            grp(new_rw_s, 1), grp(new_rw_shift, 1), grp(new_gd_s, 1), grp(new_gd_conv, 1))
```

```python
import functools

import jax
import jax.numpy as jnp
from jax import lax
from jax.experimental import pallas as pl
from jax.experimental.pallas import tpu as pltpu

F32, BF16 = jnp.float32, jnp.bfloat16

RMS_EPS = 1e-6
RWKV_LN_EPS = 64e-5
RWKV_N = 64
GDN_DK = 128
GDN_CONV = 4
CHUNK = 64
LANES = 128
SUBLANES = 8
GROUP = 16
VMEM_LIMIT = 56 * 2**20


def _cparams(*sem):
    return pltpu.CompilerParams(dimension_semantics=sem, vmem_limit_bytes=VMEM_LIMIT)


def _pick_tile(n, cap, mult):
    best = None
    for t in range(mult, min(n, cap) + 1, mult):
        if n % t == 0:
            best = t
    return best or n


def _group(n):
    g = GROUP
    while n % g:
        g //= 2
    return g


def _dot(a, b):
    return jnp.dot(a.astype(BF16), b.astype(BF16), preferred_element_type=F32)


def _dot_nt(a, b):
    return lax.dot_general(a.astype(BF16), b.astype(BF16), (((1,), (1,)), ((), ())),
                           preferred_element_type=F32)


def _dot_tn(a, b):
    return lax.dot_general(a.astype(BF16), b.astype(BF16), (((0,), (0,)), ((), ())),
                           preferred_element_type=F32)


def _split3(x):
    hi = x.astype(BF16)
    r1 = x - hi.astype(F32)
    mid = r1.astype(BF16)
    lo = (r1 - mid.astype(F32)).astype(BF16)
    return hi, mid, lo


def _tri(n, transpose=False):
    row = lax.broadcasted_iota(jnp.int32, (n, n), 0)
    col = lax.broadcasted_iota(jnp.int32, (n, n), 1)
    m = (row <= col) if transpose else (row >= col)
    return jnp.where(m, 1.0, 0.0).astype(BF16)


def _row_to_col(row):
    n = row.shape[1]
    eye = lax.broadcasted_iota(jnp.int32, (n, n), 0) == lax.broadcasted_iota(jnp.int32, (n, n), 1)
    return jnp.sum(jnp.where(eye, jnp.broadcast_to(row, (n, n)), 0.0), axis=1, keepdims=True)


def _segsum(x, width, exact):
    l = x.shape[1]
    gi = lax.broadcasted_iota(jnp.int32, (LANES, LANES), 0) // width
    gj = lax.broadcasted_iota(jnp.int32, (LANES, LANES), 1) // width
    bd = jnp.where(gi == gj, 1.0, 0.0).astype(BF16)
    hi = x.astype(BF16)
    lo = (x - hi.astype(F32)).astype(BF16) if exact else None
    outs = []
    for t in range(l // LANES):
        sl = slice(t * LANES, (t + 1) * LANES)
        s = jnp.dot(hi[:, sl], bd, preferred_element_type=F32)
        if exact:
            s = s + jnp.dot(lo[:, sl], bd, preferred_element_type=F32)
        outs.append(s)
    return outs[0] if len(outs) == 1 else jnp.concatenate(outs, axis=1)


def _modnorm(x, nw, scale, shift):
    y = x * lax.rsqrt(jnp.mean(x * x, axis=-1, keepdims=True) + RMS_EPS) * nw
    return y * (1.0 + scale) + shift


A_SLOTS = 3


def _mm_kernel(*refs, n_w, has_add, act, k0, ni, nsteps):
    a_hbm = refs[0]
    w_refs = refs[1:1 + n_w]
    pos = 1 + n_w
    add_ref = refs[pos] if has_add else None
    pos += int(has_add)
    o_ref = refs[pos]
    wbf = refs[pos + 1:pos + 1 + n_w]
    a_buf, a_sem = refs[pos + 1 + n_w:]
    tm, kk = a_buf.shape[1:]
    step = pl.program_id(0) * ni + pl.program_id(1)

    def a_copy(s):
        row = pl.multiple_of((s % ni) * tm, 16)
        slot = s % A_SLOTS
        return pltpu.make_async_copy(a_hbm.at[pl.ds(row, tm), pl.ds(k0, kk)], a_buf.at[slot], a_sem.at[slot])

    @pl.when(step == 0)
    def _():
        for s in range(min(A_SLOTS - 1, nsteps)):
            a_copy(s).start()

    @pl.when(step + (A_SLOTS - 1) < nsteps)
    def _():
        a_copy(step + (A_SLOTS - 1)).start()

    @pl.when(pl.program_id(1) == 0)
    def _():
        for w, s in zip(w_refs, wbf):
            s[...] = w[...].astype(BF16)

    a_copy(step).wait()
    a = a_buf[step % A_SLOTS]
    y = jnp.dot(a, wbf[0][...], preferred_element_type=F32)
    if n_w == 2:
        u = jnp.dot(a, wbf[1][...], preferred_element_type=F32)
        y = y * jax.nn.sigmoid(y) * u
    if has_add:
        y = y + add_ref[...]
    if act == "tanh":
        y = jnp.tanh(y)
    elif act == "sigmoid":
        y = jax.nn.sigmoid(y)
    o_ref[...] = y.astype(o_ref.dtype)


def pmatmul(a, w, widx=(), *, n_out, tn, k0=0, kk=None, w2=None, add=None, act=None,
            out_dtype=F32, tm_cap=832, single_buffer_w=False):
    m = a.shape[0]
    kk = a.shape[1] if kk is None else kk
    kb = k0 // kk
    tm = _pick_tile(m, tm_cap, 16)
    ws = [w] if w2 is None else [w, w2]
    nlead = len(widx)
    grid = (n_out // tn, m // tm)
    in_specs = [pl.BlockSpec(memory_space=pl.ANY)]
    wmode = dict(pipeline_mode=pl.Buffered(1)) if single_buffer_w else {}
    for _ in ws:
        in_specs.append(pl.BlockSpec((None,) * nlead + (kk, tn),
                                     lambda j, i: tuple(widx) + (kb, j), **wmode))
    args = [a] + ws
    if add is not None:
        in_specs.append(pl.BlockSpec((tm, tn), lambda j, i: (i, j)))
        args.append(add)
    return pl.pallas_call(
        functools.partial(_mm_kernel, n_w=len(ws), has_add=add is not None, act=act, k0=k0,
                          ni=grid[1], nsteps=grid[0] * grid[1]),
        grid=grid,
        in_specs=in_specs,
        out_specs=pl.BlockSpec((tm, tn), lambda j, i: (i, j)),
        out_shape=jax.ShapeDtypeStruct((m, n_out), out_dtype),
        scratch_shapes=[pltpu.VMEM((kk, tn), BF16) for _ in ws] + [
            pltpu.VMEM((A_SLOTS, tm, kk), BF16), pltpu.SemaphoreType.DMA((A_SLOTS,))],
        compiler_params=_cparams("arbitrary", "arbitrary"),
        name=f"mm_m{m}_k{kk}_n{n_out}" + ("_glu" if w2 is not None else "") + ("_add" if add is not None else ""),
    )(*args)


def _tile_specs(tr, d, np_tiles, tps, nbp):
    tok = lambda: pl.BlockSpec((tr, d), lambda i: (i, 0))
    seq = lambda n: pl.BlockSpec((None, n, d), lambda i: (jnp.minimum(i // tps, nbp - 1), 0, 0))
    samp = lambda: pl.BlockSpec((tr, d), lambda i: (0, 0))
    row = lambda n=1: pl.BlockSpec((n, d), lambda i: (0, 0))
    return tok, seq, samp, row


def _pre_rwkv_kernel(x_ref, mp_ref, scs_ref, shs_ref, nw_ref, mix_ref, prev_s_ref,
                     o0, o1, o2, o3, o4, o5, hlast_ref, hs_ref, carry_ref, *, np_tiles, tps):
    i = pl.program_id(0)
    is_s = i == np_tiles
    shift = jnp.where(is_s, shs_ref[...], mp_ref[0:1, :])
    scale = jnp.where(is_s, scs_ref[...], mp_ref[1:2, :])
    h = _modnorm(x_ref[...], nw_ref[...], scale, shift)
    tr = h.shape[0]

    @pl.when(i % tps == 0)
    def _():
        carry_ref[...] = jnp.zeros_like(carry_ref)

    rowi = lax.broadcasted_iota(jnp.int32, h.shape, 0)
    prev_p = jnp.where(rowi == 0, carry_ref[...], pltpu.roll(h, 1, axis=0))
    prev = jnp.where(is_s, prev_s_ref[...], prev_p)
    xx = prev - h
    for n, o in enumerate((o0, o1, o2, o3, o4, o5)):
        o[...] = (h + xx * mix_ref[n:n + 1, :]).astype(BF16)
    last = h[tr - 1:tr, :]
    carry_ref[...] = last

    @pl.when(jnp.logical_not(is_s))
    def _():
        hlast_ref[...] = last

    @pl.when(is_s)
    def _():
        hs_ref[...] = h


def pre_rwkv(x, modp, scale_s, shift_s, nw, mix, prev_s, t):
    m, d = x.shape
    tr = scale_s.shape[0]
    nbp = modp.shape[0]
    tps = t // tr
    np_tiles = nbp * tps
    tok, seq, samp, row = _tile_specs(tr, d, np_tiles, tps, nbp)
    outs = pl.pallas_call(
        functools.partial(_pre_rwkv_kernel, np_tiles=np_tiles, tps=tps),
        grid=(np_tiles + 1,),
        in_specs=[tok(), seq(2), samp(), samp(), row(), row(6), samp()],
        out_specs=[tok() for _ in range(6)] + [seq(1), samp()],
        out_shape=[jax.ShapeDtypeStruct((m, d), BF16)] * 6 + [jax.ShapeDtypeStruct((nbp, 1, d), F32),
                                                              jax.ShapeDtypeStruct((tr, d), F32)],
        scratch_shapes=[pltpu.VMEM((1, d), F32)],
        compiler_params=_cparams("arbitrary"),
        name="pre_rwkv",
    )(x, modp, scale_s, shift_s, nw.reshape(1, d), mix, prev_s)
    return outs[:6], outs[6], outs[7]


def _resnorm_kernel(x_ref, dl_ref, mp_ref, gs_ref, scs_ref, shs_ref, nw_ref, xo_ref, *h_refs, np_tiles):
    is_s = pl.program_id(0) == np_tiles
    gate = jnp.where(is_s, gs_ref[...], mp_ref[0:1, :])
    xn = x_ref[...] + gate * dl_ref[...]
    xo_ref[...] = xn
    if h_refs:
        scale = jnp.where(is_s, scs_ref[...], mp_ref[1:2, :])
        shift = jnp.where(is_s, shs_ref[...], mp_ref[2:3, :])
        h_refs[0][...] = _modnorm(xn, nw_ref[...], scale, shift).astype(BF16)


def resnorm(x, delta, modp, gate_s, scale_s, shift_s, nw, t, emit_h=True):
    m, d = x.shape
    tr = gate_s.shape[0]
    nbp = modp.shape[0]
    tps = t // tr
    np_tiles = nbp * tps
    tok, seq, samp, row = _tile_specs(tr, d, np_tiles, tps, nbp)
    outs = pl.pallas_call(
        functools.partial(_resnorm_kernel, np_tiles=np_tiles),
        grid=(np_tiles + 1,),
        in_specs=[tok(), tok(), seq(3), samp(), samp(), samp(), row()],
        out_specs=[tok()] + ([tok()] if emit_h else []),
        out_shape=[jax.ShapeDtypeStruct((m, d), F32)] + ([jax.ShapeDtypeStruct((m, d), BF16)] if emit_h else []),
        compiler_params=_cparams("arbitrary"),
        name="resnorm",
    )(x, delta, modp, gate_s, scale_s, shift_s, nw.reshape(1, d))
    return (outs[0], outs[1]) if emit_h else (outs[0], None)


def _final_kernel(x_ref, dl_ref, mp_ref, gs_ref, nw_ref, yp_ref, ys_ref, *, np_tiles):
    is_s = pl.program_id(0) == np_tiles
    gate = jnp.where(is_s, gs_ref[...], mp_ref[0:1, :])
    xn = x_ref[...] + gate * dl_ref[...]
    y = xn * lax.rsqrt(jnp.mean(xn * xn, axis=-1, keepdims=True) + RMS_EPS) * nw_ref[...]

    @pl.when(jnp.logical_not(is_s))
    def _():
        yp_ref[...] = y

    @pl.when(is_s)
    def _():
        ys_ref[...] = y


def final_norm(x, delta, modp, gate_s, nw, t):
    m, d = x.shape
    tr = gate_s.shape[0]
    nbp = modp.shape[0]
    tps = t // tr
    np_tiles = nbp * tps
    tok, seq, samp, row = _tile_specs(tr, d, np_tiles, tps, nbp)
    return pl.pallas_call(
        functools.partial(_final_kernel, np_tiles=np_tiles),
        grid=(np_tiles + 1,),
        in_specs=[tok(), tok(), seq(1), samp(), row()],
        out_specs=[pl.BlockSpec((tr, d), lambda i: (jnp.minimum(i, np_tiles - 1), 0)), samp()],
        out_shape=[jax.ShapeDtypeStruct((np_tiles * tr, d), F32), jax.ShapeDtypeStruct((tr, d), F32)],
        compiler_params=_cparams("arbitrary"),
        name="final_norm",
    )(x, delta, modp, gate_s, nw.reshape(1, d))


def _gdn_conv_kernel(x_ref, w_ref, buf_ref, o_ref, prev_ref, *, np_tiles, tps):
    sec = pl.program_id(0)
    i = pl.program_id(1)
    is_s = i == np_tiles

    x = x_ref[...]

    @pl.when(i % tps == 0)
    def _():
        prev_ref[...] = jnp.zeros_like(prev_ref)

    def tap(k):
        return w_ref[GDN_CONV - 1 - k:GDN_CONV - k, :]

    def finish(y):
        a = y * jax.nn.sigmoid(y)
        qscale = jnp.where(sec == 0, GDN_DK ** -0.5, 1.0)
        nrm = a * (lax.rsqrt(_segsum(a * a, GDN_DK, exact=False) + 1e-6) * qscale)
        o_ref[...] = jnp.where(sec == 2, a, nrm)

    @pl.when(is_s)
    def _():
        y = x * tap(0)
        for k in range(1, GDN_CONV):
            y = y + buf_ref[GDN_CONV - 1 - k] * tap(k)
        finish(y)

    @pl.when(jnp.logical_not(is_s))
    def _():
        prev = prev_ref[...]
        rowi = lax.broadcasted_iota(jnp.int32, x.shape, 0)
        y = x * tap(0)
        for k in range(1, GDN_CONV):
            back = jnp.where(rowi >= k, pltpu.roll(x, k, axis=0), pltpu.roll(prev, k, axis=0))
            y = y + back * tap(k)
        prev_ref[...] = x
        finish(y)


def gdn_conv(main, conv_w, bufs, nbp, t):
    m = main.shape[0]
    d = main.shape[1] // 4
    tr = bufs.shape[1]
    tps = t // tr
    np_tiles = nbp * tps
    return pl.pallas_call(
        functools.partial(_gdn_conv_kernel, np_tiles=np_tiles, tps=tps),
        grid=(3, np_tiles + 1),
        in_specs=[pl.BlockSpec((tr, d), lambda s, i: (i, s)),
                  pl.BlockSpec((GDN_CONV, d), lambda s, i: (0, s)),
                  pl.BlockSpec((GDN_CONV - 1, tr, d), lambda s, i: (0, 0, s))],
        out_specs=pl.BlockSpec((None, tr, d), lambda s, i: (s, i, 0)),
        out_shape=jax.ShapeDtypeStruct((3, m, d), F32),
        scratch_shapes=[pltpu.VMEM((tr, d), F32)],
        compiler_params=_cparams("arbitrary", "arbitrary"),
        name="gdn_conv",
    )(main, conv_w, bufs)


def _inv_kernel(m_ref, n_ref, *, c):
    nblk = c // SUBLANES
    zeros = jnp.zeros((SUBLANES * c, LANES), F32)
    for ib in range(nblk):
        n_ref[ib * SUBLANES * c:(ib + 1) * SUBLANES * c, :] = zeros

        def row(ii, carry, ib=ib):
            base = pl.multiple_of((ib * SUBLANES + ii) * c, c)
            acc = [m_ref[pl.ds(base + SUBLANES * k, SUBLANES), :] for k in range(ib + 1)]
            for j in range((ib + 1) * SUBLANES):
                coef = m_ref[pl.ds(base + j, 1), :]
                for k in range(j // SUBLANES + 1):
                    acc[k] = acc[k] + coef * n_ref[j * c + SUBLANES * k:j * c + SUBLANES * (k + 1), :]
            for k in range(ib + 1):
                n_ref[pl.ds(base + SUBLANES * k, SUBLANES), :] = -acc[k]
            return carry

        lax.fori_loop(0, SUBLANES, row, 0)


def tri_inverse(mats):
    shape = mats.shape
    c = shape[-1]
    u = 1
    for s in shape[:-2]:
        u *= s
    up = -(-u // LANES) * LANES
    flat = jnp.transpose(mats.reshape(u, c, c), (1, 2, 0)).reshape(c * c, u)
    if up != u:
        flat = jnp.pad(flat, ((0, 0), (0, up - u)))
    out = pl.pallas_call(
        functools.partial(_inv_kernel, c=c),
        grid=(up // LANES,),
        in_specs=[pl.BlockSpec((c * c, LANES), lambda g: (0, g))],
        out_specs=pl.BlockSpec((c * c, LANES), lambda g: (0, g)),
        out_shape=jax.ShapeDtypeStruct((c * c, up), F32),
        compiler_params=_cparams("arbitrary"),
        name="tri_inverse",
    )(flat)
    return jnp.transpose(out[:, :u].reshape(c, c, u), (2, 0, 1)).reshape(shape)


def _rwkv_token_prep(r, k, wl, al, w0, a0, k_k, k_a, r_k):
    w_log = -jax.nn.softplus(-(w0 + wl)) - 0.5
    lw = -jnp.exp(w_log)
    a = jax.nn.sigmoid(a0 + al)
    kx = k * k_k
    kap = kx * lax.rsqrt(_segsum(kx * kx, RWKV_N, exact=False) + 1e-6)
    k_mod = k * (1.0 + (a - 1.0) * k_a)
    rk = _segsum(r * k_mod * r_k, RWKV_N, exact=True)
    return lw, kap, k_mod, kap * a, rk


def _rw1_kernel(r_ref, k_ref, v_ref, wl_ref, al_ref, w0_ref, a0_ref, kk_ref, ka_ref, rk_ref,
                kaph_ref, rh_ref, kh_ref, bh_ref, khp_ref, bhp_ref, vb_ref, bonus_ref, pc_ref, mb_ref):
    v = v_ref[...]
    lw, kap, k, b, rk = _rwkv_token_prep(r_ref[...], k_ref[...], wl_ref[...], al_ref[...], w0_ref[...],
                                         a0_ref[...], kk_ref[...], ka_ref[...], rk_ref[...])
    bonus_ref[...] = rk * v
    vb_ref[...] = v.astype(BF16)
    c, lb = lw.shape
    tri = _tri(c)
    hi, mid, lo = _split3(lw)
    p = (jnp.dot(tri, hi, preferred_element_type=F32) + jnp.dot(tri, mid, preferred_element_type=F32)
         + jnp.dot(tri, lo, preferred_element_type=F32))
    pc = p[c - 1:c, :]
    en = jnp.exp(-p)
    ec = jnp.exp(pc - p)
    kaph = kap * jnp.exp(p - lw)
    bh = (b * en).astype(BF16)
    kaph_ref[...] = kaph.astype(BF16)
    rh_ref[...] = (r_ref[...] * jnp.exp(p)).astype(BF16)
    kh_ref[...] = (k * en).astype(BF16)
    bh_ref[...] = bh
    khp_ref[...] = (k * ec).astype(BF16)
    bhp_ref[...] = (b * ec).astype(BF16)
    pc_ref[...] = jnp.exp(pc)
    lane = lax.broadcasted_iota(jnp.int32, (c, LANES), 1)
    m0 = lane < RWKV_N
    rowi = lax.broadcasted_iota(jnp.int32, (2 * c, c), 0) % c
    coli = lax.broadcasted_iota(jnp.int32, (2 * c, c), 1)
    strict = rowi > coli
    for jt in range(lb // LANES):
        kp = kaph[:, jt * LANES:(jt + 1) * LANES]
        bp = bh[:, jt * LANES:(jt + 1) * LANES]
        lhs = jnp.concatenate([jnp.where(m0, kp, 0.0), jnp.where(m0, 0.0, kp)], axis=0)
        g = _dot_nt(lhs, bp)
        g = jnp.where(strict, g, 0.0)
        mb_ref[2 * jt] = g[:c]
        mb_ref[2 * jt + 1] = g[c:]


def _rw3_kernel(kaph_ref, rh_ref, kh_ref, bh_ref, khp_ref, bhp_ref, v_ref, pc_ref, n_ref,
                bonus_ref, g_ref, lnw_ref, lnb_ref, yg_ref, sfin_ref, a_ref):
    ci = pl.program_id(1)
    c = kaph_ref.shape[0]
    npair = a_ref.shape[0]

    @pl.when(ci == 0)
    def _():
        a_ref[...] = jnp.zeros_like(a_ref)

    lane2 = lax.broadcasted_iota(jnp.int32, (2 * c, LANES), 1)
    m0_2 = lane2 < RWKV_N
    m0 = lax.broadcasted_iota(jnp.int32, (c, LANES), 1) < RWKV_N
    row = lax.broadcasted_iota(jnp.int32, (c, c), 0)
    col = lax.broadcasted_iota(jnp.int32, (c, c), 1)
    strict = row > col
    incl = row >= col
    rr = lax.broadcasted_iota(jnp.int32, (LANES, LANES), 0)
    cc = lax.broadcasted_iota(jnp.int32, (LANES, LANES), 1)
    blockdiag = (rr < RWKV_N) == (cc < RWKV_N)
    ng = _group(npair)
    halves = (0, 1)

    def group(gi, carry):
        js = [gi * ng + g for g in range(ng)]
        lss = [pl.ds(pl.multiple_of(j * LANES, LANES), LANES) for j in js]
        lhs = [jnp.concatenate([kaph_ref[:, ls], rh_ref[:, ls]], axis=0) for ls in lss]
        vv = [v_ref[:, ls] for ls in lss]
        a0 = [a_ref[j] for j in js]
        x = [_dot(l, a) for l, a in zip(lhs, a0)]
        lhs_m = []
        for l in lhs:
            l32 = l.astype(F32)
            lhs_m.append((jnp.where(m0_2, l32, 0.0).astype(BF16), jnp.where(m0_2, 0.0, l32).astype(BF16)))
        gk = [[_dot_nt(lhs_m[g][h], kh_ref[:, lss[g]]) for h in halves] for g in range(ng)]
        gb = [[_dot_nt(lhs_m[g][h][c:], bh_ref[:, lss[g]]) for h in halves] for g in range(ng)]
        mk = [[jnp.where(strict, gk[g][h][:c], 0.0).astype(BF16) for h in halves] for g in range(ng)]
        lrk = [[jnp.where(incl, gk[g][h][c:], 0.0).astype(BF16) for h in halves] for g in range(ng)]
        lrb = [[jnp.where(incl, gb[g][h], 0.0).astype(BF16) for h in halves] for g in range(ng)]
        mkv = [[_dot(mk[g][h], vv[g]) for h in halves] for g in range(ng)]
        lrkv = [[_dot(lrk[g][h], vv[g]) for h in halves] for g in range(ng)]
        kv = [_dot_tn(khp_ref[:, lss[g]], vv[g]) for g in range(ng)]
        rhs = [[x[g][:c] + mkv[g][h] for h in halves] for g in range(ng)]
        nr = [[_dot(n_ref[2 * js[g] + h], rhs[g][h]) for h in halves] for g in range(ng)]
        uh = [[rhs[g][h] + nr[g][h] for h in halves] for g in range(ng)]
        u = [jnp.where(m0, uh[g][0], uh[g][1]) for g in range(ng)]
        lu = [[_dot(lrb[g][h], uh[g][h]) for h in halves] for g in range(ng)]
        bu = [_dot_tn(bhp_ref[:, lss[g]], u[g]) for g in range(ng)]
        y = [x[g][c:] + jnp.where(m0, lrkv[g][0] - lu[g][0], lrkv[g][1] - lu[g][1]) for g in range(ng)]
        mu = [_segsum(y[g], RWKV_N, exact=True) * (1.0 / RWKV_N) for g in range(ng)]
        yc = [y[g] - mu[g] for g in range(ng)]
        var = [_segsum(yc[g] * yc[g], RWKV_N, exact=False) * (1.0 / RWKV_N) for g in range(ng)]
        for g in range(ng):
            ls = lss[g]
            yn = yc[g] * lax.rsqrt(var[g] + RWKV_LN_EPS) * lnw_ref[:, ls] + lnb_ref[:, ls]
            yg_ref[:, ls] = ((yn + bonus_ref[:, ls]) * g_ref[:, ls]).astype(BF16)
            pcc = _row_to_col(pc_ref[:, ls])
            a_ref[js[g]] = pcc * a0[g] + jnp.where(blockdiag, kv[g] - bu[g], 0.0)
        return carry

    lax.fori_loop(0, npair // ng, group, 0)

    @pl.when(ci == pl.num_programs(1) - 1)
    def _():
        sfin_ref[...] = a_ref[...]


def rwkv_prompt_scan(r, k, v, wl, al, g, w0, a0, k_k, k_a, r_k, ln_w, ln_b, bsz, t):
    d = r.shape[1]
    c = CHUNK
    nc = t // c
    h = d // RWKV_N
    lb = min(d, 4096)
    row2 = lambda x: x.reshape(1, d)
    tok2 = lambda: pl.BlockSpec((c, lb), lambda bi, ci, li: (bi * nc + ci, li))
    prm = lambda: pl.BlockSpec((1, lb), lambda bi, ci, li: (0, li))
    tok = lambda: pl.BlockSpec((None, c, lb), lambda bi, ci, li: (bi, ci, li))
    bf = jax.ShapeDtypeStruct((bsz, t, d), BF16)
    kaph, rh, kh, bh, khp, bhp, vb, bonus, pc, mb = pl.pallas_call(
        _rw1_kernel,
        grid=(bsz, nc, d // lb),
        in_specs=[tok2() for _ in range(5)] + [prm() for _ in range(5)],
        out_specs=[tok() for _ in range(8)] + [
            pl.BlockSpec((None, None, 1, lb), lambda bi, ci, li: (bi, ci, 0, li)),
            pl.BlockSpec((None, None, lb // RWKV_N, c, c), lambda bi, ci, li: (bi, ci, li, 0, 0)),
        ],
        out_shape=[bf] * 7 + [jax.ShapeDtypeStruct((bsz, t, d), F32),
                              jax.ShapeDtypeStruct((bsz, nc, 1, d), F32),
                              jax.ShapeDtypeStruct((bsz, nc, h, c, c), F32)],
        compiler_params=_cparams("arbitrary", "arbitrary", "arbitrary"),
        name="rwkv_chunk_prep",
    )(r, k, v, wl, al, row2(w0), row2(a0), row2(k_k), row2(k_a), row2(r_k))
    nmat = tri_inverse(mb)
    tokd = lambda: pl.BlockSpec((None, c, d), lambda bi, ci: (bi, ci, 0))
    tok2d = lambda: pl.BlockSpec((c, d), lambda bi, ci: (bi * nc + ci, 0))
    prmd = lambda: pl.BlockSpec((1, d), lambda bi, ci: (0, 0))
    yg, sfin = pl.pallas_call(
        _rw3_kernel,
        grid=(bsz, nc),
        in_specs=[tokd() for _ in range(7)] + [
            pl.BlockSpec((None, None, 1, d), lambda bi, ci: (bi, ci, 0, 0)),
            pl.BlockSpec((None, None, h, c, c), lambda bi, ci: (bi, ci, 0, 0, 0)),
            tokd(), tok2d(), prmd(), prmd(),
        ],
        out_specs=[tok2d(), pl.BlockSpec((None, h // 2, LANES, LANES), lambda bi, ci: (bi, 0, 0, 0))],
        out_shape=[jax.ShapeDtypeStruct((bsz * t, d), BF16),
                   jax.ShapeDtypeStruct((bsz, h // 2, LANES, LANES), F32)],
        scratch_shapes=[pltpu.VMEM((h // 2, LANES, LANES), F32)],
        compiler_params=_cparams("arbitrary", "arbitrary"),
        name="rwkv_chunk_scan",
    )(kaph, rh, kh, bh, khp, bhp, vb, pc, nmat, bonus, g, row2(ln_w), row2(ln_b))
    n = RWKV_N
    s_even = sfin[:, :, :n, :n]
    s_odd = sfin[:, :, n:, n:]
    s = jnp.stack([s_even, s_odd], axis=2).reshape(bsz, h, n, n)
    return yg, jnp.swapaxes(s, -1, -2)


def _rws_kernel(s_ref, w_ref, kap_ref, b_ref, k_ref, v_ref, r_ref, so_ref, y_ref):
    s = s_ref[...]
    n = s.shape[-1]
    eye = (lax.broadcasted_iota(jnp.int32, (n, n), 0) == lax.broadcasted_iota(jnp.int32, (n, n), 1))[None]
    sa = -jnp.sum(s * kap_ref[...], axis=2, keepdims=True)
    vcol = jnp.sum(jnp.where(eye, v_ref[...], 0.0), axis=2, keepdims=True)
    sn = s * w_ref[...] + sa * b_ref[...] + vcol * k_ref[...]
    so_ref[...] = sn
    ycol = jnp.sum(sn * r_ref[...], axis=2, keepdims=True)
    y_ref[...] = jnp.sum(jnp.where(eye, ycol, 0.0), axis=1, keepdims=True)


def rwkv_decode_step(s0, w, kap, b, k, v, r):
    bsz, h, n, _ = s0.shape
    vec = lambda: pl.BlockSpec((None, h, 1, n), lambda bi: (bi, 0, 0, 0))
    st = lambda: pl.BlockSpec((None, h, n, n), lambda bi: (bi, 0, 0, 0))
    rows = lambda x: x.reshape(bsz, h, 1, n)
    sn, y = pl.pallas_call(
        _rws_kernel,
        grid=(bsz,),
        in_specs=[st()] + [vec() for _ in range(6)],
        out_specs=[st(), vec()],
        out_shape=[jax.ShapeDtypeStruct(s0.shape, F32), jax.ShapeDtypeStruct((bsz, h, 1, n), F32)],
        compiler_params=_cparams("arbitrary"),
        name="rwkv_decode",
    )(s0, rows(w), rows(kap), rows(b), rows(k), rows(v), rows(r))
    return y.reshape(bsz, h, n), sn


def _gd_head_scalars(g_ref, gcr_s, tri_t):
    hi, mid, lo = _split3(g_ref[...])
    gcr_s[...] = (jnp.dot(hi, tri_t, preferred_element_type=F32)
                  + jnp.dot(mid, tri_t, preferred_element_type=F32)
                  + jnp.dot(lo, tri_t, preferred_element_type=F32))


def _gd1_kernel(k_ref, g_ref, beta_ref, l_ref, gcr_s):
    c = k_ref.shape[0]
    gh = g_ref.shape[0]
    _gd_head_scalars(g_ref, gcr_s, _tri(c, transpose=True))
    row = lax.broadcasted_iota(jnp.int32, (c, c), 0)
    col = lax.broadcasted_iota(jnp.int32, (c, c), 1)
    strict = row > col
    ng = _group(gh)

    def group(gi, carry):
        js = [gi * ng + g for g in range(ng)]
        gr = [gcr_s[pl.ds(j, 1), :] for j in js]
        gcol = [_row_to_col(x) for x in gr]
        bcol = [_row_to_col(beta_ref[pl.ds(j, 1), :]) for j in js]
        kk = [k_ref[:, pl.ds(pl.multiple_of(j * GDN_DK, GDN_DK), GDN_DK)] for j in js]
        gram = [_dot_nt(kk[g] * bcol[g], kk[g]) for g in range(ng)]
        for g in range(ng):
            dec = jnp.exp(jnp.where(strict, gcol[g] - gr[g], 0.0))
            l_ref[js[g]] = jnp.where(strict, gram[g] * dec, 0.0)
        return carry

    lax.fori_loop(0, gh // ng, group, 0)


def _gd3_kernel(q_ref, k_ref, v_ref, g_ref, beta_ref, n_ref, z_ref, nw_ref, og_ref, sfin_ref, s_ref, gcr_s):
    ci = pl.program_id(1)
    c = k_ref.shape[0]
    gh = g_ref.shape[0]

    @pl.when(ci == 0)
    def _():
        s_ref[...] = jnp.zeros_like(s_ref)

    _gd_head_scalars(g_ref, gcr_s, _tri(c, transpose=True))
    row = lax.broadcasted_iota(jnp.int32, (c, c), 0)
    col = lax.broadcasted_iota(jnp.int32, (c, c), 1)
    incl = row >= col
    ng = _group(gh)
    rng = range(ng)

    def group(gi, carry):
        js = [gi * ng + g for g in rng]
        lss = [pl.ds(pl.multiple_of(j * GDN_DK, GDN_DK), GDN_DK) for j in js]
        gr = [gcr_s[pl.ds(j, 1), :] for j in js]
        gcol = [_row_to_col(x) for x in gr]
        bcol = [_row_to_col(beta_ref[pl.ds(j, 1), :]) for j in js]
        glast = [x[:, c - 1:c] for x in gr]
        q = [q_ref[:, ls] for ls in lss]
        kk = [k_ref[:, ls] for ls in lss]
        vv = [v_ref[:, ls] for ls in lss]
        s = [s_ref[j] for j in js]
        eg = [jnp.exp(x) for x in gcol]
        rhs = [jnp.concatenate([vv[g] * bcol[g], kk[g] * bcol[g] * eg[g]], axis=1) for g in rng]
        qk = [_dot_nt(q[g], kk[g]) for g in rng]
        nr = [_dot(n_ref[js[g]], rhs[g]) for g in rng]
        qs = [_dot(q[g] * eg[g], s[g]) for g in rng]
        sol = [rhs[g] + nr[g] for g in rng]
        ws = [_dot(sol[g][:, GDN_DK:], s[g]) for g in rng]
        v_new = [sol[g][:, :GDN_DK] - ws[g] for g in rng]
        attn = [jnp.where(incl, qk[g] * jnp.exp(jnp.where(incl, gcol[g] - gr[g], 0.0)), 0.0) for g in rng]
        av = [_dot(attn[g], v_new[g]) for g in rng]
        kv = [_dot_tn(kk[g] * jnp.exp(glast[g] - gcol[g]), v_new[g]) for g in rng]
        for g in rng:
            o = qs[g] + av[g]
            on = o * lax.rsqrt(jnp.mean(o * o, axis=-1, keepdims=True) + RMS_EPS) * nw_ref[...]
            z = z_ref[:, lss[g]]
            og_ref[:, lss[g]] = (on * (z * jax.nn.sigmoid(z))).astype(BF16)
            s_ref[js[g]] = s[g] * jnp.exp(glast[g]) + kv[g]
        return carry

    lax.fori_loop(0, gh // ng, group, 0)

    @pl.when(ci == pl.num_programs(1) - 1)
    def _():
        sfin_ref[...] = s_ref[...]


def gdn_prompt_scan(qkv, main, g, beta, norm_w, bsz, t):
    d = qkv.shape[2]
    c = CHUNK
    nc = t // c
    gh = d // GDN_DK
    rows = lambda x: jnp.swapaxes(x.reshape(bsz, nc, c, gh), -1, -2)
    g_r, beta_r = rows(g), rows(beta)
    sect = lambda s: pl.BlockSpec((None, c, d), lambda bi, ci: (s, bi * nc + ci, 0))
    hrow = lambda: pl.BlockSpec((None, None, gh, c), lambda bi, ci: (bi, ci, 0, 0))
    mat = lambda: pl.BlockSpec((None, None, gh, c, c), lambda bi, ci: (bi, ci, 0, 0, 0))
    lmat = pl.pallas_call(
        _gd1_kernel,
        grid=(bsz, nc),
        in_specs=[sect(1), hrow(), hrow()],
        out_specs=mat(),
        out_shape=jax.ShapeDtypeStruct((bsz, nc, gh, c, c), F32),
        scratch_shapes=[pltpu.VMEM((gh, c), F32)],
        compiler_params=_cparams("arbitrary", "arbitrary"),
        name="gdn_chunk_prep",
    )(qkv, g_r, beta_r)
    nmat = tri_inverse(lmat)
    og, sfin = pl.pallas_call(
        _gd3_kernel,
        grid=(bsz, nc),
        in_specs=[sect(0), sect(1), sect(2), hrow(), hrow(), mat(),
                  pl.BlockSpec((c, d), lambda bi, ci: (bi * nc + ci, 3)),
                  pl.BlockSpec((1, GDN_DK), lambda bi, ci: (0, 0))],
        out_specs=[pl.BlockSpec((c, d), lambda bi, ci: (bi * nc + ci, 0)),
                   pl.BlockSpec((None, gh, GDN_DK, GDN_DK), lambda bi, ci: (bi, 0, 0, 0))],
        out_shape=[jax.ShapeDtypeStruct((bsz * t, d), BF16),
                   jax.ShapeDtypeStruct((bsz, gh, GDN_DK, GDN_DK), F32)],
        scratch_shapes=[pltpu.VMEM((gh, GDN_DK, GDN_DK), F32), pltpu.VMEM((gh, c), F32)],
        compiler_params=_cparams("arbitrary", "arbitrary"),
        name="gdn_chunk_scan",
    )(qkv, qkv, qkv, g_r, beta_r, nmat, main, norm_w.reshape(1, GDN_DK))
    return og, sfin


def _gds_kernel(s_ref, q_ref, k_ref, v_ref, beta_ref, eg_ref, so_ref, o_ref):
    s = s_ref[...]
    n = s.shape[-1]
    eye = (lax.broadcasted_iota(jnp.int32, (n, n), 0) == lax.broadcasted_iota(jnp.int32, (n, n), 1))[None]
    kcol = jnp.sum(jnp.where(eye, k_ref[...], 0.0), axis=2, keepdims=True)
    qcol = jnp.sum(jnp.where(eye, q_ref[...], 0.0), axis=2, keepdims=True)
    eg = eg_ref[...]
    ks = jnp.sum(kcol * s, axis=1, keepdims=True)
    v_new = beta_ref[...] * (v_ref[...] - eg * ks)
    sn = s * eg + kcol * v_new
    so_ref[...] = sn
    o_ref[...] = jnp.sum(qcol * sn, axis=1, keepdims=True)


def gdn_decode_step(s0, q, k, v, beta, g):
    bsz, gh, dk, _ = s0.shape
    d = gh * dk
    rows = lambda x: x.reshape(bsz, gh, 1, dk)
    wide = lambda x: jnp.broadcast_to(x[:, :, None, None], (bsz, gh, 1, dk))
    vec = lambda: pl.BlockSpec((None, gh, 1, dk), lambda bi: (bi, 0, 0, 0))
    st = lambda: pl.BlockSpec((None, gh, dk, dk), lambda bi: (bi, 0, 0, 0))
    sn, o = pl.pallas_call(
        _gds_kernel,
        grid=(bsz,),
        in_specs=[st()] + [vec() for _ in range(5)],
        out_specs=[st(), vec()],
        out_shape=[jax.ShapeDtypeStruct(s0.shape, F32), jax.ShapeDtypeStruct((bsz, gh, 1, dk), F32)],
        compiler_params=_cparams("arbitrary"),
        name="gdn_decode",
    )(s0, rows(q), rows(k), rows(v), wide(beta), wide(jnp.exp(g)))
    return o.reshape(bsz, d), sn


def _rwkv_layer(x, modp, mods, nw, j, P, bp, t, shift_prev, s0):
    m, d = x.shape
    mp = bp * t
    bs = m - mp
    nh = d // RWKV_N
    xs, hlast, h_s = pre_rwkv(x, modp[:, 0:2], mods[1], mods[0], nw, P['rw_mix'][j], shift_prev, t)
    tn = min(d, 512)
    mm = lambda a, w, widx, n, tnn, **kw: pmatmul(a, w, widx, n_out=n, tn=tnn, **kw)
    r = mm(xs[0], P['rw_w_rkv'], (j, 0), d, tn)
    k = mm(xs[1], P['rw_w_rkv'], (j, 1), d, tn)
    v = mm(xs[2], P['rw_w_rkv'], (j, 2), d, tn)
    lora = P['rw_w1'].shape[-1]
    wl = mm(mm(xs[3], P['rw_w1'], (j,), lora, lora, act="tanh", out_dtype=BF16), P['rw_w2'], (j,), d, tn)
    al = mm(mm(xs[4], P['rw_a1'], (j,), lora, lora, out_dtype=BF16), P['rw_a2'], (j,), d, tn)
    gl = P['rw_g1'].shape[-1]
    glp = -(-gl // LANES) * LANES
    g1 = jnp.pad(P['rw_g1'][j], ((0, 0), (0, glp - gl)))
    g2 = jnp.pad(P['rw_g2'][j], ((0, glp - gl), (0, 0)))
    g = mm(mm(xs[5], g1, (), glp, glp, act="sigmoid", out_dtype=BF16), g2, (), d, tn)
    prm = [P[n][j] for n in ('rw_w0', 'rw_a0', 'rw_k_k', 'rw_k_a')] + [P['rw_r_k'][j].reshape(d)]
    ln_w, ln_b = P['rw_ln_w'][j], P['rw_ln_b'][j]
    yg_p, sp = rwkv_prompt_scan(r, k, v, wl, al, g, *prm, ln_w, ln_b, bp, t)

    rs, ks, vs, wls, als, gs = (a[mp:] for a in (r, k, v, wl, al, g))
    w_log = -jax.nn.softplus(-(prm[0] + wls)) - 0.5
    dec = jnp.exp(-jnp.exp(w_log))
    a = jax.nn.sigmoid(prm[1] + als)
    hv = lambda z: z.reshape(bs, nh, RWKV_N)
    kx = hv(ks * prm[2])
    kap = kx * lax.rsqrt(jnp.sum(kx * kx, axis=-1, keepdims=True) + 1e-6)
    k_mod = ks * (1.0 + (a - 1.0) * prm[3])
    ys, ss = rwkv_decode_step(s0, hv(dec), kap, kap * hv(a), hv(k_mod), hv(vs), hv(rs))
    mu = jnp.mean(ys, axis=-1, keepdims=True)
    var = jnp.mean(jnp.square(ys - mu), axis=-1, keepdims=True)
    yn = ((ys - mu) * lax.rsqrt(var + RWKV_LN_EPS)).reshape(bs, d) * ln_w + ln_b
    bonus = jnp.sum(hv(rs) * hv(k_mod) * prm[4].reshape(nh, RWKV_N), axis=-1, keepdims=True) * hv(vs)
    yg_s = ((yn + bonus.reshape(bs, d)) * gs).astype(BF16)

    out = mm(jnp.concatenate([yg_p, yg_s], axis=0), P['rw_w_o'], (j,), d, tn)
    return out, (sp, ss), (hlast[:, 0], h_s)


def _gdn_layer(h, j, P, bp, t, conv_prev, s0):
    m, d = h.shape
    mp = bp * t
    bs = m - mp
    gh = d // GDN_DK
    cdim = 3 * d
    tn = min(d, 512)
    w_in = P['gd_w_in']
    main = pmatmul(h, w_in, (j,), n_out=4 * d, tn=tn)
    w_tail = jnp.pad(w_in[j][:, 4 * d:], ((0, 0), (0, LANES - 2 * gh)))
    tail = pmatmul(h, w_tail, (), n_out=LANES, tn=LANES)
    qkv = gdn_conv(main, P['gd_conv_w'][j], jnp.swapaxes(conv_prev, 0, 1), bp, t)
    beta = jax.nn.sigmoid(tail[:, :gh])
    g = -jnp.exp(P['gd_A_log'][j]) * jax.nn.softplus(tail[:, gh:2 * gh] + P['gd_dt_bias'][j])
    norm_w = P['gd_norm_w'][j]
    og_p, sp = gdn_prompt_scan(qkv, main, g[:mp].reshape(bp, t, gh), beta[:mp].reshape(bp, t, gh), norm_w, bp, t)

    o_s, ss = gdn_decode_step(s0, qkv[0, mp:], qkv[1, mp:], qkv[2, mp:], beta[mp:], g[mp:])
    oh = o_s.reshape(bs, gh, GDN_DK)
    oh = oh * lax.rsqrt(jnp.mean(oh * oh, axis=-1, keepdims=True) + RMS_EPS) * norm_w
    og_s = (oh.reshape(bs, d) * jax.nn.silu(main[mp:, cdim:])).astype(BF16)

    out = pmatmul(jnp.concatenate([og_p, og_s], axis=0), P['gd_w_out'], (j,), n_out=d, tn=tn)
    conv_p = jnp.stack([main[(b + 1) * t - (GDN_CONV - 1):(b + 1) * t, :cdim] for b in range(bp)])
    conv_s = jnp.concatenate([conv_prev[:, 1:], main[mp:, None, :cdim]], axis=1)
    return out, (sp, ss), (conv_p, conv_s)


def kernel(x_prompt, x_sample, state_rwkv, state_rwkv_shift, state_gdn, state_gdn_conv, c_prompt, c_sample,
           w_ada, b_ada, norm1_w, norm2_w, rw_mix, rw_w_rkv, rw_w0, rw_w1, rw_w2, rw_a0, rw_a1, rw_a2,
           rw_g1, rw_g2, rw_k_k, rw_k_a, rw_r_k, rw_ln_w, rw_ln_b, rw_w_o, gd_w_in, gd_conv_w, gd_A_log,
           gd_dt_bias, gd_norm_w, gd_w_out, ffn_w_gate, ffn_w_up, ffn_w_down, final_norm_w):
    P = {
        'rw_mix': rw_mix, 'rw_w_rkv': rw_w_rkv, 'rw_w0': rw_w0, 'rw_w1': rw_w1, 'rw_w2': rw_w2,
        'rw_a0': rw_a0, 'rw_a1': rw_a1, 'rw_a2': rw_a2, 'rw_g1': rw_g1, 'rw_g2': rw_g2,
        'rw_k_k': rw_k_k, 'rw_k_a': rw_k_a, 'rw_r_k': rw_r_k, 'rw_ln_w': rw_ln_w, 'rw_ln_b': rw_ln_b,
        'rw_w_o': rw_w_o, 'gd_w_in': gd_w_in, 'gd_conv_w': gd_conv_w, 'gd_A_log': gd_A_log,
        'gd_dt_bias': gd_dt_bias, 'gd_norm_w': gd_norm_w, 'gd_w_out': gd_w_out,
    }
    bp, t, d = x_prompt.shape
    bs = x_sample.shape[0]
    depth = w_ada.shape[0]
    dff = ffn_w_gate.shape[-1]
    mp = bp * t
    assert x_sample.shape[1] == 1 and t % CHUNK == 0 and t % bs == 0 and bs % 16 == 0 and d % (2 * LANES) == 0

    x = jnp.concatenate([x_prompt.reshape(mp, d), x_sample.reshape(bs, d)], axis=0)
    c_all = jnp.concatenate([c_prompt, c_sample], axis=0)
    nb = bp + bs
    nbp = -(-nb // 16) * 16
    c_act = jnp.pad(jax.nn.silu(c_all), ((0, nbp - nb), (0, 0))).astype(BF16)
    tn_d = min(d, 512)
    tn_f = _pick_tile(dff, 256, LANES)
    tn_o = min(d, 512)
    modp, mods = [], []
    for layer in range(depth):
        mod = (pmatmul(c_act, w_ada, (layer,), n_out=6 * d, tn=tn_d)[:nb] + b_ada[layer]).reshape(nb, 6, d)
        modp.append(mod[:bp])
        mods.append(jnp.swapaxes(mod[bp:], 0, 1))

    new_rw_s, new_rw_shift, new_gd_s, new_gd_conv = [], [], [], []
    h = None
    for layer in range(depth):
        j = layer // 2
        mpl, msl = modp[layer], mods[layer]
        if layer % 2 == 0:
            out, s_new, sh_new = _rwkv_layer(x, mpl, msl, norm1_w[layer], j, P, bp, t,
                                             state_rwkv_shift[j], state_rwkv[j])
            new_rw_s.append(s_new)
            new_rw_shift.append(sh_new)
        else:
            if h is None:
                _, h = resnorm(x, jnp.zeros_like(x), jnp.stack([mpl[:, 2], mpl[:, 1], mpl[:, 0]], axis=1),
                               msl[2], msl[1], msl[0], norm1_w[layer], t)
            out, s_new, cb_new = _gdn_layer(h, j, P, bp, t, state_gdn_conv[j], state_gdn[j])
            new_gd_s.append(s_new)
            new_gd_conv.append(cb_new)
        x, h = resnorm(x, out, jnp.stack([mpl[:, 2], mpl[:, 4], mpl[:, 3]], axis=1),
                       msl[2], msl[4], msl[3], norm2_w[layer], t)
        act = pmatmul(h, ffn_w_gate, (layer,), n_out=dff, tn=tn_f, w2=ffn_w_up, out_dtype=BF16)
        kh = dff // 2
        part = pmatmul(act, ffn_w_down, (layer,), n_out=d, tn=tn_o, k0=0, kk=kh, single_buffer_w=True)
        ffn = pmatmul(act, ffn_w_down, (layer,), n_out=d, tn=tn_o, k0=kh, kk=kh, add=part, single_buffer_w=True)
        if layer + 1 < depth:
            nxt_p, nxt_s = modp[layer + 1], mods[layer + 1]
            x, h = resnorm(x, ffn, jnp.stack([mpl[:, 5], nxt_p[:, 1], nxt_p[:, 0]], axis=1),
                           msl[5], nxt_s[1], nxt_s[0], norm1_w[layer + 1], t, emit_h=(layer + 1) % 2 == 1)
        else:
            y_p, y_s = final_norm(x, ffn, mpl[:, 5:6], msl[5], final_norm_w, t)
    grp = lambda lst, i: jnp.stack([e[i] for e in lst])
    return (y_p.reshape(bp, t, d), y_s.reshape(bs, 1, d),
            grp(new_rw_s, 0), grp(new_rw_shift, 0), grp(new_gd_s, 0), grp(new_gd_conv, 0),
            grp(new_rw_s, 1), grp(new_rw_shift, 1), grp(new_gd_s, 1), grp(new_gd_conv, 1))
```

```python
import functools

import jax
import jax.numpy as jnp
from jax import lax
from jax.experimental import pallas as pl
from jax.experimental.pallas import tpu as pltpu

F32, BF16 = jnp.float32, jnp.bfloat16

RMS_EPS = 1e-6
RWKV_LN_EPS = 64e-5
RWKV_N = 64
GDN_DK = 128
GDN_CONV = 4
CHUNK = 64
LANES = 128
SUBLANES = 8
GROUP = 32
VMEM_LIMIT = 56 * 2**20


def _cparams(*sem):
    return pltpu.CompilerParams(dimension_semantics=sem, vmem_limit_bytes=VMEM_LIMIT)


def _pick_tile(n, cap, mult):
    best = None
    for t in range(mult, min(n, cap) + 1, mult):
        if n % t == 0:
            best = t
    return best or n


def _group(n):
    g = GROUP
    while n % g:
        g //= 2
    return g


def _dot(a, b):
    return jnp.dot(a.astype(BF16), b.astype(BF16), preferred_element_type=F32)


def _dot_nt(a, b):
    return lax.dot_general(a.astype(BF16), b.astype(BF16), (((1,), (1,)), ((), ())),
                           preferred_element_type=F32)


def _dot_tn(a, b):
    return lax.dot_general(a.astype(BF16), b.astype(BF16), (((0,), (0,)), ((), ())),
                           preferred_element_type=F32)


def _split3(x):
    hi = x.astype(BF16)
    r1 = x - hi.astype(F32)
    mid = r1.astype(BF16)
    lo = (r1 - mid.astype(F32)).astype(BF16)
    return hi, mid, lo


def _tri(n, transpose=False):
    row = lax.broadcasted_iota(jnp.int32, (n, n), 0)
    col = lax.broadcasted_iota(jnp.int32, (n, n), 1)
    m = (row <= col) if transpose else (row >= col)
    return jnp.where(m, 1.0, 0.0).astype(BF16)


def _row_to_col(row):
    n = row.shape[1]
    eye = lax.broadcasted_iota(jnp.int32, (n, n), 0) == lax.broadcasted_iota(jnp.int32, (n, n), 1)
    return jnp.sum(jnp.where(eye, jnp.broadcast_to(row, (n, n)), 0.0), axis=1, keepdims=True)


def _segsum(x, width, exact):
    l = x.shape[1]
    gi = lax.broadcasted_iota(jnp.int32, (LANES, LANES), 0) // width
    gj = lax.broadcasted_iota(jnp.int32, (LANES, LANES), 1) // width
    bd = jnp.where(gi == gj, 1.0, 0.0).astype(BF16)
    hi = x.astype(BF16)
    lo = (x - hi.astype(F32)).astype(BF16) if exact else None
    outs = []
    for t in range(l // LANES):
        sl = slice(t * LANES, (t + 1) * LANES)
        s = jnp.dot(hi[:, sl], bd, preferred_element_type=F32)
        if exact:
            s = s + jnp.dot(lo[:, sl], bd, preferred_element_type=F32)
        outs.append(s)
    return outs[0] if len(outs) == 1 else jnp.concatenate(outs, axis=1)


def _modnorm(x, nw, scale, shift):
    y = x * lax.rsqrt(jnp.mean(x * x, axis=-1, keepdims=True) + RMS_EPS) * nw
    return y * (1.0 + scale) + shift


A_SLOTS = 3


def _mm_kernel(*refs, n_w, has_add, act, k0, ni, nsteps):
    a_hbm = refs[0]
    w_refs = refs[1:1 + n_w]
    pos = 1 + n_w
    add_ref = refs[pos] if has_add else None
    pos += int(has_add)
    o_ref = refs[pos]
    wbf = refs[pos + 1:pos + 1 + n_w]
    a_buf, a_sem = refs[pos + 1 + n_w:]
    tm, kk = a_buf.shape[1:]
    step = pl.program_id(0) * ni + pl.program_id(1)

    def a_copy(s):
        row = pl.multiple_of((s % ni) * tm, 16)
        slot = s % A_SLOTS
        return pltpu.make_async_copy(a_hbm.at[pl.ds(row, tm), pl.ds(k0, kk)], a_buf.at[slot], a_sem.at[slot])

    if ni == 1:
        @pl.when(step == 0)
        def _():
            a_copy(0).start()
            a_copy(0).wait()
    else:
        @pl.when(step == 0)
        def _():
            for s in range(min(A_SLOTS - 1, nsteps)):
                a_copy(s).start()

        @pl.when(step + (A_SLOTS - 1) < nsteps)
        def _():
            a_copy(step + (A_SLOTS - 1)).start()

    @pl.when(pl.program_id(1) == 0)
    def _():
        for w, s in zip(w_refs, wbf):
            s[...] = w[...].astype(BF16)

    if ni == 1:
        a = a_buf[0]
    else:
        a_copy(step).wait()
        a = a_buf[step % A_SLOTS]
    y = jnp.dot(a, wbf[0][...], preferred_element_type=F32)
    if n_w == 2:
        u = jnp.dot(a, wbf[1][...], preferred_element_type=F32)
        y = y * jax.nn.sigmoid(y) * u
    if has_add:
        y = y + add_ref[...]
    if act == "tanh":
        y = jnp.tanh(y)
    elif act == "sigmoid":
        y = jax.nn.sigmoid(y)
    o_ref[...] = y.astype(o_ref.dtype)


def pmatmul(a, w, widx=(), *, n_out, tn, k0=0, kk=None, w2=None, add=None, act=None,
            out_dtype=F32, tm_cap=832, single_buffer_w=False):
    m = a.shape[0]
    kk = a.shape[1] if kk is None else kk
    kb = k0 // kk
    if kk <= 512:
        tm_cap = 5 * tm_cap // 2
    tm = _pick_tile(m, tm_cap, 16)
    ws = [w] if w2 is None else [w, w2]
    nlead = len(widx)
    grid = (n_out // tn, m // tm)
    in_specs = [pl.BlockSpec(memory_space=pl.ANY)]
    wmode = dict(pipeline_mode=pl.Buffered(1)) if single_buffer_w else {}
    for _ in ws:
        in_specs.append(pl.BlockSpec((None,) * nlead + (kk, tn),
                                     lambda j, i: tuple(widx) + (kb, j), **wmode))
    args = [a] + ws
    if add is not None:
        in_specs.append(pl.BlockSpec((tm, tn), lambda j, i: (i, j)))
        args.append(add)
    return pl.pallas_call(
        functools.partial(_mm_kernel, n_w=len(ws), has_add=add is not None, act=act, k0=k0,
                          ni=grid[1], nsteps=grid[0] * grid[1]),
        grid=grid,
        in_specs=in_specs,
        out_specs=pl.BlockSpec((tm, tn), lambda j, i: (i, j)),
        out_shape=jax.ShapeDtypeStruct((m, n_out), out_dtype),
        scratch_shapes=[pltpu.VMEM((kk, tn), BF16) for _ in ws] + [
            pltpu.VMEM((A_SLOTS, tm, kk), BF16), pltpu.SemaphoreType.DMA((A_SLOTS,))],
        compiler_params=_cparams("arbitrary", "arbitrary"),
        name=f"mm_m{m}_k{kk}_n{n_out}" + ("_glu" if w2 is not None else "") + ("_add" if add is not None else ""),
    )(*args)


def _tile_specs(tr, d, np_tiles, tps, nbp):
    tok = lambda: pl.BlockSpec((tr, d), lambda i: (i, 0))
    seq = lambda n: pl.BlockSpec((None, n, d), lambda i: (jnp.minimum(i // tps, nbp - 1), 0, 0))
    samp = lambda: pl.BlockSpec((tr, d), lambda i: (0, 0))
    row = lambda n=1: pl.BlockSpec((n, d), lambda i: (0, 0))
    return tok, seq, samp, row


def _pre_rwkv_kernel(x_ref, mp_ref, scs_ref, shs_ref, nw_ref, mix_ref, prev_s_ref,
                     o0, o1, o2, o3, o4, o5, hlast_ref, hs_ref, carry_ref, *, np_tiles, tps):
    i = pl.program_id(0)
    is_s = i == np_tiles
    shift = jnp.where(is_s, shs_ref[...], mp_ref[0:1, :])
    scale = jnp.where(is_s, scs_ref[...], mp_ref[1:2, :])
    h = _modnorm(x_ref[...], nw_ref[...], scale, shift)
    tr = h.shape[0]

    @pl.when(i % tps == 0)
    def _():
        carry_ref[...] = jnp.zeros_like(carry_ref)

    rowi = lax.broadcasted_iota(jnp.int32, h.shape, 0)
    prev_p = jnp.where(rowi == 0, carry_ref[...], pltpu.roll(h, 1, axis=0))
    prev = jnp.where(is_s, prev_s_ref[...], prev_p)
    xx = prev - h
    for n, o in enumerate((o0, o1, o2, o3, o4, o5)):
        o[...] = (h + xx * mix_ref[n:n + 1, :]).astype(BF16)
    last = h[tr - 1:tr, :]
    carry_ref[...] = last

    @pl.when(jnp.logical_not(is_s))
    def _():
        hlast_ref[...] = last

    @pl.when(is_s)
    def _():
        hs_ref[...] = h


def pre_rwkv(x, modp, scale_s, shift_s, nw, mix, prev_s, t):
    m, d = x.shape
    tr = scale_s.shape[0]
    nbp = modp.shape[0]
    tps = t // tr
    np_tiles = nbp * tps
    tok, seq, samp, row = _tile_specs(tr, d, np_tiles, tps, nbp)
    outs = pl.pallas_call(
        functools.partial(_pre_rwkv_kernel, np_tiles=np_tiles, tps=tps),
        grid=(np_tiles + 1,),
        in_specs=[tok(), seq(2), samp(), samp(), row(), row(6), samp()],
        out_specs=[tok() for _ in range(6)] + [seq(1), samp()],
        out_shape=[jax.ShapeDtypeStruct((m, d), BF16)] * 6 + [jax.ShapeDtypeStruct((nbp, 1, d), F32),
                                                              jax.ShapeDtypeStruct((tr, d), F32)],
        scratch_shapes=[pltpu.VMEM((1, d), F32)],
        compiler_params=_cparams("arbitrary"),
        name="pre_rwkv",
    )(x, modp, scale_s, shift_s, nw.reshape(1, d), mix, prev_s)
    return outs[:6], outs[6], outs[7]


def _resnorm_kernel(x_ref, dl_ref, mp_ref, gs_ref, scs_ref, shs_ref, nw_ref, xo_ref, *h_refs, np_tiles):
    is_s = pl.program_id(0) == np_tiles
    gate = jnp.where(is_s, gs_ref[...], mp_ref[0:1, :])
    xn = x_ref[...] + gate * dl_ref[...]
    xo_ref[...] = xn
    if h_refs:
        scale = jnp.where(is_s, scs_ref[...], mp_ref[1:2, :])
        shift = jnp.where(is_s, shs_ref[...], mp_ref[2:3, :])
        h_refs[0][...] = _modnorm(xn, nw_ref[...], scale, shift).astype(BF16)


def resnorm(x, delta, modp, gate_s, scale_s, shift_s, nw, t, emit_h=True):
    m, d = x.shape
    tr = gate_s.shape[0]
    nbp = modp.shape[0]
    tps = t // tr
    np_tiles = nbp * tps
    tok, seq, samp, row = _tile_specs(tr, d, np_tiles, tps, nbp)
    outs = pl.pallas_call(
        functools.partial(_resnorm_kernel, np_tiles=np_tiles),
        grid=(np_tiles + 1,),
        in_specs=[tok(), tok(), seq(3), samp(), samp(), samp(), row()],
        out_specs=[tok()] + ([tok()] if emit_h else []),
        out_shape=[jax.ShapeDtypeStruct((m, d), F32)] + ([jax.ShapeDtypeStruct((m, d), BF16)] if emit_h else []),
        compiler_params=_cparams("arbitrary"),
        name="resnorm",
    )(x, delta, modp, gate_s, scale_s, shift_s, nw.reshape(1, d))
    return (outs[0], outs[1]) if emit_h else (outs[0], None)


def _final_kernel(x_ref, dl_ref, mp_ref, gs_ref, nw_ref, yp_ref, ys_ref, *, np_tiles):
    is_s = pl.program_id(0) == np_tiles
    gate = jnp.where(is_s, gs_ref[...], mp_ref[0:1, :])
    xn = x_ref[...] + gate * dl_ref[...]
    y = xn * lax.rsqrt(jnp.mean(xn * xn, axis=-1, keepdims=True) + RMS_EPS) * nw_ref[...]

    @pl.when(jnp.logical_not(is_s))
    def _():
        yp_ref[...] = y

    @pl.when(is_s)
    def _():
        ys_ref[...] = y


def final_norm(x, delta, modp, gate_s, nw, t):
    m, d = x.shape
    tr = gate_s.shape[0]
    nbp = modp.shape[0]
    tps = t // tr
    np_tiles = nbp * tps
    tok, seq, samp, row = _tile_specs(tr, d, np_tiles, tps, nbp)
    return pl.pallas_call(
        functools.partial(_final_kernel, np_tiles=np_tiles),
        grid=(np_tiles + 1,),
        in_specs=[tok(), tok(), seq(1), samp(), row()],
        out_specs=[pl.BlockSpec((tr, d), lambda i: (jnp.minimum(i, np_tiles - 1), 0)), samp()],
        out_shape=[jax.ShapeDtypeStruct((np_tiles * tr, d), F32), jax.ShapeDtypeStruct((tr, d), F32)],
        compiler_params=_cparams("arbitrary"),
        name="final_norm",
    )(x, delta, modp, gate_s, nw.reshape(1, d))


def _gdn_conv_kernel(x_ref, w_ref, buf_ref, o_ref, prev_ref, *, np_tiles, tps):
    sec = pl.program_id(0)
    i = pl.program_id(1)
    is_s = i == np_tiles

    x = x_ref[...]

    @pl.when(i % tps == 0)
    def _():
        prev_ref[...] = jnp.zeros_like(prev_ref)

    def tap(k):
        return w_ref[GDN_CONV - 1 - k:GDN_CONV - k, :]

    def finish(y):
        a = y * jax.nn.sigmoid(y)
        qscale = jnp.where(sec == 0, GDN_DK ** -0.5, 1.0)
        nrm = a * (lax.rsqrt(_segsum(a * a, GDN_DK, exact=False) + 1e-6) * qscale)
        o_ref[...] = jnp.where(sec == 2, a, nrm)

    @pl.when(is_s)
    def _():
        y = x * tap(0)
        for k in range(1, GDN_CONV):
            y = y + buf_ref[GDN_CONV - 1 - k] * tap(k)
        finish(y)

    @pl.when(jnp.logical_not(is_s))
    def _():
        prev = prev_ref[...]
        rowi = lax.broadcasted_iota(jnp.int32, x.shape, 0)
        y = x * tap(0)
        for k in range(1, GDN_CONV):
            back = jnp.where(rowi >= k, pltpu.roll(x, k, axis=0), pltpu.roll(prev, k, axis=0))
            y = y + back * tap(k)
        prev_ref[...] = x
        finish(y)


def gdn_conv(main, conv_w, bufs, nbp, t):
    m = main.shape[0]
    d = main.shape[1] // 4
    tr = bufs.shape[1]
    tps = t // tr
    np_tiles = nbp * tps
    return pl.pallas_call(
        functools.partial(_gdn_conv_kernel, np_tiles=np_tiles, tps=tps),
        grid=(3, np_tiles + 1),
        in_specs=[pl.BlockSpec((tr, d), lambda s, i: (i, s)),
                  pl.BlockSpec((GDN_CONV, d), lambda s, i: (0, s)),
                  pl.BlockSpec((GDN_CONV - 1, tr, d), lambda s, i: (0, 0, s))],
        out_specs=pl.BlockSpec((None, tr, d), lambda s, i: (s, i, 0)),
        out_shape=jax.ShapeDtypeStruct((3, m, d), F32),
        scratch_shapes=[pltpu.VMEM((tr, d), F32)],
        compiler_params=_cparams("arbitrary", "arbitrary"),
        name="gdn_conv",
    )(main, conv_w, bufs)


def _inv_kernel(m_ref, o_ref, n_ref, *, c):
    nblk = c // SUBLANES
    zeros = jnp.zeros((SUBLANES * c, LANES), F32)
    for ib in range(nblk):
        n_ref[ib * SUBLANES * c:(ib + 1) * SUBLANES * c, :] = zeros

        def row(ii, carry, ib=ib):
            base = pl.multiple_of((ib * SUBLANES + ii) * c, c)
            acc = [m_ref[pl.ds(base + SUBLANES * k, SUBLANES), :] for k in range(ib + 1)]
            for j in range((ib + 1) * SUBLANES):
                coef = m_ref[pl.ds(base + j, 1), :]
                for k in range(j // SUBLANES + 1):
                    acc[k] = acc[k] + coef * n_ref[j * c + SUBLANES * k:j * c + SUBLANES * (k + 1), :]
            for k in range(ib + 1):
                n_ref[pl.ds(base + SUBLANES * k, SUBLANES), :] = -acc[k]
            return carry

        lax.fori_loop(0, SUBLANES, row, 0)
    o_ref[...] = n_ref[...].astype(BF16)


def tri_inverse(mats, pack_pairs=False):
    shape = mats.shape
    c = shape[-1]
    u = 1
    for s in shape[:-2]:
        u *= s
    up = -(-u // LANES) * LANES
    flat = jnp.transpose(mats.reshape(u, c, c), (1, 2, 0)).reshape(c * c, u)
    if up != u:
        flat = jnp.pad(flat, ((0, 0), (0, up - u)))
    out = pl.pallas_call(
        functools.partial(_inv_kernel, c=c),
        grid=(up // LANES,),
        in_specs=[pl.BlockSpec((c * c, LANES), lambda g: (0, g))],
        out_specs=pl.BlockSpec((c * c, LANES), lambda g: (0, g)),
        out_shape=jax.ShapeDtypeStruct((c * c, up), BF16),
        scratch_shapes=[pltpu.VMEM((c * c, LANES), F32)],
        compiler_params=_cparams("arbitrary"),
        name="tri_inverse",
    )(flat)
    out = out[:, :u]
    if pack_pairs:
        packed = jnp.transpose(out.reshape(c, c, u // 2, 2), (2, 0, 3, 1))
        return packed.reshape(shape[:-3] + (shape[-3] // 2, c, 2 * c))
    return jnp.transpose(out.reshape(c, c, u), (2, 0, 1)).reshape(shape)


def _rwkv_token_prep(r, k, wl, al, w0, a0, k_k, k_a, r_k):
    w_log = -jax.nn.softplus(-(w0 + wl)) - 0.5
    lw = -jnp.exp(w_log)
    a = jax.nn.sigmoid(a0 + al)
    kx = k * k_k
    kap = kx * lax.rsqrt(_segsum(kx * kx, RWKV_N, exact=False) + 1e-6)
    k_mod = k * (1.0 + (a - 1.0) * k_a)
    rk = _segsum(r * k_mod * r_k, RWKV_N, exact=True)
    return lw, kap, k_mod, kap * a, rk


def _rw1_kernel(r_ref, k_ref, v_ref, wl_ref, al_ref, w0_ref, a0_ref, kk_ref, ka_ref, rk_ref,
                kaph_ref, rh_ref, kh_ref, bh_ref, khp_ref, bhp_ref, vb_ref, bonus_ref, pc_ref, mb_ref):
    v = v_ref[...]
    lw, kap, k, b, rk = _rwkv_token_prep(r_ref[...], k_ref[...], wl_ref[...], al_ref[...], w0_ref[...],
                                         a0_ref[...], kk_ref[...], ka_ref[...], rk_ref[...])
    bonus_ref[...] = rk * v
    vb_ref[...] = v.astype(BF16)
    c, lb = lw.shape
    tri = _tri(c)
    hi, mid, lo = _split3(lw)
    p = (jnp.dot(tri, hi, preferred_element_type=F32) + jnp.dot(tri, mid, preferred_element_type=F32)
         + jnp.dot(tri, lo, preferred_element_type=F32))
    pc = p[c - 1:c, :]
    en = jnp.exp(-p)
    ec = jnp.exp(pc - p)
    kaph = kap * jnp.exp(p - lw)
    bh = (b * en).astype(BF16)
    kaph_ref[...] = kaph.astype(BF16)
    rh_ref[...] = (r_ref[...] * jnp.exp(p)).astype(BF16)
    kh_ref[...] = (k * en).astype(BF16)
    bh_ref[...] = bh
    khp_ref[...] = (k * ec).astype(BF16)
    bhp_ref[...] = (b * ec).astype(BF16)
    pc_ref[...] = jnp.exp(pc)
    lane = lax.broadcasted_iota(jnp.int32, (c, LANES), 1)
    m0 = lane < RWKV_N
    rowi = lax.broadcasted_iota(jnp.int32, (2 * c, c), 0) % c
    coli = lax.broadcasted_iota(jnp.int32, (2 * c, c), 1)
    strict = rowi > coli
    for jt in range(lb // LANES):
        kp = kaph[:, jt * LANES:(jt + 1) * LANES]
        bp = bh[:, jt * LANES:(jt + 1) * LANES]
        lhs = jnp.concatenate([jnp.where(m0, kp, 0.0), jnp.where(m0, 0.0, kp)], axis=0)
        g = _dot_nt(lhs, bp)
        g = jnp.where(strict, g, 0.0)
        mb_ref[2 * jt] = g[:c]
        mb_ref[2 * jt + 1] = g[c:]


def _rw3_kernel(kaph_ref, rh_ref, kh_ref, bh_ref, khp_ref, bhp_ref, v_ref, pc_ref, n_ref,
                bonus_ref, g_ref, lnw_ref, lnb_ref, yg_ref, sfin_ref, a_ref):
    ci = pl.program_id(1)
    c = kaph_ref.shape[0]
    npair = a_ref.shape[0]

    @pl.when(ci == 0)
    def _():
        a_ref[...] = jnp.zeros_like(a_ref)

    m0 = lax.broadcasted_iota(jnp.int32, (c, LANES), 1) < RWKV_N
    row = lax.broadcasted_iota(jnp.int32, (c, LANES), 0)
    src = lax.broadcasted_iota(jnp.int32, (c, LANES), 1) % c
    strict = row > src
    incl = row >= src
    rr = lax.broadcasted_iota(jnp.int32, (LANES, LANES), 0)
    cc = lax.broadcasted_iota(jnp.int32, (LANES, LANES), 1)
    blockdiag = (rr < RWKV_N) == (cc < RWKV_N)
    ng = _group(npair)
    rng = range(ng)

    def by_head(t):
        t32 = t.astype(F32)
        return jnp.concatenate([jnp.where(m0, t32, 0.0), jnp.where(m0, 0.0, t32)], axis=0).astype(BF16)

    def group(gi, carry):
        js = [gi * ng + g for g in rng]
        lss = [pl.ds(pl.multiple_of(j * LANES, LANES), LANES) for j in js]
        lhs = [jnp.concatenate([kaph_ref[:, ls], rh_ref[:, ls]], axis=0) for ls in lss]
        vv = [v_ref[:, ls] for ls in lss]
        a0 = [a_ref[j] for j in js]
        x = [_dot(l, a) for l, a in zip(lhs, a0)]
        gk = [_dot_nt(lhs[g], by_head(kh_ref[:, lss[g]])) for g in rng]
        gb = [_dot_nt(lhs[g][c:], by_head(bh_ref[:, lss[g]])) for g in rng]
        vbd = [by_head(vv[g]) for g in rng]
        mk = [jnp.where(strict, gk[g][:c], 0.0) for g in rng]
        lrk = [jnp.where(incl, gk[g][c:], 0.0) for g in rng]
        lrb = [jnp.where(incl, gb[g], 0.0) for g in rng]
        mkv = [_dot(mk[g], vbd[g]) for g in rng]
        lrkv = [_dot(lrk[g], vbd[g]) for g in rng]
        kv = [_dot_tn(khp_ref[:, lss[g]], vv[g]) for g in rng]
        rhs = [x[g][:c] + mkv[g] for g in rng]
        nr = [_dot(n_ref[js[g]], by_head(rhs[g])) for g in rng]
        u = [rhs[g] + nr[g] for g in rng]
        lu = [_dot(lrb[g], by_head(u[g])) for g in rng]
        bu = [_dot_tn(bhp_ref[:, lss[g]], u[g]) for g in rng]
        y = [x[g][c:] + (lrkv[g] - lu[g]) for g in rng]
        mu = [_segsum(y[g], RWKV_N, exact=True) * (1.0 / RWKV_N) for g in range(ng)]
        yc = [y[g] - mu[g] for g in range(ng)]
        var = [_segsum(yc[g] * yc[g], RWKV_N, exact=False) * (1.0 / RWKV_N) for g in range(ng)]
        for g in range(ng):
            ls = lss[g]
            yn = yc[g] * lax.rsqrt(var[g] + RWKV_LN_EPS) * lnw_ref[:, ls] + lnb_ref[:, ls]
            yg_ref[:, ls] = ((yn + bonus_ref[:, ls]) * g_ref[:, ls]).astype(BF16)
            pcc = _row_to_col(pc_ref[:, ls])
            a_ref[js[g]] = pcc * a0[g] + jnp.where(blockdiag, kv[g] - bu[g], 0.0)
        return carry

    lax.fori_loop(0, npair // ng, group, 0)

    @pl.when(ci == pl.num_programs(1) - 1)
    def _():
        sfin_ref[...] = a_ref[...]


def rwkv_prompt_scan(r, k, v, wl, al, g, w0, a0, k_k, k_a, r_k, ln_w, ln_b, bsz, t):
    d = r.shape[1]
    c = CHUNK
    nc = t // c
    h = d // RWKV_N
    lb = min(d, 4096)
    row2 = lambda x: x.reshape(1, d)
    tok2 = lambda: pl.BlockSpec((c, lb), lambda bi, ci, li: (bi * nc + ci, li))
    prm = lambda: pl.BlockSpec((1, lb), lambda bi, ci, li: (0, li))
    tok = lambda: pl.BlockSpec((None, c, lb), lambda bi, ci, li: (bi, ci, li))
    bf = jax.ShapeDtypeStruct((bsz, t, d), BF16)
    kaph, rh, kh, bh, khp, bhp, vb, bonus, pc, mb = pl.pallas_call(
        _rw1_kernel,
        grid=(bsz, nc, d // lb),
        in_specs=[tok2() for _ in range(5)] + [prm() for _ in range(5)],
        out_specs=[tok() for _ in range(8)] + [
            pl.BlockSpec((None, None, 1, lb), lambda bi, ci, li: (bi, ci, 0, li)),
            pl.BlockSpec((None, None, lb // RWKV_N, c, c), lambda bi, ci, li: (bi, ci, li, 0, 0)),
        ],
        out_shape=[bf] * 7 + [jax.ShapeDtypeStruct((bsz, t, d), F32),
                              jax.ShapeDtypeStruct((bsz, nc, 1, d), F32),
                              jax.ShapeDtypeStruct((bsz, nc, h, c, c), F32)],
        compiler_params=_cparams("arbitrary", "arbitrary", "arbitrary"),
        name="rwkv_chunk_prep",
    )(r, k, v, wl, al, row2(w0), row2(a0), row2(k_k), row2(k_a), row2(r_k))
    assert 2 * c == LANES
    nmat = tri_inverse(mb, pack_pairs=True)
    tokd = lambda: pl.BlockSpec((None, c, d), lambda bi, ci: (bi, ci, 0))
    tok2d = lambda: pl.BlockSpec((c, d), lambda bi, ci: (bi * nc + ci, 0))
    prmd = lambda: pl.BlockSpec((1, d), lambda bi, ci: (0, 0))
    yg, sfin = pl.pallas_call(
        _rw3_kernel,
        grid=(bsz, nc),
        in_specs=[tokd() for _ in range(7)] + [
            pl.BlockSpec((None, None, 1, d), lambda bi, ci: (bi, ci, 0, 0)),
            pl.BlockSpec((None, None, h // 2, c, 2 * c), lambda bi, ci: (bi, ci, 0, 0, 0)),
            tokd(), tok2d(), prmd(), prmd(),
        ],
        out_specs=[tok2d(), pl.BlockSpec((None, h // 2, LANES, LANES), lambda bi, ci: (bi, 0, 0, 0))],
        out_shape=[jax.ShapeDtypeStruct((bsz * t, d), BF16),
                   jax.ShapeDtypeStruct((bsz, h // 2, LANES, LANES), F32)],
        scratch_shapes=[pltpu.VMEM((h // 2, LANES, LANES), F32)],
        compiler_params=_cparams("arbitrary", "arbitrary"),
        name="rwkv_chunk_scan",
    )(kaph, rh, kh, bh, khp, bhp, vb, pc, nmat, bonus, g, row2(ln_w), row2(ln_b))
    n = RWKV_N
    s_even = sfin[:, :, :n, :n]
    s_odd = sfin[:, :, n:, n:]
    s = jnp.stack([s_even, s_odd], axis=2).reshape(bsz, h, n, n)
    return yg, jnp.swapaxes(s, -1, -2)


def _rws_kernel(s_ref, w_ref, kap_ref, b_ref, k_ref, v_ref, r_ref, so_ref, y_ref):
    s = s_ref[...]
    n = s.shape[-1]
    eye = (lax.broadcasted_iota(jnp.int32, (n, n), 0) == lax.broadcasted_iota(jnp.int32, (n, n), 1))[None]
    sa = -jnp.sum(s * kap_ref[...], axis=2, keepdims=True)
    vcol = jnp.sum(jnp.where(eye, v_ref[...], 0.0), axis=2, keepdims=True)
    sn = s * w_ref[...] + sa * b_ref[...] + vcol * k_ref[...]
    so_ref[...] = sn
    ycol = jnp.sum(sn * r_ref[...], axis=2, keepdims=True)
    y_ref[...] = jnp.sum(jnp.where(eye, ycol, 0.0), axis=1, keepdims=True)


def rwkv_decode_step(s0, w, kap, b, k, v, r):
    bsz, h, n, _ = s0.shape
    vec = lambda: pl.BlockSpec((None, h, 1, n), lambda bi: (bi, 0, 0, 0))
    st = lambda: pl.BlockSpec((None, h, n, n), lambda bi: (bi, 0, 0, 0))
    rows = lambda x: x.reshape(bsz, h, 1, n)
    sn, y = pl.pallas_call(
        _rws_kernel,
        grid=(bsz,),
        in_specs=[st()] + [vec() for _ in range(6)],
        out_specs=[st(), vec()],
        out_shape=[jax.ShapeDtypeStruct(s0.shape, F32), jax.ShapeDtypeStruct((bsz, h, 1, n), F32)],
        compiler_params=_cparams("arbitrary"),
        name="rwkv_decode",
    )(s0, rows(w), rows(kap), rows(b), rows(k), rows(v), rows(r))
    return y.reshape(bsz, h, n), sn


def _gd_head_scalars(g_ref, gcr_s, tri_t):
    hi, mid, lo = _split3(g_ref[...])
    gcr_s[...] = (jnp.dot(hi, tri_t, preferred_element_type=F32)
                  + jnp.dot(mid, tri_t, preferred_element_type=F32)
                  + jnp.dot(lo, tri_t, preferred_element_type=F32))


def _gd1_kernel(k_ref, g_ref, beta_ref, l_ref, gcr_s):
    c = k_ref.shape[0]
    gh = g_ref.shape[0]
    _gd_head_scalars(g_ref, gcr_s, _tri(c, transpose=True))
    row = lax.broadcasted_iota(jnp.int32, (c, c), 0)
    col = lax.broadcasted_iota(jnp.int32, (c, c), 1)
    strict = row > col
    ng = _group(gh)

    def group(gi, carry):
        js = [gi * ng + g for g in range(ng)]
        gr = [gcr_s[pl.ds(j, 1), :] for j in js]
        gcol = [_row_to_col(x) for x in gr]
        bcol = [_row_to_col(beta_ref[pl.ds(j, 1), :]) for j in js]
        kk = [k_ref[:, pl.ds(pl.multiple_of(j * GDN_DK, GDN_DK), GDN_DK)] for j in js]
        gram = [_dot_nt(kk[g] * bcol[g], kk[g]) for g in range(ng)]
        for g in range(ng):
            dec = jnp.exp(jnp.where(strict, gcol[g] - gr[g], 0.0))
            l_ref[js[g]] = jnp.where(strict, gram[g] * dec, 0.0)
        return carry

    lax.fori_loop(0, gh // ng, group, 0)


def _gd3_kernel(q_ref, k_ref, v_ref, g_ref, beta_ref, n_ref, z_ref, nw_ref, og_ref, sfin_ref, s_ref, gcr_s):
    ci = pl.program_id(1)
    c = k_ref.shape[0]
    gh = g_ref.shape[0]

    @pl.when(ci == 0)
    def _():
        s_ref[...] = jnp.zeros_like(s_ref)

    _gd_head_scalars(g_ref, gcr_s, _tri(c, transpose=True))
    row = lax.broadcasted_iota(jnp.int32, (c, c), 0)
    col = lax.broadcasted_iota(jnp.int32, (c, c), 1)
    incl = row >= col
    ng = _group(gh)
    rng = range(ng)

    def group(gi, carry):
        js = [gi * ng + g for g in rng]
        lss = [pl.ds(pl.multiple_of(j * GDN_DK, GDN_DK), GDN_DK) for j in js]
        gr = [gcr_s[pl.ds(j, 1), :] for j in js]
        gcol = [_row_to_col(x) for x in gr]
        bcol = [_row_to_col(beta_ref[pl.ds(j, 1), :]) for j in js]
        glast = [x[:, c - 1:c] for x in gr]
        q = [q_ref[:, ls] for ls in lss]
        kk = [k_ref[:, ls] for ls in lss]
        vv = [v_ref[:, ls] for ls in lss]
        s = [s_ref[j] for j in js]
        eg = [jnp.exp(x) for x in gcol]
        rhs = [jnp.concatenate([vv[g] * bcol[g], kk[g] * bcol[g] * eg[g]], axis=1) for g in rng]
        qk = [_dot_nt(q[g], kk[g]) for g in rng]
        nr = [_dot(n_ref[js[g]], rhs[g]) for g in rng]
        qs = [_dot(q[g] * eg[g], s[g]) for g in rng]
        sol = [rhs[g] + nr[g] for g in rng]
        ws = [_dot(sol[g][:, GDN_DK:], s[g]) for g in rng]
        v_new = [sol[g][:, :GDN_DK] - ws[g] for g in rng]
        attn = [jnp.where(incl, qk[g] * jnp.exp(jnp.where(incl, gcol[g] - gr[g], 0.0)), 0.0) for g in rng]
        av = [_dot(attn[g], v_new[g]) for g in rng]
        kv = [_dot_tn(kk[g] * jnp.exp(glast[g] - gcol[g]), v_new[g]) for g in rng]
        for g in rng:
            o = qs[g] + av[g]
            on = o * lax.rsqrt(jnp.mean(o * o, axis=-1, keepdims=True) + RMS_EPS) * nw_ref[...]
            z = z_ref[:, lss[g]]
            og_ref[:, lss[g]] = (on * (z * jax.nn.sigmoid(z))).astype(BF16)
            s_ref[js[g]] = s[g] * jnp.exp(glast[g]) + kv[g]
        return carry

    lax.fori_loop(0, gh // ng, group, 0)

    @pl.when(ci == pl.num_programs(1) - 1)
    def _():
        sfin_ref[...] = s_ref[...]


def gdn_prompt_scan(qkv, main, g, beta, norm_w, bsz, t):
    d = qkv.shape[2]
    c = CHUNK
    nc = t // c
    gh = d // GDN_DK
    rows = lambda x: jnp.swapaxes(x.reshape(bsz, nc, c, gh), -1, -2)
    g_r, beta_r = rows(g), rows(beta)
    sect = lambda s: pl.BlockSpec((None, c, d), lambda bi, ci: (s, bi * nc + ci, 0))
    hrow = lambda: pl.BlockSpec((None, None, gh, c), lambda bi, ci: (bi, ci, 0, 0))
    mat = lambda: pl.BlockSpec((None, None, gh, c, c), lambda bi, ci: (bi, ci, 0, 0, 0))
    lmat = pl.pallas_call(
        _gd1_kernel,
        grid=(bsz, nc),
        in_specs=[sect(1), hrow(), hrow()],
        out_specs=mat(),
        out_shape=jax.ShapeDtypeStruct((bsz, nc, gh, c, c), F32),
        scratch_shapes=[pltpu.VMEM((gh, c), F32)],
        compiler_params=_cparams("arbitrary", "arbitrary"),
        name="gdn_chunk_prep",
    )(qkv, g_r, beta_r)
    nmat = tri_inverse(lmat)
    og, sfin = pl.pallas_call(
        _gd3_kernel,
        grid=(bsz, nc),
        in_specs=[sect(0), sect(1), sect(2), hrow(), hrow(), mat(),
                  pl.BlockSpec((c, d), lambda bi, ci: (bi * nc + ci, 3)),
                  pl.BlockSpec((1, GDN_DK), lambda bi, ci: (0, 0))],
        out_specs=[pl.BlockSpec((c, d), lambda bi, ci: (bi * nc + ci, 0)),
                   pl.BlockSpec((None, gh, GDN_DK, GDN_DK), lambda bi, ci: (bi, 0, 0, 0))],
        out_shape=[jax.ShapeDtypeStruct((bsz * t, d), BF16),
                   jax.ShapeDtypeStruct((bsz, gh, GDN_DK, GDN_DK), F32)],
        scratch_shapes=[pltpu.VMEM((gh, GDN_DK, GDN_DK), F32), pltpu.VMEM((gh, c), F32)],
        compiler_params=_cparams("arbitrary", "arbitrary"),
        name="gdn_chunk_scan",
    )(qkv, qkv, qkv, g_r, beta_r, nmat, main, norm_w.reshape(1, GDN_DK))
    return og, sfin


def _gds_kernel(s_ref, q_ref, k_ref, v_ref, beta_ref, eg_ref, so_ref, o_ref):
    s = s_ref[...]
    n = s.shape[-1]
    eye = (lax.broadcasted_iota(jnp.int32, (n, n), 0) == lax.broadcasted_iota(jnp.int32, (n, n), 1))[None]
    kcol = jnp.sum(jnp.where(eye, k_ref[...], 0.0), axis=2, keepdims=True)
    qcol = jnp.sum(jnp.where(eye, q_ref[...], 0.0), axis=2, keepdims=True)
    eg = eg_ref[...]
    ks = jnp.sum(kcol * s, axis=1, keepdims=True)
    v_new = beta_ref[...] * (v_ref[...] - eg * ks)
    sn = s * eg + kcol * v_new
    so_ref[...] = sn
    o_ref[...] = jnp.sum(qcol * sn, axis=1, keepdims=True)


def gdn_decode_step(s0, q, k, v, beta, g):
    bsz, gh, dk, _ = s0.shape
    d = gh * dk
    rows = lambda x: x.reshape(bsz, gh, 1, dk)
    wide = lambda x: jnp.broadcast_to(x[:, :, None, None], (bsz, gh, 1, dk))
    vec = lambda: pl.BlockSpec((None, gh, 1, dk), lambda bi: (bi, 0, 0, 0))
    st = lambda: pl.BlockSpec((None, gh, dk, dk), lambda bi: (bi, 0, 0, 0))
    sn, o = pl.pallas_call(
        _gds_kernel,
        grid=(bsz,),
        in_specs=[st()] + [vec() for _ in range(5)],
        out_specs=[st(), vec()],
        out_shape=[jax.ShapeDtypeStruct(s0.shape, F32), jax.ShapeDtypeStruct((bsz, gh, 1, dk), F32)],
        compiler_params=_cparams("arbitrary"),
        name="gdn_decode",
    )(s0, rows(q), rows(k), rows(v), wide(beta), wide(jnp.exp(g)))
    return o.reshape(bsz, d), sn


def _rwkv_layer(x, modp, mods, nw, j, P, bp, t, shift_prev, s0):
    m, d = x.shape
    mp = bp * t
    bs = m - mp
    nh = d // RWKV_N
    xs, hlast, h_s = pre_rwkv(x, modp[:, 0:2], mods[1], mods[0], nw, P['rw_mix'][j], shift_prev, t)
    tn = min(d, 512)
    mm = lambda a, w, widx, n, tnn, **kw: pmatmul(a, w, widx, n_out=n, tn=tnn, **kw)
    r = mm(xs[0], P['rw_w_rkv'], (j, 0), d, tn)
    k = mm(xs[1], P['rw_w_rkv'], (j, 1), d, tn)
    v = mm(xs[2], P['rw_w_rkv'], (j, 2), d, tn)
    lora = P['rw_w1'].shape[-1]
    wl = mm(mm(xs[3], P['rw_w1'], (j,), lora, lora, act="tanh", out_dtype=BF16), P['rw_w2'], (j,), d, tn)
    al = mm(mm(xs[4], P['rw_a1'], (j,), lora, lora, out_dtype=BF16), P['rw_a2'], (j,), d, tn)
    gl = P['rw_g1'].shape[-1]
    glp = -(-gl // LANES) * LANES
    g1 = jnp.pad(P['rw_g1'][j], ((0, 0), (0, glp - gl)))
    g2 = jnp.pad(P['rw_g2'][j], ((0, glp - gl), (0, 0)))
    g = mm(mm(xs[5], g1, (), glp, glp, act="sigmoid", out_dtype=BF16), g2, (), d, tn)
    prm = [P[n][j] for n in ('rw_w0', 'rw_a0', 'rw_k_k', 'rw_k_a')] + [P['rw_r_k'][j].reshape(d)]
    ln_w, ln_b = P['rw_ln_w'][j], P['rw_ln_b'][j]
    yg_p, sp = rwkv_prompt_scan(r, k, v, wl, al, g, *prm, ln_w, ln_b, bp, t)

    rs, ks, vs, wls, als, gs = (a[mp:] for a in (r, k, v, wl, al, g))
    w_log = -jax.nn.softplus(-(prm[0] + wls)) - 0.5
    dec = jnp.exp(-jnp.exp(w_log))
    a = jax.nn.sigmoid(prm[1] + als)
    hv = lambda z: z.reshape(bs, nh, RWKV_N)
    kx = hv(ks * prm[2])
    kap = kx * lax.rsqrt(jnp.sum(kx * kx, axis=-1, keepdims=True) + 1e-6)
    k_mod = ks * (1.0 + (a - 1.0) * prm[3])
    ys, ss = rwkv_decode_step(s0, hv(dec), kap, kap * hv(a), hv(k_mod), hv(vs), hv(rs))
    mu = jnp.mean(ys, axis=-1, keepdims=True)
    var = jnp.mean(jnp.square(ys - mu), axis=-1, keepdims=True)
    yn = ((ys - mu) * lax.rsqrt(var + RWKV_LN_EPS)).reshape(bs, d) * ln_w + ln_b
    bonus = jnp.sum(hv(rs) * hv(k_mod) * prm[4].reshape(nh, RWKV_N), axis=-1, keepdims=True) * hv(vs)
    yg_s = ((yn + bonus.reshape(bs, d)) * gs).astype(BF16)

    out = mm(jnp.concatenate([yg_p, yg_s], axis=0), P['rw_w_o'], (j,), d, tn)
    return out, (sp, ss), (hlast[:, 0], h_s)


def _gdn_layer(h, j, P, bp, t, conv_prev, s0):
    m, d = h.shape
    mp = bp * t
    bs = m - mp
    gh = d // GDN_DK
    cdim = 3 * d
    tn = min(d, 512)
    w_in = P['gd_w_in']
    main = pmatmul(h, w_in, (j,), n_out=4 * d, tn=tn)
    w_tail = jnp.pad(w_in[j][:, 4 * d:], ((0, 0), (0, LANES - 2 * gh)))
    tail = pmatmul(h, w_tail, (), n_out=LANES, tn=LANES)
    qkv = gdn_conv(main, P['gd_conv_w'][j], jnp.swapaxes(conv_prev, 0, 1), bp, t)
    beta = jax.nn.sigmoid(tail[:, :gh])
    g = -jnp.exp(P['gd_A_log'][j]) * jax.nn.softplus(tail[:, gh:2 * gh] + P['gd_dt_bias'][j])
    norm_w = P['gd_norm_w'][j]
    og_p, sp = gdn_prompt_scan(qkv, main, g[:mp].reshape(bp, t, gh), beta[:mp].reshape(bp, t, gh), norm_w, bp, t)

    o_s, ss = gdn_decode_step(s0, qkv[0, mp:], qkv[1, mp:], qkv[2, mp:], beta[mp:], g[mp:])
    oh = o_s.reshape(bs, gh, GDN_DK)
    oh = oh * lax.rsqrt(jnp.mean(oh * oh, axis=-1, keepdims=True) + RMS_EPS) * norm_w
    og_s = (oh.reshape(bs, d) * jax.nn.silu(main[mp:, cdim:])).astype(BF16)

    out = pmatmul(jnp.concatenate([og_p, og_s], axis=0), P['gd_w_out'], (j,), n_out=d, tn=tn)
    conv_p = jnp.stack([main[(b + 1) * t - (GDN_CONV - 1):(b + 1) * t, :cdim] for b in range(bp)])
    conv_s = jnp.concatenate([conv_prev[:, 1:], main[mp:, None, :cdim]], axis=1)
    return out, (sp, ss), (conv_p, conv_s)


def kernel(x_prompt, x_sample, state_rwkv, state_rwkv_shift, state_gdn, state_gdn_conv, c_prompt, c_sample,
           w_ada, b_ada, norm1_w, norm2_w, rw_mix, rw_w_rkv, rw_w0, rw_w1, rw_w2, rw_a0, rw_a1, rw_a2,
           rw_g1, rw_g2, rw_k_k, rw_k_a, rw_r_k, rw_ln_w, rw_ln_b, rw_w_o, gd_w_in, gd_conv_w, gd_A_log,
           gd_dt_bias, gd_norm_w, gd_w_out, ffn_w_gate, ffn_w_up, ffn_w_down, final_norm_w):
    P = {
        'rw_mix': rw_mix, 'rw_w_rkv': rw_w_rkv, 'rw_w0': rw_w0, 'rw_w1': rw_w1, 'rw_w2': rw_w2,
        'rw_a0': rw_a0, 'rw_a1': rw_a1, 'rw_a2': rw_a2, 'rw_g1': rw_g1, 'rw_g2': rw_g2,
        'rw_k_k': rw_k_k, 'rw_k_a': rw_k_a, 'rw_r_k': rw_r_k, 'rw_ln_w': rw_ln_w, 'rw_ln_b': rw_ln_b,
        'rw_w_o': rw_w_o, 'gd_w_in': gd_w_in, 'gd_conv_w': gd_conv_w, 'gd_A_log': gd_A_log,
        'gd_dt_bias': gd_dt_bias, 'gd_norm_w': gd_norm_w, 'gd_w_out': gd_w_out,
    }
    bp, t, d = x_prompt.shape
    bs = x_sample.shape[0]
    depth = w_ada.shape[0]
    dff = ffn_w_gate.shape[-1]
    mp = bp * t
    assert x_sample.shape[1] == 1 and t % CHUNK == 0 and t % bs == 0 and bs % 16 == 0 and d % (2 * LANES) == 0

    x = jnp.concatenate([x_prompt.reshape(mp, d), x_sample.reshape(bs, d)], axis=0)
    c_all = jnp.concatenate([c_prompt, c_sample], axis=0)
    nb = bp + bs
    nbp = -(-nb // 16) * 16
    c_act = jnp.pad(jax.nn.silu(c_all), ((0, nbp - nb), (0, 0))).astype(BF16)
    tn_d = min(d, 512)
    tn_f = _pick_tile(dff, 256, LANES)
    tn_o = min(d, 512)
    modp, mods = [], []
    for layer in range(depth):
        mod = (pmatmul(c_act, w_ada, (layer,), n_out=6 * d, tn=tn_d)[:nb] + b_ada[layer]).reshape(nb, 6, d)
        modp.append(mod[:bp])
        mods.append(jnp.swapaxes(mod[bp:], 0, 1))

    new_rw_s, new_rw_shift, new_gd_s, new_gd_conv = [], [], [], []
    h = None
    for layer in range(depth):
        j = layer // 2
        mpl, msl = modp[layer], mods[layer]
        if layer % 2 == 0:
            out, s_new, sh_new = _rwkv_layer(x, mpl, msl, norm1_w[layer], j, P, bp, t,
                                             state_rwkv_shift[j], state_rwkv[j])
            new_rw_s.append(s_new)
            new_rw_shift.append(sh_new)
        else:
            if h is None:
                _, h = resnorm(x, jnp.zeros_like(x), jnp.stack([mpl[:, 2], mpl[:, 1], mpl[:, 0]], axis=1),
                               msl[2], msl[1], msl[0], norm1_w[layer], t)
            out, s_new, cb_new = _gdn_layer(h, j, P, bp, t, state_gdn_conv[j], state_gdn[j])
            new_gd_s.append(s_new)
            new_gd_conv.append(cb_new)
        x, h = resnorm(x, out, jnp.stack([mpl[:, 2], mpl[:, 4], mpl[:, 3]], axis=1),
                       msl[2], msl[4], msl[3], norm2_w[layer], t)
        act = pmatmul(h, ffn_w_gate, (layer,), n_out=dff, tn=tn_f, w2=ffn_w_up, out_dtype=BF16)
        kh = dff // 2
        part = pmatmul(act, ffn_w_down, (layer,), n_out=d, tn=tn_o, k0=0, kk=kh, single_buffer_w=True)
        ffn = pmatmul(act, ffn_w_down, (layer,), n_out=d, tn=tn_o, k0=kh, kk=kh, add=part, single_buffer_w=True)
        if layer + 1 < depth:
            nxt_p, nxt_s = modp[layer + 1], mods[layer + 1]
            x, h = resnorm(x, ffn, jnp.stack([mpl[:, 5], nxt_p[:, 1], nxt_p[:, 0]], axis=1),
                           msl[5], nxt_s[1], nxt_s[0], norm1_w[layer + 1], t, emit_h=(layer + 1) % 2 == 1)
        else:
            y_p, y_s = final_norm(x, ffn, mpl[:, 5:6], msl[5], final_norm_w, t)
    grp = lambda lst, i: jnp.stack([e[i] for e in lst])
    return (y_p.reshape(bp, t, d), y_s.reshape(bs, 1, d),
            grp(new_rw_s, 0), grp(new_rw_shift, 0), grp(new_gd_s, 0), grp(new_gd_conv, 0),
            grp(new_rw_s, 1), grp(new_rw_shift, 1), grp(new_gd_s, 1), grp(new_gd_conv, 1))
```

```python
import functools

import jax
import jax.numpy as jnp
from jax import lax
from jax.experimental import pallas as pl
from jax.experimental.pallas import tpu as pltpu

F32, BF16 = jnp.float32, jnp.bfloat16

RMS_EPS = 1e-6
RWKV_LN_EPS = 64e-5
RWKV_N = 64
GDN_DK = 128
GDN_CONV = 4
CHUNK = 64
LANES = 128
SUBLANES = 8
GROUP = 32
VMEM_LIMIT = 56 * 2**20


def _cparams(*sem):
    return pltpu.CompilerParams(dimension_semantics=sem, vmem_limit_bytes=VMEM_LIMIT)


def _pick_tile(n, cap, mult):
    best = None
    for t in range(mult, min(n, cap) + 1, mult):
        if n % t == 0:
            best = t
    return best or n


def _group(n):
    g = GROUP
    while n % g:
        g //= 2
    return g


def _dot(a, b):
    return jnp.dot(a.astype(BF16), b.astype(BF16), preferred_element_type=F32)


def _dot_nt(a, b):
    return lax.dot_general(a.astype(BF16), b.astype(BF16), (((1,), (1,)), ((), ())),
                           preferred_element_type=F32)


def _dot_tn(a, b):
    return lax.dot_general(a.astype(BF16), b.astype(BF16), (((0,), (0,)), ((), ())),
                           preferred_element_type=F32)


def _split3(x):
    hi = x.astype(BF16)
    r1 = x - hi.astype(F32)
    mid = r1.astype(BF16)
    lo = (r1 - mid.astype(F32)).astype(BF16)
    return hi, mid, lo


def _tri(n, transpose=False):
    row = lax.broadcasted_iota(jnp.int32, (n, n), 0)
    col = lax.broadcasted_iota(jnp.int32, (n, n), 1)
    m = (row <= col) if transpose else (row >= col)
    return jnp.where(m, 1.0, 0.0).astype(BF16)


def _row_to_col(row):
    n = row.shape[1]
    eye = lax.broadcasted_iota(jnp.int32, (n, n), 0) == lax.broadcasted_iota(jnp.int32, (n, n), 1)
    return jnp.sum(jnp.where(eye, jnp.broadcast_to(row, (n, n)), 0.0), axis=1, keepdims=True)


def _segsum(x, width, exact):
    l = x.shape[1]
    gi = lax.broadcasted_iota(jnp.int32, (LANES, LANES), 0) // width
    gj = lax.broadcasted_iota(jnp.int32, (LANES, LANES), 1) // width
    bd = jnp.where(gi == gj, 1.0, 0.0).astype(BF16)
    hi = x.astype(BF16)
    lo = (x - hi.astype(F32)).astype(BF16) if exact else None
    outs = []
    for t in range(l // LANES):
        sl = slice(t * LANES, (t + 1) * LANES)
        s = jnp.dot(hi[:, sl], bd, preferred_element_type=F32)
        if exact:
            s = s + jnp.dot(lo[:, sl], bd, preferred_element_type=F32)
        outs.append(s)
    return outs[0] if len(outs) == 1 else jnp.concatenate(outs, axis=1)


def _modnorm(x, nw, scale, shift):
    y = x * lax.rsqrt(jnp.mean(x * x, axis=-1, keepdims=True) + RMS_EPS) * nw
    return y * (1.0 + scale) + shift


A_SLOTS = 3


def _mm_kernel(*refs, n_w, has_add, act, k0, ni, nsteps):
    a_hbm = refs[0]
    w_refs = refs[1:1 + n_w]
    pos = 1 + n_w
    add_ref = refs[pos] if has_add else None
    pos += int(has_add)
    o_ref = refs[pos]
    wbf = refs[pos + 1:pos + 1 + n_w]
    a_buf, a_sem = refs[pos + 1 + n_w:]
    tm, kk = a_buf.shape[1:]
    step = pl.program_id(0) * ni + pl.program_id(1)

    def a_copy(s):
        row = pl.multiple_of((s % ni) * tm, 16)
        slot = s % A_SLOTS
        return pltpu.make_async_copy(a_hbm.at[pl.ds(row, tm), pl.ds(k0, kk)], a_buf.at[slot], a_sem.at[slot])

    if ni == 1:
        @pl.when(step == 0)
        def _():
            a_copy(0).start()
            a_copy(0).wait()
    else:
        @pl.when(step == 0)
        def _():
            for s in range(min(A_SLOTS - 1, nsteps)):
                a_copy(s).start()

        @pl.when(step + (A_SLOTS - 1) < nsteps)
        def _():
            a_copy(step + (A_SLOTS - 1)).start()

    @pl.when(pl.program_id(1) == 0)
    def _():
        for w, s in zip(w_refs, wbf):
            s[...] = w[...].astype(BF16)

    if ni == 1:
        a = a_buf[0]
    else:
        a_copy(step).wait()
        a = a_buf[step % A_SLOTS]
    y = jnp.dot(a, wbf[0][...], preferred_element_type=F32)
    if n_w == 2:
        u = jnp.dot(a, wbf[1][...], preferred_element_type=F32)
        y = y * jax.nn.sigmoid(y) * u
    if has_add:
        y = y + add_ref[...]
    if act == "tanh":
        y = jnp.tanh(y)
    elif act == "sigmoid":
        y = jax.nn.sigmoid(y)
    o_ref[...] = y.astype(o_ref.dtype)


def pmatmul(a, w, widx=(), *, n_out, tn, k0=0, kk=None, w2=None, add=None, act=None,
            out_dtype=F32, tm_cap=832):
    m = a.shape[0]
    kk = a.shape[1] if kk is None else kk
    kb = k0 // kk
    if kk <= 512:
        tm_cap = 5 * tm_cap // 2
    tm = _pick_tile(m, tm_cap, 16)
    ws = [w] if w2 is None else [w, w2]
    nlead = len(widx)
    grid = (n_out // tn, m // tm)
    in_specs = [pl.BlockSpec(memory_space=pl.ANY)]
    for _ in ws:
        in_specs.append(pl.BlockSpec((None,) * nlead + (kk, tn),
                                     lambda j, i: tuple(widx) + (kb, j)))
    args = [a] + ws
    if add is not None:
        in_specs.append(pl.BlockSpec((tm, tn), lambda j, i: (i, j)))
        args.append(add)
    return pl.pallas_call(
        functools.partial(_mm_kernel, n_w=len(ws), has_add=add is not None, act=act, k0=k0,
                          ni=grid[1], nsteps=grid[0] * grid[1]),
        grid=grid,
        in_specs=in_specs,
        out_specs=pl.BlockSpec((tm, tn), lambda j, i: (i, j)),
        out_shape=jax.ShapeDtypeStruct((m, n_out), out_dtype),
        scratch_shapes=[pltpu.VMEM((kk, tn), BF16) for _ in ws] + [
            pltpu.VMEM((A_SLOTS, tm, kk), BF16), pltpu.SemaphoreType.DMA((A_SLOTS,))],
        compiler_params=_cparams("arbitrary", "arbitrary"),
        name=f"mm_m{m}_k{kk}_n{n_out}" + ("_glu" if w2 is not None else "") + ("_add" if add is not None else ""),
    )(*args)


def _tile_specs(tr, d, np_tiles, tps, nbp):
    tok = lambda: pl.BlockSpec((tr, d), lambda i: (i, 0))
    seq = lambda n: pl.BlockSpec((None, n, d), lambda i: (jnp.minimum(i // tps, nbp - 1), 0, 0))
    samp = lambda: pl.BlockSpec((tr, d), lambda i: (0, 0))
    row = lambda n=1: pl.BlockSpec((n, d), lambda i: (0, 0))
    return tok, seq, samp, row


def _pre_rwkv_kernel(x_ref, mp_ref, scs_ref, shs_ref, nw_ref, mix_ref, prev_s_ref,
                     o0, o1, o2, o3, o4, o5, hlast_ref, hs_ref, carry_ref, *, np_tiles, tps):
    i = pl.program_id(0)
    is_s = i == np_tiles
    shift = jnp.where(is_s, shs_ref[...], mp_ref[0:1, :])
    scale = jnp.where(is_s, scs_ref[...], mp_ref[1:2, :])
    h = _modnorm(x_ref[...], nw_ref[...], scale, shift)
    tr = h.shape[0]

    @pl.when(i % tps == 0)
    def _():
        carry_ref[...] = jnp.zeros_like(carry_ref)

    rowi = lax.broadcasted_iota(jnp.int32, h.shape, 0)
    prev_p = jnp.where(rowi == 0, carry_ref[...], pltpu.roll(h, 1, axis=0))
    prev = jnp.where(is_s, prev_s_ref[...], prev_p)
    xx = prev - h
    for n, o in enumerate((o0, o1, o2, o3, o4, o5)):
        o[...] = (h + xx * mix_ref[n:n + 1, :]).astype(BF16)
    last = h[tr - 1:tr, :]
    carry_ref[...] = last

    @pl.when(jnp.logical_not(is_s))
    def _():
        hlast_ref[...] = last

    @pl.when(is_s)
    def _():
        hs_ref[...] = h


def pre_rwkv(x, modp, scale_s, shift_s, nw, mix, prev_s, t):
    m, d = x.shape
    tr = scale_s.shape[0]
    nbp = modp.shape[0]
    tps = t // tr
    np_tiles = nbp * tps
    tok, seq, samp, row = _tile_specs(tr, d, np_tiles, tps, nbp)
    outs = pl.pallas_call(
        functools.partial(_pre_rwkv_kernel, np_tiles=np_tiles, tps=tps),
        grid=(np_tiles + 1,),
        in_specs=[tok(), seq(2), samp(), samp(), row(), row(6), samp()],
        out_specs=[tok() for _ in range(6)] + [seq(1), samp()],
        out_shape=[jax.ShapeDtypeStruct((m, d), BF16)] * 6 + [jax.ShapeDtypeStruct((nbp, 1, d), F32),
                                                              jax.ShapeDtypeStruct((tr, d), F32)],
        scratch_shapes=[pltpu.VMEM((1, d), F32)],
        compiler_params=_cparams("arbitrary"),
        name="pre_rwkv",
    )(x, modp, scale_s, shift_s, nw.reshape(1, d), mix, prev_s)
    return outs[:6], outs[6], outs[7]


def _resnorm_kernel(x_ref, dl_ref, mp_ref, gs_ref, scs_ref, shs_ref, nw_ref, xo_ref, *h_refs, np_tiles):
    is_s = pl.program_id(0) == np_tiles
    gate = jnp.where(is_s, gs_ref[...], mp_ref[0:1, :])
    xn = x_ref[...] + gate * dl_ref[...]
    xo_ref[...] = xn
    if h_refs:
        scale = jnp.where(is_s, scs_ref[...], mp_ref[1:2, :])
        shift = jnp.where(is_s, shs_ref[...], mp_ref[2:3, :])
        h_refs[0][...] = _modnorm(xn, nw_ref[...], scale, shift).astype(BF16)


def resnorm(x, delta, modp, gate_s, scale_s, shift_s, nw, t, emit_h=True):
    m, d = x.shape
    tr = gate_s.shape[0]
    nbp = modp.shape[0]
    tps = t // tr
    np_tiles = nbp * tps
    tok, seq, samp, row = _tile_specs(tr, d, np_tiles, tps, nbp)
    outs = pl.pallas_call(
        functools.partial(_resnorm_kernel, np_tiles=np_tiles),
        grid=(np_tiles + 1,),
        in_specs=[tok(), tok(), seq(3), samp(), samp(), samp(), row()],
        out_specs=[tok()] + ([tok()] if emit_h else []),
        out_shape=[jax.ShapeDtypeStruct((m, d), F32)] + ([jax.ShapeDtypeStruct((m, d), BF16)] if emit_h else []),
        compiler_params=_cparams("arbitrary"),
        name="resnorm",
    )(x, delta, modp, gate_s, scale_s, shift_s, nw.reshape(1, d))
    return (outs[0], outs[1]) if emit_h else (outs[0], None)


def _final_kernel(x_ref, dl_ref, mp_ref, gs_ref, nw_ref, yp_ref, ys_ref, *, np_tiles):
    is_s = pl.program_id(0) == np_tiles
    gate = jnp.where(is_s, gs_ref[...], mp_ref[0:1, :])
    xn = x_ref[...] + gate * dl_ref[...]
    y = xn * lax.rsqrt(jnp.mean(xn * xn, axis=-1, keepdims=True) + RMS_EPS) * nw_ref[...]

    @pl.when(jnp.logical_not(is_s))
    def _():
        yp_ref[...] = y

    @pl.when(is_s)
    def _():
        ys_ref[...] = y


def final_norm(x, delta, modp, gate_s, nw, t):
    m, d = x.shape
    tr = gate_s.shape[0]
    nbp = modp.shape[0]
    tps = t // tr
    np_tiles = nbp * tps
    tok, seq, samp, row = _tile_specs(tr, d, np_tiles, tps, nbp)
    return pl.pallas_call(
        functools.partial(_final_kernel, np_tiles=np_tiles),
        grid=(np_tiles + 1,),
        in_specs=[tok(), tok(), seq(1), samp(), row()],
        out_specs=[pl.BlockSpec((tr, d), lambda i: (jnp.minimum(i, np_tiles - 1), 0)), samp()],
        out_shape=[jax.ShapeDtypeStruct((np_tiles * tr, d), F32), jax.ShapeDtypeStruct((tr, d), F32)],
        compiler_params=_cparams("arbitrary"),
        name="final_norm",
    )(x, delta, modp, gate_s, nw.reshape(1, d))


def _gdn_conv_kernel(x_ref, w_ref, buf_ref, o_ref, prev_ref, *, np_tiles, tps):
    sec = pl.program_id(0)
    i = pl.program_id(1)
    is_s = i == np_tiles

    x = x_ref[...]

    @pl.when(i % tps == 0)
    def _():
        prev_ref[...] = jnp.zeros_like(prev_ref)

    def tap(k):
        return w_ref[GDN_CONV - 1 - k:GDN_CONV - k, :]

    def finish(y):
        a = y * jax.nn.sigmoid(y)
        qscale = jnp.where(sec == 0, GDN_DK ** -0.5, 1.0)
        nrm = a * (lax.rsqrt(_segsum(a * a, GDN_DK, exact=False) + 1e-6) * qscale)
        o_ref[...] = jnp.where(sec == 2, a, nrm)

    @pl.when(is_s)
    def _():
        y = x * tap(0)
        for k in range(1, GDN_CONV):
            y = y + buf_ref[GDN_CONV - 1 - k] * tap(k)
        finish(y)

    @pl.when(jnp.logical_not(is_s))
    def _():
        prev = prev_ref[...]
        rowi = lax.broadcasted_iota(jnp.int32, x.shape, 0)
        y = x * tap(0)
        for k in range(1, GDN_CONV):
            back = jnp.where(rowi >= k, pltpu.roll(x, k, axis=0), pltpu.roll(prev, k, axis=0))
            y = y + back * tap(k)
        prev_ref[...] = x
        finish(y)


def gdn_conv(main, conv_w, bufs, nbp, t):
    m = main.shape[0]
    d = main.shape[1] // 4
    tr = bufs.shape[1]
    tps = t // tr
    np_tiles = nbp * tps
    return pl.pallas_call(
        functools.partial(_gdn_conv_kernel, np_tiles=np_tiles, tps=tps),
        grid=(3, np_tiles + 1),
        in_specs=[pl.BlockSpec((tr, d), lambda s, i: (i, s)),
                  pl.BlockSpec((GDN_CONV, d), lambda s, i: (0, s)),
                  pl.BlockSpec((GDN_CONV - 1, tr, d), lambda s, i: (0, 0, s))],
        out_specs=pl.BlockSpec((None, tr, d), lambda s, i: (s, i, 0)),
        out_shape=jax.ShapeDtypeStruct((3, m, d), F32),
        scratch_shapes=[pltpu.VMEM((tr, d), F32)],
        compiler_params=_cparams("arbitrary", "arbitrary"),
        name="gdn_conv",
    )(main, conv_w, bufs)


def _inv_kernel(m_ref, n_ref, *, c):
    nblk = c // SUBLANES
    zeros = jnp.zeros((SUBLANES * c, LANES), F32)
    for ib in range(nblk):
        n_ref[ib * SUBLANES * c:(ib + 1) * SUBLANES * c, :] = zeros

        def row(ii, carry, ib=ib):
            base = pl.multiple_of((ib * SUBLANES + ii) * c, c)
            acc = [m_ref[pl.ds(base + SUBLANES * k, SUBLANES), :] for k in range(ib + 1)]
            for j in range((ib + 1) * SUBLANES):
                coef = m_ref[pl.ds(base + j, 1), :]
                for k in range(j // SUBLANES + 1):
                    acc[k] = acc[k] + coef * n_ref[j * c + SUBLANES * k:j * c + SUBLANES * (k + 1), :]
            for k in range(ib + 1):
                n_ref[pl.ds(base + SUBLANES * k, SUBLANES), :] = -acc[k]
            return carry

        lax.fori_loop(0, SUBLANES, row, 0)


def tri_inverse(mats):
    shape = mats.shape
    c = shape[-1]
    u = 1
    for s in shape[:-2]:
        u *= s
    up = -(-u // LANES) * LANES
    flat = jnp.transpose(mats.reshape(u, c, c), (1, 2, 0)).reshape(c * c, u)
    if up != u:
        flat = jnp.pad(flat, ((0, 0), (0, up - u)))
    out = pl.pallas_call(
        functools.partial(_inv_kernel, c=c),
        grid=(up // LANES,),
        in_specs=[pl.BlockSpec((c * c, LANES), lambda g: (0, g))],
        out_specs=pl.BlockSpec((c * c, LANES), lambda g: (0, g)),
        out_shape=jax.ShapeDtypeStruct((c * c, up), F32),
        compiler_params=_cparams("arbitrary"),
        name="tri_inverse",
    )(flat)
    return jnp.transpose(out[:, :u].reshape(c, c, u), (2, 0, 1)).reshape(shape)


def _rwkv_token_prep(r, k, wl, al, w0, a0, k_k, k_a, r_k):
    w_log = -jax.nn.softplus(-(w0 + wl)) - 0.5
    lw = -jnp.exp(w_log)
    a = jax.nn.sigmoid(a0 + al)
    kx = k * k_k
    kap = kx * lax.rsqrt(_segsum(kx * kx, RWKV_N, exact=False) + 1e-6)
    k_mod = k * (1.0 + (a - 1.0) * k_a)
    rk = _segsum(r * k_mod * r_k, RWKV_N, exact=True)
    return lw, kap, k_mod, kap * a, rk


def _rw1_kernel(r_ref, k_ref, v_ref, wl_ref, al_ref, w0_ref, a0_ref, kk_ref, ka_ref, rk_ref,
                kaph_ref, rh_ref, kh_ref, bh_ref, khp_ref, bhp_ref, vb_ref, bonus_ref, pc_ref, mb_ref):
    v = v_ref[...]
    lw, kap, k, b, rk = _rwkv_token_prep(r_ref[...], k_ref[...], wl_ref[...], al_ref[...], w0_ref[...],
                                         a0_ref[...], kk_ref[...], ka_ref[...], rk_ref[...])
    bonus_ref[...] = rk * v
    vb_ref[...] = v.astype(BF16)
    c, lb = lw.shape
    tri = _tri(c)
    hi, mid, lo = _split3(lw)
    p = (jnp.dot(tri, hi, preferred_element_type=F32) + jnp.dot(tri, mid, preferred_element_type=F32)
         + jnp.dot(tri, lo, preferred_element_type=F32))
    pc = p[c - 1:c, :]
    en = jnp.exp(-p)
    ec = jnp.exp(pc - p)
    kaph = kap * jnp.exp(p - lw)
    bh = (b * en).astype(BF16)
    kaph_ref[...] = kaph.astype(BF16)
    rh_ref[...] = (r_ref[...] * jnp.exp(p)).astype(BF16)
    kh_ref[...] = (k * en).astype(BF16)
    bh_ref[...] = bh
    khp_ref[...] = (k * ec).astype(BF16)
    bhp_ref[...] = (b * ec).astype(BF16)
    pc_ref[...] = jnp.exp(pc)
    lane = lax.broadcasted_iota(jnp.int32, (c, LANES), 1)
    m0 = lane < RWKV_N
    rowi = lax.broadcasted_iota(jnp.int32, (2 * c, c), 0) % c
    coli = lax.broadcasted_iota(jnp.int32, (2 * c, c), 1)
    strict = rowi > coli
    for jt in range(lb // LANES):
        kp = kaph[:, jt * LANES:(jt + 1) * LANES]
        bp = bh[:, jt * LANES:(jt + 1) * LANES]
        lhs = jnp.concatenate([jnp.where(m0, kp, 0.0), jnp.where(m0, 0.0, kp)], axis=0)
        g = _dot_nt(lhs, bp)
        g = jnp.where(strict, g, 0.0)
        mb_ref[2 * jt] = g[:c]
        mb_ref[2 * jt + 1] = g[c:]


def _rw3_kernel(kaph_ref, rh_ref, kh_ref, bh_ref, khp_ref, bhp_ref, v_ref, pc_ref, n_ref,
                bonus_ref, g_ref, lnw_ref, lnb_ref, yg_ref, sfin_ref, a_ref):
    ci = pl.program_id(1)
    c = kaph_ref.shape[0]
    npair = a_ref.shape[0]

    @pl.when(ci == 0)
    def _():
        a_ref[...] = jnp.zeros_like(a_ref)

    m0 = lax.broadcasted_iota(jnp.int32, (c, LANES), 1) < RWKV_N
    row = lax.broadcasted_iota(jnp.int32, (c, LANES), 0)
    src = lax.broadcasted_iota(jnp.int32, (c, LANES), 1) % c
    strict = row > src
    incl = row >= src
    rr = lax.broadcasted_iota(jnp.int32, (LANES, LANES), 0)
    cc = lax.broadcasted_iota(jnp.int32, (LANES, LANES), 1)
    blockdiag = (rr < RWKV_N) == (cc < RWKV_N)
    ng = _group(npair)
    rng = range(ng)

    def by_head(t):
        t32 = t.astype(F32)
        return jnp.concatenate([jnp.where(m0, t32, 0.0), jnp.where(m0, 0.0, t32)], axis=0).astype(BF16)

    def group(gi, carry):
        js = [gi * ng + g for g in rng]
        lss = [pl.ds(pl.multiple_of(j * LANES, LANES), LANES) for j in js]
        lhs = [jnp.concatenate([kaph_ref[:, ls], rh_ref[:, ls]], axis=0) for ls in lss]
        vv = [v_ref[:, ls] for ls in lss]
        a0 = [a_ref[j] for j in js]
        x = [_dot(l, a) for l, a in zip(lhs, a0)]
        gk = [_dot_nt(lhs[g], by_head(kh_ref[:, lss[g]])) for g in rng]
        gb = [_dot_nt(lhs[g][c:], by_head(bh_ref[:, lss[g]])) for g in rng]
        vbd = [by_head(vv[g]) for g in rng]
        mk = [jnp.where(strict, gk[g][:c], 0.0) for g in rng]
        lrk = [jnp.where(incl, gk[g][c:], 0.0) for g in rng]
        lrb = [jnp.where(incl, gb[g], 0.0) for g in rng]
        mkv = [_dot(mk[g], vbd[g]) for g in rng]
        lrkv = [_dot(lrk[g], vbd[g]) for g in rng]
        kv = [_dot_tn(khp_ref[:, lss[g]], vv[g]) for g in rng]
        rhs = [x[g][:c] + mkv[g] for g in rng]
        npk = [jnp.concatenate([n_ref[2 * js[g]], n_ref[2 * js[g] + 1]], axis=1) for g in rng]
        nr = [_dot(npk[g], by_head(rhs[g])) for g in rng]
        u = [rhs[g] + nr[g] for g in rng]
        lu = [_dot(lrb[g], by_head(u[g])) for g in rng]
        bu = [_dot_tn(bhp_ref[:, lss[g]], u[g]) for g in rng]
        y = [x[g][c:] + (lrkv[g] - lu[g]) for g in rng]
        mu = [_segsum(y[g], RWKV_N, exact=True) * (1.0 / RWKV_N) for g in range(ng)]
        yc = [y[g] - mu[g] for g in range(ng)]
        var = [_segsum(yc[g] * yc[g], RWKV_N, exact=False) * (1.0 / RWKV_N) for g in range(ng)]
        for g in range(ng):
            ls = lss[g]
            yn = yc[g] * lax.rsqrt(var[g] + RWKV_LN_EPS) * lnw_ref[:, ls] + lnb_ref[:, ls]
            yg_ref[:, ls] = ((yn + bonus_ref[:, ls]) * g_ref[:, ls]).astype(BF16)
            pcc = _row_to_col(pc_ref[:, ls])
            a_ref[js[g]] = pcc * a0[g] + jnp.where(blockdiag, kv[g] - bu[g], 0.0)
        return carry

    lax.fori_loop(0, npair // ng, group, 0)

    @pl.when(ci == pl.num_programs(1) - 1)
    def _():
        sfin_ref[...] = a_ref[...]


def rwkv_prompt_scan(r, k, v, wl, al, g, w0, a0, k_k, k_a, r_k, ln_w, ln_b, bsz, t):
    d = r.shape[1]
    c = CHUNK
    nc = t // c
    h = d // RWKV_N
    lb = min(d, 4096)
    row2 = lambda x: x.reshape(1, d)
    tok2 = lambda: pl.BlockSpec((c, lb), lambda bi, ci, li: (bi * nc + ci, li))
    prm = lambda: pl.BlockSpec((1, lb), lambda bi, ci, li: (0, li))
    tok = lambda: pl.BlockSpec((None, c, lb), lambda bi, ci, li: (bi, ci, li))
    bf = jax.ShapeDtypeStruct((bsz, t, d), BF16)
    kaph, rh, kh, bh, khp, bhp, vb, bonus, pc, mb = pl.pallas_call(
        _rw1_kernel,
        grid=(bsz, nc, d // lb),
        in_specs=[tok2() for _ in range(5)] + [prm() for _ in range(5)],
        out_specs=[tok() for _ in range(8)] + [
            pl.BlockSpec((None, None, 1, lb), lambda bi, ci, li: (bi, ci, 0, li)),
            pl.BlockSpec((None, None, lb // RWKV_N, c, c), lambda bi, ci, li: (bi, ci, li, 0, 0)),
        ],
        out_shape=[bf] * 7 + [jax.ShapeDtypeStruct((bsz, t, d), F32),
                              jax.ShapeDtypeStruct((bsz, nc, 1, d), F32),
                              jax.ShapeDtypeStruct((bsz, nc, h, c, c), F32)],
        compiler_params=_cparams("arbitrary", "arbitrary", "arbitrary"),
        name="rwkv_chunk_prep",
    )(r, k, v, wl, al, row2(w0), row2(a0), row2(k_k), row2(k_a), row2(r_k))
    assert 2 * c == LANES
    nmat = tri_inverse(mb)
    tokd = lambda: pl.BlockSpec((None, c, d), lambda bi, ci: (bi, ci, 0))
    tok2d = lambda: pl.BlockSpec((c, d), lambda bi, ci: (bi * nc + ci, 0))
    prmd = lambda: pl.BlockSpec((1, d), lambda bi, ci: (0, 0))
    yg, sfin = pl.pallas_call(
        _rw3_kernel,
        grid=(bsz, nc),
        in_specs=[tokd() for _ in range(7)] + [
            pl.BlockSpec((None, None, 1, d), lambda bi, ci: (bi, ci, 0, 0)),
            pl.BlockSpec((None, None, h, c, c), lambda bi, ci: (bi, ci, 0, 0, 0)),
            tokd(), tok2d(), prmd(), prmd(),
        ],
        out_specs=[tok2d(), pl.BlockSpec((None, h // 2, LANES, LANES), lambda bi, ci: (bi, 0, 0, 0))],
        out_shape=[jax.ShapeDtypeStruct((bsz * t, d), BF16),
                   jax.ShapeDtypeStruct((bsz, h // 2, LANES, LANES), F32)],
        scratch_shapes=[pltpu.VMEM((h // 2, LANES, LANES), F32)],
        compiler_params=_cparams("arbitrary", "arbitrary"),
        name="rwkv_chunk_scan",
    )(kaph, rh, kh, bh, khp, bhp, vb, pc, nmat, bonus, g, row2(ln_w), row2(ln_b))
    n = RWKV_N
    s_even = sfin[:, :, :n, :n]
    s_odd = sfin[:, :, n:, n:]
    s = jnp.stack([s_even, s_odd], axis=2).reshape(bsz, h, n, n)
    return yg, jnp.swapaxes(s, -1, -2)


def _rws_kernel(s_ref, w_ref, kap_ref, b_ref, k_ref, v_ref, r_ref, so_ref, y_ref):
    s = s_ref[...]
    n = s.shape[-1]
    eye = (lax.broadcasted_iota(jnp.int32, (n, n), 0) == lax.broadcasted_iota(jnp.int32, (n, n), 1))[None]
    sa = -jnp.sum(s * kap_ref[...], axis=2, keepdims=True)
    vcol = jnp.sum(jnp.where(eye, v_ref[...], 0.0), axis=2, keepdims=True)
    sn = s * w_ref[...] + sa * b_ref[...] + vcol * k_ref[...]
    so_ref[...] = sn
    ycol = jnp.sum(sn * r_ref[...], axis=2, keepdims=True)
    y_ref[...] = jnp.sum(jnp.where(eye, ycol, 0.0), axis=1, keepdims=True)


def rwkv_decode_step(s0, w, kap, b, k, v, r):
    bsz, h, n, _ = s0.shape
    vec = lambda: pl.BlockSpec((None, h, 1, n), lambda bi: (bi, 0, 0, 0))
    st = lambda: pl.BlockSpec((None, h, n, n), lambda bi: (bi, 0, 0, 0))
    rows = lambda x: x.reshape(bsz, h, 1, n)
    sn, y = pl.pallas_call(
        _rws_kernel,
        grid=(bsz,),
        in_specs=[st()] + [vec() for _ in range(6)],
        out_specs=[st(), vec()],
        out_shape=[jax.ShapeDtypeStruct(s0.shape, F32), jax.ShapeDtypeStruct((bsz, h, 1, n), F32)],
        compiler_params=_cparams("arbitrary"),
        name="rwkv_decode",
    )(s0, rows(w), rows(kap), rows(b), rows(k), rows(v), rows(r))
    return y.reshape(bsz, h, n), sn


def _gd_head_scalars(g_ref, gcr_s, tri_t):
    hi, mid, lo = _split3(g_ref[...])
    gcr_s[...] = (jnp.dot(hi, tri_t, preferred_element_type=F32)
                  + jnp.dot(mid, tri_t, preferred_element_type=F32)
                  + jnp.dot(lo, tri_t, preferred_element_type=F32))


def _gd1_kernel(k_ref, g_ref, beta_ref, l_ref, gcr_s):
    c = k_ref.shape[0]
    gh = g_ref.shape[0]
    _gd_head_scalars(g_ref, gcr_s, _tri(c, transpose=True))
    row = lax.broadcasted_iota(jnp.int32, (c, c), 0)
    col = lax.broadcasted_iota(jnp.int32, (c, c), 1)
    strict = row > col
    ng = _group(gh)

    def group(gi, carry):
        js = [gi * ng + g for g in range(ng)]
        gr = [gcr_s[pl.ds(j, 1), :] for j in js]
        gcol = [_row_to_col(x) for x in gr]
        bcol = [_row_to_col(beta_ref[pl.ds(j, 1), :]) for j in js]
        kk = [k_ref[:, pl.ds(pl.multiple_of(j * GDN_DK, GDN_DK), GDN_DK)] for j in js]
        gram = [_dot_nt(kk[g] * bcol[g], kk[g]) for g in range(ng)]
        for g in range(ng):
            dec = jnp.exp(jnp.where(strict, gcol[g] - gr[g], 0.0))
            l_ref[js[g]] = jnp.where(strict, gram[g] * dec, 0.0)
        return carry

    lax.fori_loop(0, gh // ng, group, 0)


def _gd3_kernel(q_ref, k_ref, v_ref, g_ref, beta_ref, n_ref, z_ref, nw_ref, og_ref, sfin_ref, s_ref, gcr_s):
    ci = pl.program_id(1)
    c = k_ref.shape[0]
    gh = g_ref.shape[0]

    @pl.when(ci == 0)
    def _():
        s_ref[...] = jnp.zeros_like(s_ref)

    _gd_head_scalars(g_ref, gcr_s, _tri(c, transpose=True))
    row = lax.broadcasted_iota(jnp.int32, (c, c), 0)
    col = lax.broadcasted_iota(jnp.int32, (c, c), 1)
    incl = row >= col
    ng = _group(gh)
    rng = range(ng)

    def group(gi, carry):
        js = [gi * ng + g for g in rng]
        lss = [pl.ds(pl.multiple_of(j * GDN_DK, GDN_DK), GDN_DK) for j in js]
        gr = [gcr_s[pl.ds(j, 1), :] for j in js]
        gcol = [_row_to_col(x) for x in gr]
        bcol = [_row_to_col(beta_ref[pl.ds(j, 1), :]) for j in js]
        glast = [x[:, c - 1:c] for x in gr]
        q = [q_ref[:, ls] for ls in lss]
        kk = [k_ref[:, ls] for ls in lss]
        vv = [v_ref[:, ls] for ls in lss]
        s = [s_ref[j] for j in js]
        eg = [jnp.exp(x) for x in gcol]
        rhs = [jnp.concatenate([vv[g] * bcol[g], kk[g] * bcol[g] * eg[g]], axis=1) for g in rng]
        qk = [_dot_nt(q[g], kk[g]) for g in rng]
        nr = [_dot(n_ref[js[g]], rhs[g]) for g in rng]
        qs = [_dot(q[g] * eg[g], s[g]) for g in rng]
        sol = [rhs[g] + nr[g] for g in rng]
        ws = [_dot(sol[g][:, GDN_DK:], s[g]) for g in rng]
        v_new = [sol[g][:, :GDN_DK] - ws[g] for g in rng]
        attn = [jnp.where(incl, qk[g] * jnp.exp(jnp.where(incl, gcol[g] - gr[g], 0.0)), 0.0) for g in rng]
        av = [_dot(attn[g], v_new[g]) for g in rng]
        kv = [_dot_tn(kk[g] * jnp.exp(glast[g] - gcol[g]), v_new[g]) for g in rng]
        for g in rng:
            o = qs[g] + av[g]
            on = o * lax.rsqrt(jnp.mean(o * o, axis=-1, keepdims=True) + RMS_EPS) * nw_ref[...]
            z = z_ref[:, lss[g]]
            og_ref[:, lss[g]] = (on * (z * jax.nn.sigmoid(z))).astype(BF16)
            s_ref[js[g]] = s[g] * jnp.exp(glast[g]) + kv[g]
        return carry

    lax.fori_loop(0, gh // ng, group, 0)

    @pl.when(ci == pl.num_programs(1) - 1)
    def _():
        sfin_ref[...] = s_ref[...]


def gdn_prompt_scan(qkv, main, g, beta, norm_w, bsz, t):
    d = qkv.shape[2]
    c = CHUNK
    nc = t // c
    gh = d // GDN_DK
    rows = lambda x: jnp.swapaxes(x.reshape(bsz, nc, c, gh), -1, -2)
    g_r, beta_r = rows(g), rows(beta)
    sect = lambda s: pl.BlockSpec((None, c, d), lambda bi, ci: (s, bi * nc + ci, 0))
    hrow = lambda: pl.BlockSpec((None, None, gh, c), lambda bi, ci: (bi, ci, 0, 0))
    mat = lambda: pl.BlockSpec((None, None, gh, c, c), lambda bi, ci: (bi, ci, 0, 0, 0))
    lmat = pl.pallas_call(
        _gd1_kernel,
        grid=(bsz, nc),
        in_specs=[sect(1), hrow(), hrow()],
        out_specs=mat(),
        out_shape=jax.ShapeDtypeStruct((bsz, nc, gh, c, c), F32),
        scratch_shapes=[pltpu.VMEM((gh, c), F32)],
        compiler_params=_cparams("arbitrary", "arbitrary"),
        name="gdn_chunk_prep",
    )(qkv, g_r, beta_r)
    nmat = tri_inverse(lmat)
    og, sfin = pl.pallas_call(
        _gd3_kernel,
        grid=(bsz, nc),
        in_specs=[sect(0), sect(1), sect(2), hrow(), hrow(), mat(),
                  pl.BlockSpec((c, d), lambda bi, ci: (bi * nc + ci, 3)),
                  pl.BlockSpec((1, GDN_DK), lambda bi, ci: (0, 0))],
        out_specs=[pl.BlockSpec((c, d), lambda bi, ci: (bi * nc + ci, 0)),
                   pl.BlockSpec((None, gh, GDN_DK, GDN_DK), lambda bi, ci: (bi, 0, 0, 0))],
        out_shape=[jax.ShapeDtypeStruct((bsz * t, d), BF16),
                   jax.ShapeDtypeStruct((bsz, gh, GDN_DK, GDN_DK), F32)],
        scratch_shapes=[pltpu.VMEM((gh, GDN_DK, GDN_DK), F32), pltpu.VMEM((gh, c), F32)],
        compiler_params=_cparams("arbitrary", "arbitrary"),
        name="gdn_chunk_scan",
    )(qkv, qkv, qkv, g_r, beta_r, nmat, main, norm_w.reshape(1, GDN_DK))
    return og, sfin


def _gds_kernel(s_ref, q_ref, k_ref, v_ref, beta_ref, eg_ref, so_ref, o_ref):
    s = s_ref[...]
    n = s.shape[-1]
    eye = (lax.broadcasted_iota(jnp.int32, (n, n), 0) == lax.broadcasted_iota(jnp.int32, (n, n), 1))[None]
    kcol = jnp.sum(jnp.where(eye, k_ref[...], 0.0), axis=2, keepdims=True)
    qcol = jnp.sum(jnp.where(eye, q_ref[...], 0.0), axis=2, keepdims=True)
    eg = eg_ref[...]
    ks = jnp.sum(kcol * s, axis=1, keepdims=True)
    v_new = beta_ref[...] * (v_ref[...] - eg * ks)
    sn = s * eg + kcol * v_new
    so_ref[...] = sn
    o_ref[...] = jnp.sum(qcol * sn, axis=1, keepdims=True)


def gdn_decode_step(s0, q, k, v, beta, g):
    bsz, gh, dk, _ = s0.shape
    d = gh * dk
    rows = lambda x: x.reshape(bsz, gh, 1, dk)
    wide = lambda x: jnp.broadcast_to(x[:, :, None, None], (bsz, gh, 1, dk))
    vec = lambda: pl.BlockSpec((None, gh, 1, dk), lambda bi: (bi, 0, 0, 0))
    st = lambda: pl.BlockSpec((None, gh, dk, dk), lambda bi: (bi, 0, 0, 0))
    sn, o = pl.pallas_call(
        _gds_kernel,
        grid=(bsz,),
        in_specs=[st()] + [vec() for _ in range(5)],
        out_specs=[st(), vec()],
        out_shape=[jax.ShapeDtypeStruct(s0.shape, F32), jax.ShapeDtypeStruct((bsz, gh, 1, dk), F32)],
        compiler_params=_cparams("arbitrary"),
        name="gdn_decode",
    )(s0, rows(q), rows(k), rows(v), wide(beta), wide(jnp.exp(g)))
    return o.reshape(bsz, d), sn


def _rwkv_layer(x, modp, mods, nw, j, P, bp, t, shift_prev, s0):
    m, d = x.shape
    mp = bp * t
    bs = m - mp
    nh = d // RWKV_N
    xs, hlast, h_s = pre_rwkv(x, modp[:, 0:2], mods[1], mods[0], nw, P['rw_mix'][j], shift_prev, t)
    tn = min(d, 512)
    mm = lambda a, w, widx, n, tnn, **kw: pmatmul(a, w, widx, n_out=n, tn=tnn, **kw)
    r = mm(xs[0], P['rw_w_rkv'], (j, 0), d, tn)
    k = mm(xs[1], P['rw_w_rkv'], (j, 1), d, tn)
    v = mm(xs[2], P['rw_w_rkv'], (j, 2), d, tn)
    lora = P['rw_w1'].shape[-1]
    wl = mm(mm(xs[3], P['rw_w1'], (j,), lora, lora, act="tanh", out_dtype=BF16), P['rw_w2'], (j,), d, tn)
    al = mm(mm(xs[4], P['rw_a1'], (j,), lora, lora, out_dtype=BF16), P['rw_a2'], (j,), d, tn)
    gl = P['rw_g1'].shape[-1]
    glp = -(-gl // LANES) * LANES
    g1 = jnp.pad(P['rw_g1'][j], ((0, 0), (0, glp - gl)))
    g2 = jnp.pad(P['rw_g2'][j], ((0, glp - gl), (0, 0)))
    g = mm(mm(xs[5], g1, (), glp, glp, act="sigmoid", out_dtype=BF16), g2, (), d, tn)
    prm = [P[n][j] for n in ('rw_w0', 'rw_a0', 'rw_k_k', 'rw_k_a')] + [P['rw_r_k'][j].reshape(d)]
    ln_w, ln_b = P['rw_ln_w'][j], P['rw_ln_b'][j]
    yg_p, sp = rwkv_prompt_scan(r, k, v, wl, al, g, *prm, ln_w, ln_b, bp, t)

    rs, ks, vs, wls, als, gs = (a[mp:] for a in (r, k, v, wl, al, g))
    w_log = -jax.nn.softplus(-(prm[0] + wls)) - 0.5
    dec = jnp.exp(-jnp.exp(w_log))
    a = jax.nn.sigmoid(prm[1] + als)
    hv = lambda z: z.reshape(bs, nh, RWKV_N)
    kx = hv(ks * prm[2])
    kap = kx * lax.rsqrt(jnp.sum(kx * kx, axis=-1, keepdims=True) + 1e-6)
    k_mod = ks * (1.0 + (a - 1.0) * prm[3])
    ys, ss = rwkv_decode_step(s0, hv(dec), kap, kap * hv(a), hv(k_mod), hv(vs), hv(rs))
    mu = jnp.mean(ys, axis=-1, keepdims=True)
    var = jnp.mean(jnp.square(ys - mu), axis=-1, keepdims=True)
    yn = ((ys - mu) * lax.rsqrt(var + RWKV_LN_EPS)).reshape(bs, d) * ln_w + ln_b
    bonus = jnp.sum(hv(rs) * hv(k_mod) * prm[4].reshape(nh, RWKV_N), axis=-1, keepdims=True) * hv(vs)
    yg_s = ((yn + bonus.reshape(bs, d)) * gs).astype(BF16)

    out = mm(jnp.concatenate([yg_p, yg_s], axis=0), P['rw_w_o'], (j,), d, tn)
    return out, (sp, ss), (hlast[:, 0], h_s)


def _gdn_layer(h, j, P, bp, t, conv_prev, s0):
    m, d = h.shape
    mp = bp * t
    bs = m - mp
    gh = d // GDN_DK
    cdim = 3 * d
    tn = min(d, 512)
    w_in = P['gd_w_in']
    main = pmatmul(h, w_in, (j,), n_out=4 * d, tn=tn)
    w_tail = jnp.pad(w_in[j][:, 4 * d:], ((0, 0), (0, LANES - 2 * gh)))
    tail = pmatmul(h, w_tail, (), n_out=LANES, tn=LANES)
    qkv = gdn_conv(main, P['gd_conv_w'][j], jnp.swapaxes(conv_prev, 0, 1), bp, t)
    beta = jax.nn.sigmoid(tail[:, :gh])
    g = -jnp.exp(P['gd_A_log'][j]) * jax.nn.softplus(tail[:, gh:2 * gh] + P['gd_dt_bias'][j])
    norm_w = P['gd_norm_w'][j]
    og_p, sp = gdn_prompt_scan(qkv, main, g[:mp].reshape(bp, t, gh), beta[:mp].reshape(bp, t, gh), norm_w, bp, t)

    o_s, ss = gdn_decode_step(s0, qkv[0, mp:], qkv[1, mp:], qkv[2, mp:], beta[mp:], g[mp:])
    oh = o_s.reshape(bs, gh, GDN_DK)
    oh = oh * lax.rsqrt(jnp.mean(oh * oh, axis=-1, keepdims=True) + RMS_EPS) * norm_w
    og_s = (oh.reshape(bs, d) * jax.nn.silu(main[mp:, cdim:])).astype(BF16)

    out = pmatmul(jnp.concatenate([og_p, og_s], axis=0), P['gd_w_out'], (j,), n_out=d, tn=tn)
    conv_p = jnp.stack([main[(b + 1) * t - (GDN_CONV - 1):(b + 1) * t, :cdim] for b in range(bp)])
    conv_s = jnp.concatenate([conv_prev[:, 1:], main[mp:, None, :cdim]], axis=1)
    return out, (sp, ss), (conv_p, conv_s)


def kernel(x_prompt, x_sample, state_rwkv, state_rwkv_shift, state_gdn, state_gdn_conv, c_prompt, c_sample,
           w_ada, b_ada, norm1_w, norm2_w, rw_mix, rw_w_rkv, rw_w0, rw_w1, rw_w2, rw_a0, rw_a1, rw_a2,
           rw_g1, rw_g2, rw_k_k, rw_k_a, rw_r_k, rw_ln_w, rw_ln_b, rw_w_o, gd_w_in, gd_conv_w, gd_A_log,
           gd_dt_bias, gd_norm_w, gd_w_out, ffn_w_gate, ffn_w_up, ffn_w_down, final_norm_w):
    P = {
        'rw_mix': rw_mix, 'rw_w_rkv': rw_w_rkv, 'rw_w0': rw_w0, 'rw_w1': rw_w1, 'rw_w2': rw_w2,
        'rw_a0': rw_a0, 'rw_a1': rw_a1, 'rw_a2': rw_a2, 'rw_g1': rw_g1, 'rw_g2': rw_g2,
        'rw_k_k': rw_k_k, 'rw_k_a': rw_k_a, 'rw_r_k': rw_r_k, 'rw_ln_w': rw_ln_w, 'rw_ln_b': rw_ln_b,
        'rw_w_o': rw_w_o, 'gd_w_in': gd_w_in, 'gd_conv_w': gd_conv_w, 'gd_A_log': gd_A_log,
        'gd_dt_bias': gd_dt_bias, 'gd_norm_w': gd_norm_w, 'gd_w_out': gd_w_out,
    }
    bp, t, d = x_prompt.shape
    bs = x_sample.shape[0]
    depth = w_ada.shape[0]
    dff = ffn_w_gate.shape[-1]
    mp = bp * t
    assert x_sample.shape[1] == 1 and t % CHUNK == 0 and t % bs == 0 and bs % 16 == 0 and d % (2 * LANES) == 0

    x = jnp.concatenate([x_prompt.reshape(mp, d), x_sample.reshape(bs, d)], axis=0)
    c_all = jnp.concatenate([c_prompt, c_sample], axis=0)
    nb = bp + bs
    nbp = -(-nb // 16) * 16
    c_act = jnp.pad(jax.nn.silu(c_all), ((0, nbp - nb), (0, 0))).astype(BF16)
    tn_d = min(d, 512)
    tn_f = _pick_tile(dff, 256, LANES)
    tn_o = min(d, 512)
    modp, mods = [], []
    for layer in range(depth):
        mod = (pmatmul(c_act, w_ada, (layer,), n_out=6 * d, tn=tn_d)[:nb] + b_ada[layer]).reshape(nb, 6, d)
        modp.append(mod[:bp])
        mods.append(jnp.swapaxes(mod[bp:], 0, 1))

    new_rw_s, new_rw_shift, new_gd_s, new_gd_conv = [], [], [], []
    h = None
    for layer in range(depth):
        j = layer // 2
        mpl, msl = modp[layer], mods[layer]
        if layer % 2 == 0:
            out, s_new, sh_new = _rwkv_layer(x, mpl, msl, norm1_w[layer], j, P, bp, t,
                                             state_rwkv_shift[j], state_rwkv[j])
            new_rw_s.append(s_new)
            new_rw_shift.append(sh_new)
        else:
            if h is None:
                _, h = resnorm(x, jnp.zeros_like(x), jnp.stack([mpl[:, 2], mpl[:, 1], mpl[:, 0]], axis=1),
                               msl[2], msl[1], msl[0], norm1_w[layer], t)
            out, s_new, cb_new = _gdn_layer(h, j, P, bp, t, state_gdn_conv[j], state_gdn[j])
            new_gd_s.append(s_new)
            new_gd_conv.append(cb_new)
        x, h = resnorm(x, out, jnp.stack([mpl[:, 2], mpl[:, 4], mpl[:, 3]], axis=1),
                       msl[2], msl[4], msl[3], norm2_w[layer], t)
        act = pmatmul(h, ffn_w_gate, (layer,), n_out=dff, tn=tn_f, w2=ffn_w_up, out_dtype=BF16)
        kh = dff // 2
        part = pmatmul(act, ffn_w_down, (layer,), n_out=d, tn=tn_o, k0=0, kk=kh, tm_cap=640)
        ffn = pmatmul(act, ffn_w_down, (layer,), n_out=d, tn=tn_o, k0=kh, kk=kh, add=part, tm_cap=640)
        if layer + 1 < depth:
            nxt_p, nxt_s = modp[layer + 1], mods[layer + 1]
            x, h = resnorm(x, ffn, jnp.stack([mpl[:, 5], nxt_p[:, 1], nxt_p[:, 0]], axis=1),
                           msl[5], nxt_s[1], nxt_s[0], norm1_w[layer + 1], t, emit_h=(layer + 1) % 2 == 1)
        else:
            y_p, y_s = final_norm(x, ffn, mpl[:, 5:6], msl[5], final_norm_w, t)
    grp = lambda lst, i: jnp.stack([e[i] for e in lst])
    return (y_p.reshape(bp, t, d), y_s.reshape(bs, 1, d),
            grp(new_rw_s, 0), grp(new_rw_shift, 0), grp(new_gd_s, 0), grp(new_gd_conv, 0),
            grp(new_rw_s, 1), grp(new_rw_shift, 1), grp(new_gd_s, 1), grp(new_gd_conv, 1))
```

```python
import functools

import jax
import jax.numpy as jnp
from jax import lax
from jax.experimental import pallas as pl
from jax.experimental.pallas import tpu as pltpu

F32, BF16 = jnp.float32, jnp.bfloat16

RMS_EPS = 1e-6
RWKV_LN_EPS = 64e-5
RWKV_N = 64
GDN_DK = 128
GDN_CONV = 4
CHUNK = 64
LANES = 128
SUBLANES = 8
GROUP = 32
VMEM_LIMIT = 56 * 2**20


def _cparams(*sem):
    return pltpu.CompilerParams(dimension_semantics=sem, vmem_limit_bytes=VMEM_LIMIT)


def _pick_tile(n, cap, mult):
    best = None
    for t in range(mult, min(n, cap) + 1, mult):
        if n % t == 0:
            best = t
    return best or n


def _group(n):
    g = GROUP
    while n % g:
        g //= 2
    return g


def _dot(a, b):
    return jnp.dot(a.astype(BF16), b.astype(BF16), preferred_element_type=F32)


def _dot_nt(a, b):
    return lax.dot_general(a.astype(BF16), b.astype(BF16), (((1,), (1,)), ((), ())),
                           preferred_element_type=F32)


def _dot_tn(a, b):
    return lax.dot_general(a.astype(BF16), b.astype(BF16), (((0,), (0,)), ((), ())),
                           preferred_element_type=F32)


def _split3(x):
    hi = x.astype(BF16)
    r1 = x - hi.astype(F32)
    mid = r1.astype(BF16)
    lo = (r1 - mid.astype(F32)).astype(BF16)
    return hi, mid, lo


def _tri(n, transpose=False):
    row = lax.broadcasted_iota(jnp.int32, (n, n), 0)
    col = lax.broadcasted_iota(jnp.int32, (n, n), 1)
    m = (row <= col) if transpose else (row >= col)
    return jnp.where(m, 1.0, 0.0).astype(BF16)


def _row_to_col(row):
    n = row.shape[1]
    eye = lax.broadcasted_iota(jnp.int32, (n, n), 0) == lax.broadcasted_iota(jnp.int32, (n, n), 1)
    return jnp.sum(jnp.where(eye, jnp.broadcast_to(row, (n, n)), 0.0), axis=1, keepdims=True)


def _segsum(x, width, exact):
    l = x.shape[1]
    gi = lax.broadcasted_iota(jnp.int32, (LANES, LANES), 0) // width
    gj = lax.broadcasted_iota(jnp.int32, (LANES, LANES), 1) // width
    bd = jnp.where(gi == gj, 1.0, 0.0).astype(BF16)
    hi = x.astype(BF16)
    lo = (x - hi.astype(F32)).astype(BF16) if exact else None
    outs = []
    for t in range(l // LANES):
        sl = slice(t * LANES, (t + 1) * LANES)
        s = jnp.dot(hi[:, sl], bd, preferred_element_type=F32)
        if exact:
            s = s + jnp.dot(lo[:, sl], bd, preferred_element_type=F32)
        outs.append(s)
    return outs[0] if len(outs) == 1 else jnp.concatenate(outs, axis=1)


def _modnorm(x, nw, scale, shift):
    y = x * lax.rsqrt(jnp.mean(x * x, axis=-1, keepdims=True) + RMS_EPS) * nw
    return y * (1.0 + scale) + shift


A_SLOTS = 3


def _mm_kernel(*refs, n_w, has_add, act, k0, ni, nsteps):
    a_hbm = refs[0]
    w_refs = refs[1:1 + n_w]
    pos = 1 + n_w
    add_ref = refs[pos] if has_add else None
    pos += int(has_add)
    o_ref = refs[pos]
    wbf = refs[pos + 1:pos + 1 + n_w]
    a_buf, a_sem = refs[pos + 1 + n_w:]
    tm, kk = a_buf.shape[1:]
    step = pl.program_id(0) * ni + pl.program_id(1)

    def a_copy(s):
        row = pl.multiple_of((s % ni) * tm, 16)
        slot = s % A_SLOTS
        return pltpu.make_async_copy(a_hbm.at[pl.ds(row, tm), pl.ds(k0, kk)], a_buf.at[slot], a_sem.at[slot])

    if ni == 1:
        @pl.when(step == 0)
        def _():
            a_copy(0).start()
            a_copy(0).wait()
    else:
        @pl.when(step == 0)
        def _():
            for s in range(min(A_SLOTS - 1, nsteps)):
                a_copy(s).start()

        @pl.when(step + (A_SLOTS - 1) < nsteps)
        def _():
            a_copy(step + (A_SLOTS - 1)).start()

    @pl.when(pl.program_id(1) == 0)
    def _():
        for w, s in zip(w_refs, wbf):
            s[...] = w[...].astype(BF16)

    if ni == 1:
        a = a_buf[0]
    else:
        a_copy(step).wait()
        a = a_buf[step % A_SLOTS]
    y = jnp.dot(a, wbf[0][...], preferred_element_type=F32)
    if n_w == 2:
        u = jnp.dot(a, wbf[1][...], preferred_element_type=F32)
        y = y * jax.nn.sigmoid(y) * u
    if has_add:
        y = y + add_ref[...]
    if act == "tanh":
        y = jnp.tanh(y)
    elif act == "sigmoid":
        y = jax.nn.sigmoid(y)
    o_ref[...] = y.astype(o_ref.dtype)


def pmatmul(a, w, widx=(), *, n_out, tn, k0=0, kk=None, w2=None, add=None, act=None,
            out_dtype=F32, tm_cap=1040):
    m = a.shape[0]
    kk = a.shape[1] if kk is None else kk
    kb = k0 // kk
    if kk <= 512:
        tm_cap = 5 * tm_cap // 2
    tm = _pick_tile(m, tm_cap, 16)
    ws = [w] if w2 is None else [w, w2]
    nlead = len(widx)
    grid = (n_out // tn, m // tm)
    in_specs = [pl.BlockSpec(memory_space=pl.ANY)]
    for _ in ws:
        in_specs.append(pl.BlockSpec((None,) * nlead + (kk, tn),
                                     lambda j, i: tuple(widx) + (kb, j)))
    args = [a] + ws
    if add is not None:
        in_specs.append(pl.BlockSpec((tm, tn), lambda j, i: (i, j)))
        args.append(add)
    return pl.pallas_call(
        functools.partial(_mm_kernel, n_w=len(ws), has_add=add is not None, act=act, k0=k0,
                          ni=grid[1], nsteps=grid[0] * grid[1]),
        grid=grid,
        in_specs=in_specs,
        out_specs=pl.BlockSpec((tm, tn), lambda j, i: (i, j)),
        out_shape=jax.ShapeDtypeStruct((m, n_out), out_dtype),
        scratch_shapes=[pltpu.VMEM((kk, tn), BF16) for _ in ws] + [
            pltpu.VMEM((A_SLOTS, tm, kk), BF16), pltpu.SemaphoreType.DMA((A_SLOTS,))],
        compiler_params=_cparams("arbitrary", "arbitrary"),
        name=f"mm_m{m}_k{kk}_n{n_out}" + ("_glu" if w2 is not None else "") + ("_add" if add is not None else ""),
    )(*args)


def _tile_specs(tr, d, np_tiles, tps, nbp):
    tok = lambda: pl.BlockSpec((tr, d), lambda i: (i, 0))
    seq = lambda n: pl.BlockSpec((None, n, d), lambda i: (jnp.minimum(i // tps, nbp - 1), 0, 0))
    samp = lambda: pl.BlockSpec((tr, d), lambda i: (0, 0))
    row = lambda n=1: pl.BlockSpec((n, d), lambda i: (0, 0))
    return tok, seq, samp, row


def _pre_rwkv_kernel(x_ref, mp_ref, scs_ref, shs_ref, nw_ref, mix_ref, prev_s_ref,
                     o0, o1, o2, o3, o4, o5, hlast_ref, hs_ref, carry_ref, *, np_tiles, tps):
    i = pl.program_id(0)
    is_s = i == np_tiles
    shift = jnp.where(is_s, shs_ref[...], mp_ref[0:1, :])
    scale = jnp.where(is_s, scs_ref[...], mp_ref[1:2, :])
    h = _modnorm(x_ref[...], nw_ref[...], scale, shift)
    tr = h.shape[0]

    @pl.when(i % tps == 0)
    def _():
        carry_ref[...] = jnp.zeros_like(carry_ref)

    rowi = lax.broadcasted_iota(jnp.int32, h.shape, 0)
    prev_p = jnp.where(rowi == 0, carry_ref[...], pltpu.roll(h, 1, axis=0))
    prev = jnp.where(is_s, prev_s_ref[...], prev_p)
    xx = prev - h
    for n, o in enumerate((o0, o1, o2, o3, o4, o5)):
        o[...] = (h + xx * mix_ref[n:n + 1, :]).astype(BF16)
    last = h[tr - 1:tr, :]
    carry_ref[...] = last

    @pl.when(jnp.logical_not(is_s))
    def _():
        hlast_ref[...] = last

    @pl.when(is_s)
    def _():
        hs_ref[...] = h


def pre_rwkv(x, modp, scale_s, shift_s, nw, mix, prev_s, t):
    m, d = x.shape
    tr = scale_s.shape[0]
    nbp = modp.shape[0]
    tps = t // tr
    np_tiles = nbp * tps
    tok, seq, samp, row = _tile_specs(tr, d, np_tiles, tps, nbp)
    outs = pl.pallas_call(
        functools.partial(_pre_rwkv_kernel, np_tiles=np_tiles, tps=tps),
        grid=(np_tiles + 1,),
        in_specs=[tok(), seq(2), samp(), samp(), row(), row(6), samp()],
        out_specs=[tok() for _ in range(6)] + [seq(1), samp()],
        out_shape=[jax.ShapeDtypeStruct((m, d), BF16)] * 6 + [jax.ShapeDtypeStruct((nbp, 1, d), F32),
                                                              jax.ShapeDtypeStruct((tr, d), F32)],
        scratch_shapes=[pltpu.VMEM((1, d), F32)],
        compiler_params=_cparams("arbitrary"),
        name="pre_rwkv",
    )(x, modp, scale_s, shift_s, nw.reshape(1, d), mix, prev_s)
    return outs[:6], outs[6], outs[7]


def _resnorm_kernel(x_ref, dl_ref, mp_ref, gs_ref, scs_ref, shs_ref, nw_ref, xo_ref, *h_refs, np_tiles):
    is_s = pl.program_id(0) == np_tiles
    gate = jnp.where(is_s, gs_ref[...], mp_ref[0:1, :])
    xn = x_ref[...] + gate * dl_ref[...]
    xo_ref[...] = xn
    if h_refs:
        scale = jnp.where(is_s, scs_ref[...], mp_ref[1:2, :])
        shift = jnp.where(is_s, shs_ref[...], mp_ref[2:3, :])
        h_refs[0][...] = _modnorm(xn, nw_ref[...], scale, shift).astype(BF16)


def resnorm(x, delta, modp, gate_s, scale_s, shift_s, nw, t, emit_h=True):
    m, d = x.shape
    tr = gate_s.shape[0]
    nbp = modp.shape[0]
    tps = t // tr
    np_tiles = nbp * tps
    tok, seq, samp, row = _tile_specs(tr, d, np_tiles, tps, nbp)
    outs = pl.pallas_call(
        functools.partial(_resnorm_kernel, np_tiles=np_tiles),
        grid=(np_tiles + 1,),
        in_specs=[tok(), tok(), seq(3), samp(), samp(), samp(), row()],
        out_specs=[tok()] + ([tok()] if emit_h else []),
        out_shape=[jax.ShapeDtypeStruct((m, d), F32)] + ([jax.ShapeDtypeStruct((m, d), BF16)] if emit_h else []),
        compiler_params=_cparams("arbitrary"),
        name="resnorm",
    )(x, delta, modp, gate_s, scale_s, shift_s, nw.reshape(1, d))
    return (outs[0], outs[1]) if emit_h else (outs[0], None)


def _final_kernel(x_ref, dl_ref, mp_ref, gs_ref, nw_ref, yp_ref, ys_ref, *, np_tiles):
    is_s = pl.program_id(0) == np_tiles
    gate = jnp.where(is_s, gs_ref[...], mp_ref[0:1, :])
    xn = x_ref[...] + gate * dl_ref[...]
    y = xn * lax.rsqrt(jnp.mean(xn * xn, axis=-1, keepdims=True) + RMS_EPS) * nw_ref[...]

    @pl.when(jnp.logical_not(is_s))
    def _():
        yp_ref[...] = y

    @pl.when(is_s)
    def _():
        ys_ref[...] = y


def final_norm(x, delta, modp, gate_s, nw, t):
    m, d = x.shape
    tr = gate_s.shape[0]
    nbp = modp.shape[0]
    tps = t // tr
    np_tiles = nbp * tps
    tok, seq, samp, row = _tile_specs(tr, d, np_tiles, tps, nbp)
    return pl.pallas_call(
        functools.partial(_final_kernel, np_tiles=np_tiles),
        grid=(np_tiles + 1,),
        in_specs=[tok(), tok(), seq(1), samp(), row()],
        out_specs=[pl.BlockSpec((tr, d), lambda i: (jnp.minimum(i, np_tiles - 1), 0)), samp()],
        out_shape=[jax.ShapeDtypeStruct((np_tiles * tr, d), F32), jax.ShapeDtypeStruct((tr, d), F32)],
        compiler_params=_cparams("arbitrary"),
        name="final_norm",
    )(x, delta, modp, gate_s, nw.reshape(1, d))


def _gdn_conv_kernel(x_ref, w_ref, buf_ref, o_ref, prev_ref, *, np_tiles, tps):
    sec = pl.program_id(0)
    i = pl.program_id(1)
    is_s = i == np_tiles

    x = x_ref[...]

    @pl.when(i % tps == 0)
    def _():
        prev_ref[...] = jnp.zeros_like(prev_ref)

    def tap(k):
        return w_ref[GDN_CONV - 1 - k:GDN_CONV - k, :]

    def finish(y):
        a = y * jax.nn.sigmoid(y)
        qscale = jnp.where(sec == 0, GDN_DK ** -0.5, 1.0)
        nrm = a * (lax.rsqrt(_segsum(a * a, GDN_DK, exact=False) + 1e-6) * qscale)
        o_ref[...] = jnp.where(sec == 2, a, nrm)

    @pl.when(is_s)
    def _():
        y = x * tap(0)
        for k in range(1, GDN_CONV):
            y = y + buf_ref[GDN_CONV - 1 - k] * tap(k)
        finish(y)

    @pl.when(jnp.logical_not(is_s))
    def _():
        prev = prev_ref[...]
        rowi = lax.broadcasted_iota(jnp.int32, x.shape, 0)
        y = x * tap(0)
        for k in range(1, GDN_CONV):
            back = jnp.where(rowi >= k, pltpu.roll(x, k, axis=0), pltpu.roll(prev, k, axis=0))
            y = y + back * tap(k)
        prev_ref[...] = x
        finish(y)


def gdn_conv(main, conv_w, bufs, nbp, t):
    m = main.shape[0]
    d = main.shape[1] // 4
    tr = bufs.shape[1]
    tps = t // tr
    np_tiles = nbp * tps
    return pl.pallas_call(
        functools.partial(_gdn_conv_kernel, np_tiles=np_tiles, tps=tps),
        grid=(3, np_tiles + 1),
        in_specs=[pl.BlockSpec((tr, d), lambda s, i: (i, s)),
                  pl.BlockSpec((GDN_CONV, d), lambda s, i: (0, s)),
                  pl.BlockSpec((GDN_CONV - 1, tr, d), lambda s, i: (0, 0, s))],
        out_specs=pl.BlockSpec((None, tr, d), lambda s, i: (s, i, 0)),
        out_shape=jax.ShapeDtypeStruct((3, m, d), F32),
        scratch_shapes=[pltpu.VMEM((tr, d), F32)],
        compiler_params=_cparams("arbitrary", "arbitrary"),
        name="gdn_conv",
    )(main, conv_w, bufs)


def _inv_kernel(m_ref, n_ref, *, c):
    nblk = c // SUBLANES
    zeros = jnp.zeros((SUBLANES * c, LANES), F32)
    for ib in range(nblk):
        n_ref[ib * SUBLANES * c:(ib + 1) * SUBLANES * c, :] = zeros

        def row(ii, carry, ib=ib):
            base = pl.multiple_of((ib * SUBLANES + ii) * c, c)
            acc = [m_ref[pl.ds(base + SUBLANES * k, SUBLANES), :] for k in range(ib + 1)]
            for j in range((ib + 1) * SUBLANES):
                coef = m_ref[pl.ds(base + j, 1), :]
                for k in range(j // SUBLANES + 1):
                    acc[k] = acc[k] + coef * n_ref[j * c + SUBLANES * k:j * c + SUBLANES * (k + 1), :]
            for k in range(ib + 1):
                n_ref[pl.ds(base + SUBLANES * k, SUBLANES), :] = -acc[k]
            return carry

        lax.fori_loop(0, SUBLANES, row, 0)


def tri_inverse(mats):
    shape = mats.shape
    c = shape[-1]
    u = 1
    for s in shape[:-2]:
        u *= s
    up = -(-u // LANES) * LANES
    flat = jnp.transpose(mats.reshape(u, c, c), (1, 2, 0)).reshape(c * c, u)
    if up != u:
        flat = jnp.pad(flat, ((0, 0), (0, up - u)))
    out = pl.pallas_call(
        functools.partial(_inv_kernel, c=c),
        grid=(up // LANES,),
        in_specs=[pl.BlockSpec((c * c, LANES), lambda g: (0, g))],
        out_specs=pl.BlockSpec((c * c, LANES), lambda g: (0, g)),
        out_shape=jax.ShapeDtypeStruct((c * c, up), F32),
        compiler_params=_cparams("arbitrary"),
        name="tri_inverse",
    )(flat)
    return jnp.transpose(out[:, :u].reshape(c, c, u), (2, 0, 1)).reshape(shape)


def _rwkv_token_prep(r, k, wl, al, w0, a0, k_k, k_a, r_k):
    w_log = -jax.nn.softplus(-(w0 + wl)) - 0.5
    lw = -jnp.exp(w_log)
    a = jax.nn.sigmoid(a0 + al)
    kx = k * k_k
    kap = kx * lax.rsqrt(_segsum(kx * kx, RWKV_N, exact=False) + 1e-6)
    k_mod = k * (1.0 + (a - 1.0) * k_a)
    rk = _segsum(r * k_mod * r_k, RWKV_N, exact=True)
    return lw, kap, k_mod, kap * a, rk


def _rw1_kernel(r_ref, k_ref, v_ref, wl_ref, al_ref, w0_ref, a0_ref, kk_ref, ka_ref, rk_ref,
                kaph_ref, rh_ref, kh_ref, bh_ref, khp_ref, bhp_ref, vb_ref, bonus_ref, pc_ref, mb_ref):
    v = v_ref[...]
    lw, kap, k, b, rk = _rwkv_token_prep(r_ref[...], k_ref[...], wl_ref[...], al_ref[...], w0_ref[...],
                                         a0_ref[...], kk_ref[...], ka_ref[...], rk_ref[...])
    bonus_ref[...] = rk * v
    vb_ref[...] = v.astype(BF16)
    c, lb = lw.shape
    tri = _tri(c)
    hi, mid, lo = _split3(lw)
    p = (jnp.dot(tri, hi, preferred_element_type=F32) + jnp.dot(tri, mid, preferred_element_type=F32)
         + jnp.dot(tri, lo, preferred_element_type=F32))
    pc = p[c - 1:c, :]
    en = jnp.exp(-p)
    ec = jnp.exp(pc - p)
    kaph = kap * jnp.exp(p - lw)
    bh = (b * en).astype(BF16)
    kaph_ref[...] = kaph.astype(BF16)
    rh_ref[...] = (r_ref[...] * jnp.exp(p)).astype(BF16)
    kh_ref[...] = (k * en).astype(BF16)
    bh_ref[...] = bh
    khp_ref[...] = (k * ec).astype(BF16)
    bhp_ref[...] = (b * ec).astype(BF16)
    pc_ref[...] = jnp.exp(pc)
    lane = lax.broadcasted_iota(jnp.int32, (c, LANES), 1)
    m0 = lane < RWKV_N
    rowi = lax.broadcasted_iota(jnp.int32, (2 * c, c), 0) % c
    coli = lax.broadcasted_iota(jnp.int32, (2 * c, c), 1)
    strict = rowi > coli
    for jt in range(lb // LANES):
        kp = kaph[:, jt * LANES:(jt + 1) * LANES]
        bp = bh[:, jt * LANES:(jt + 1) * LANES]
        lhs = jnp.concatenate([jnp.where(m0, kp, 0.0), jnp.where(m0, 0.0, kp)], axis=0)
        g = _dot_nt(lhs, bp)
        g = jnp.where(strict, g, 0.0)
        mb_ref[2 * jt] = g[:c]
        mb_ref[2 * jt + 1] = g[c:]


def _rw3_kernel(kaph_ref, rh_ref, kh_ref, bh_ref, khp_ref, bhp_ref, v_ref, pc_ref, n_ref,
                bonus_ref, g_ref, lnw_ref, lnb_ref, yg_ref, sfin_ref, a_ref):
    ci = pl.program_id(1)
    c = kaph_ref.shape[0]
    npair = a_ref.shape[0]

    @pl.when(ci == 0)
    def _():
        a_ref[...] = jnp.zeros_like(a_ref)

    m0 = lax.broadcasted_iota(jnp.int32, (c, LANES), 1) < RWKV_N
    row = lax.broadcasted_iota(jnp.int32, (c, LANES), 0)
    src = lax.broadcasted_iota(jnp.int32, (c, LANES), 1) % c
    strict = row > src
    incl = row >= src
    rr = lax.broadcasted_iota(jnp.int32, (LANES, LANES), 0)
    cc = lax.broadcasted_iota(jnp.int32, (LANES, LANES), 1)
    blockdiag = (rr < RWKV_N) == (cc < RWKV_N)
    ng = _group(npair)
    rng = range(ng)

    def by_head(t):
        t32 = t.astype(F32)
        return jnp.concatenate([jnp.where(m0, t32, 0.0), jnp.where(m0, 0.0, t32)], axis=0).astype(BF16)

    def group(gi, carry):
        js = [gi * ng + g for g in rng]
        lss = [pl.ds(pl.multiple_of(j * LANES, LANES), LANES) for j in js]
        lhs = [jnp.concatenate([kaph_ref[:, ls], rh_ref[:, ls]], axis=0) for ls in lss]
        vv = [v_ref[:, ls] for ls in lss]
        a0 = [a_ref[j] for j in js]
        x = [_dot(l, a) for l, a in zip(lhs, a0)]
        gk = [_dot_nt(lhs[g], by_head(kh_ref[:, lss[g]])) for g in rng]
        gb = [_dot_nt(lhs[g][c:], by_head(bh_ref[:, lss[g]])) for g in rng]
        vbd = [by_head(vv[g]) for g in rng]
        mk = [jnp.where(strict, gk[g][:c], 0.0) for g in rng]
        lrk = [jnp.where(incl, gk[g][c:], 0.0) for g in rng]
        lrb = [jnp.where(incl, gb[g], 0.0) for g in rng]
        mkv = [_dot(mk[g], vbd[g]) for g in rng]
        lrkv = [_dot(lrk[g], vbd[g]) for g in rng]
        kv = [_dot_tn(khp_ref[:, lss[g]], vv[g]) for g in rng]
        rhs = [x[g][:c] + mkv[g] for g in rng]
        npk = [jnp.concatenate([n_ref[2 * js[g]], n_ref[2 * js[g] + 1]], axis=1) for g in rng]
        nr = [_dot(npk[g], by_head(rhs[g])) for g in rng]
        u = [rhs[g] + nr[g] for g in rng]
        lu = [_dot(lrb[g], by_head(u[g])) for g in rng]
        bu = [_dot_tn(bhp_ref[:, lss[g]], u[g]) for g in rng]
        y = [x[g][c:] + (lrkv[g] - lu[g]) for g in rng]
        mu = [_segsum(y[g], RWKV_N, exact=True) * (1.0 / RWKV_N) for g in range(ng)]
        yc = [y[g] - mu[g] for g in range(ng)]
        var = [_segsum(yc[g] * yc[g], RWKV_N, exact=False) * (1.0 / RWKV_N) for g in range(ng)]
        for g in range(ng):
            ls = lss[g]
            yn = yc[g] * lax.rsqrt(var[g] + RWKV_LN_EPS) * lnw_ref[:, ls] + lnb_ref[:, ls]
            yg_ref[:, ls] = ((yn + bonus_ref[:, ls]) * g_ref[:, ls]).astype(BF16)
            pcc = _row_to_col(pc_ref[:, ls])
            a_ref[js[g]] = pcc * a0[g] + jnp.where(blockdiag, kv[g] - bu[g], 0.0)
        return carry

    lax.fori_loop(0, npair // ng, group, 0)

    @pl.when(ci == pl.num_programs(1) - 1)
    def _():
        sfin_ref[...] = a_ref[...]


def rwkv_prompt_scan(r, k, v, wl, al, g, w0, a0, k_k, k_a, r_k, ln_w, ln_b, bsz, t):
    d = r.shape[1]
    c = CHUNK
    nc = t // c
    h = d // RWKV_N
    lb = min(d, 4096)
    row2 = lambda x: x.reshape(1, d)
    tok2 = lambda: pl.BlockSpec((c, lb), lambda bi, ci, li: (bi * nc + ci, li))
    prm = lambda: pl.BlockSpec((1, lb), lambda bi, ci, li: (0, li))
    tok = lambda: pl.BlockSpec((None, c, lb), lambda bi, ci, li: (bi, ci, li))
    bf = jax.ShapeDtypeStruct((bsz, t, d), BF16)
    kaph, rh, kh, bh, khp, bhp, vb, bonus, pc, mb = pl.pallas_call(
        _rw1_kernel,
        grid=(bsz, nc, d // lb),
        in_specs=[tok2() for _ in range(5)] + [prm() for _ in range(5)],
        out_specs=[tok() for _ in range(8)] + [
            pl.BlockSpec((None, None, 1, lb), lambda bi, ci, li: (bi, ci, 0, li)),
            pl.BlockSpec((None, None, lb // RWKV_N, c, c), lambda bi, ci, li: (bi, ci, li, 0, 0)),
        ],
        out_shape=[bf] * 7 + [jax.ShapeDtypeStruct((bsz, t, d), F32),
                              jax.ShapeDtypeStruct((bsz, nc, 1, d), F32),
                              jax.ShapeDtypeStruct((bsz, nc, h, c, c), F32)],
        compiler_params=_cparams("arbitrary", "arbitrary", "arbitrary"),
        name="rwkv_chunk_prep",
    )(r, k, v, wl, al, row2(w0), row2(a0), row2(k_k), row2(k_a), row2(r_k))
    assert 2 * c == LANES
    nmat = tri_inverse(mb)
    tokd = lambda: pl.BlockSpec((None, c, d), lambda bi, ci: (bi, ci, 0))
    tok2d = lambda: pl.BlockSpec((c, d), lambda bi, ci: (bi * nc + ci, 0))
    prmd = lambda: pl.BlockSpec((1, d), lambda bi, ci: (0, 0))
    yg, sfin = pl.pallas_call(
        _rw3_kernel,
        grid=(bsz, nc),
        in_specs=[tokd() for _ in range(7)] + [
            pl.BlockSpec((None, None, 1, d), lambda bi, ci: (bi, ci, 0, 0)),
            pl.BlockSpec((None, None, h, c, c), lambda bi, ci: (bi, ci, 0, 0, 0)),
            tokd(), tok2d(), prmd(), prmd(),
        ],
        out_specs=[tok2d(), pl.BlockSpec((None, h // 2, LANES, LANES), lambda bi, ci: (bi, 0, 0, 0))],
        out_shape=[jax.ShapeDtypeStruct((bsz * t, d), BF16),
                   jax.ShapeDtypeStruct((bsz, h // 2, LANES, LANES), F32)],
        scratch_shapes=[pltpu.VMEM((h // 2, LANES, LANES), F32)],
        compiler_params=_cparams("arbitrary", "arbitrary"),
        name="rwkv_chunk_scan",
    )(kaph, rh, kh, bh, khp, bhp, vb, pc, nmat, bonus, g, row2(ln_w), row2(ln_b))
    n = RWKV_N
    s_even = sfin[:, :, :n, :n]
    s_odd = sfin[:, :, n:, n:]
    s = jnp.stack([s_even, s_odd], axis=2).reshape(bsz, h, n, n)
    return yg, jnp.swapaxes(s, -1, -2)


def _rws_kernel(s_ref, w_ref, kap_ref, b_ref, k_ref, v_ref, r_ref, so_ref, y_ref):
    s = s_ref[...]
    n = s.shape[-1]
    eye = (lax.broadcasted_iota(jnp.int32, (n, n), 0) == lax.broadcasted_iota(jnp.int32, (n, n), 1))[None]
    sa = -jnp.sum(s * kap_ref[...], axis=2, keepdims=True)
    vcol = jnp.sum(jnp.where(eye, v_ref[...], 0.0), axis=2, keepdims=True)
    sn = s * w_ref[...] + sa * b_ref[...] + vcol * k_ref[...]
    so_ref[...] = sn
    ycol = jnp.sum(sn * r_ref[...], axis=2, keepdims=True)
    y_ref[...] = jnp.sum(jnp.where(eye, ycol, 0.0), axis=1, keepdims=True)


def rwkv_decode_step(s0, w, kap, b, k, v, r):
    bsz, h, n, _ = s0.shape
    vec = lambda: pl.BlockSpec((None, h, 1, n), lambda bi: (bi, 0, 0, 0))
    st = lambda: pl.BlockSpec((None, h, n, n), lambda bi: (bi, 0, 0, 0))
    rows = lambda x: x.reshape(bsz, h, 1, n)
    sn, y = pl.pallas_call(
        _rws_kernel,
        grid=(bsz,),
        in_specs=[st()] + [vec() for _ in range(6)],
        out_specs=[st(), vec()],
        out_shape=[jax.ShapeDtypeStruct(s0.shape, F32), jax.ShapeDtypeStruct((bsz, h, 1, n), F32)],
        compiler_params=_cparams("arbitrary"),
        name="rwkv_decode",
    )(s0, rows(w), rows(kap), rows(b), rows(k), rows(v), rows(r))
    return y.reshape(bsz, h, n), sn


def _gd_head_scalars(g_ref, gcr_s, tri_t):
    hi, mid, lo = _split3(g_ref[...])
    gcr_s[...] = (jnp.dot(hi, tri_t, preferred_element_type=F32)
                  + jnp.dot(mid, tri_t, preferred_element_type=F32)
                  + jnp.dot(lo, tri_t, preferred_element_type=F32))


def _gd1_kernel(k_ref, g_ref, beta_ref, l_ref, gcr_s):
    c = k_ref.shape[0]
    gh = g_ref.shape[0]
    _gd_head_scalars(g_ref, gcr_s, _tri(c, transpose=True))
    row = lax.broadcasted_iota(jnp.int32, (c, c), 0)
    col = lax.broadcasted_iota(jnp.int32, (c, c), 1)
    strict = row > col
    ng = _group(gh)

    def group(gi, carry):
        js = [gi * ng + g for g in range(ng)]
        gr = [gcr_s[pl.ds(j, 1), :] for j in js]
        gcol = [_row_to_col(x) for x in gr]
        bcol = [_row_to_col(beta_ref[pl.ds(j, 1), :]) for j in js]
        kk = [k_ref[:, pl.ds(pl.multiple_of(j * GDN_DK, GDN_DK), GDN_DK)] for j in js]
        gram = [_dot_nt(kk[g] * bcol[g], kk[g]) for g in range(ng)]
        for g in range(ng):
            dec = jnp.exp(jnp.where(strict, gcol[g] - gr[g], 0.0))
            l_ref[js[g]] = jnp.where(strict, gram[g] * dec, 0.0)
        return carry

    lax.fori_loop(0, gh // ng, group, 0)


def _gd3_kernel(q_ref, k_ref, v_ref, g_ref, beta_ref, n_ref, z_ref, nw_ref, og_ref, sfin_ref, s_ref, gcr_s):
    ci = pl.program_id(1)
    c = k_ref.shape[0]
    gh = g_ref.shape[0]

    @pl.when(ci == 0)
    def _():
        s_ref[...] = jnp.zeros_like(s_ref)

    _gd_head_scalars(g_ref, gcr_s, _tri(c, transpose=True))
    row = lax.broadcasted_iota(jnp.int32, (c, c), 0)
    col = lax.broadcasted_iota(jnp.int32, (c, c), 1)
    incl = row >= col
    ng = _group(gh)
    rng = range(ng)

    def group(gi, carry):
        js = [gi * ng + g for g in rng]
        lss = [pl.ds(pl.multiple_of(j * GDN_DK, GDN_DK), GDN_DK) for j in js]
        gr = [gcr_s[pl.ds(j, 1), :] for j in js]
        gcol = [_row_to_col(x) for x in gr]
        bcol = [_row_to_col(beta_ref[pl.ds(j, 1), :]) for j in js]
        glast = [x[:, c - 1:c] for x in gr]
        q = [q_ref[:, ls] for ls in lss]
        kk = [k_ref[:, ls] for ls in lss]
        vv = [v_ref[:, ls] for ls in lss]
        s = [s_ref[j] for j in js]
        eg = [jnp.exp(x) for x in gcol]
        rhs = [jnp.concatenate([vv[g] * bcol[g], kk[g] * bcol[g] * eg[g]], axis=1) for g in rng]
        qk = [_dot_nt(q[g], kk[g]) for g in rng]
        nr = [_dot(n_ref[js[g]], rhs[g]) for g in rng]
        qs = [_dot(q[g] * eg[g], s[g]) for g in rng]
        sol = [rhs[g] + nr[g] for g in rng]
        ws = [_dot(sol[g][:, GDN_DK:], s[g]) for g in rng]
        v_new = [sol[g][:, :GDN_DK] - ws[g] for g in rng]
        attn = [jnp.where(incl, qk[g] * jnp.exp(jnp.where(incl, gcol[g] - gr[g], 0.0)), 0.0) for g in rng]
        av = [_dot(attn[g], v_new[g]) for g in rng]
        kv = [_dot_tn(kk[g] * jnp.exp(glast[g] - gcol[g]), v_new[g]) for g in rng]
        for g in rng:
            o = qs[g] + av[g]
            on = o * lax.rsqrt(jnp.mean(o * o, axis=-1, keepdims=True) + RMS_EPS) * nw_ref[...]
            z = z_ref[:, lss[g]]
            og_ref[:, lss[g]] = (on * (z * jax.nn.sigmoid(z))).astype(BF16)
            s_ref[js[g]] = s[g] * jnp.exp(glast[g]) + kv[g]
        return carry

    lax.fori_loop(0, gh // ng, group, 0)

    @pl.when(ci == pl.num_programs(1) - 1)
    def _():
        sfin_ref[...] = s_ref[...]


def gdn_prompt_scan(qkv, main, g, beta, norm_w, bsz, t):
    d = qkv.shape[2]
    c = CHUNK
    nc = t // c
    gh = d // GDN_DK
    rows = lambda x: jnp.swapaxes(x.reshape(bsz, nc, c, gh), -1, -2)
    g_r, beta_r = rows(g), rows(beta)
    sect = lambda s: pl.BlockSpec((None, c, d), lambda bi, ci: (s, bi * nc + ci, 0))
    hrow = lambda: pl.BlockSpec((None, None, gh, c), lambda bi, ci: (bi, ci, 0, 0))
    mat = lambda: pl.BlockSpec((None, None, gh, c, c), lambda bi, ci: (bi, ci, 0, 0, 0))
    lmat = pl.pallas_call(
        _gd1_kernel,
        grid=(bsz, nc),
        in_specs=[sect(1), hrow(), hrow()],
        out_specs=mat(),
        out_shape=jax.ShapeDtypeStruct((bsz, nc, gh, c, c), F32),
        scratch_shapes=[pltpu.VMEM((gh, c), F32)],
        compiler_params=_cparams("arbitrary", "arbitrary"),
        name="gdn_chunk_prep",
    )(qkv, g_r, beta_r)
    nmat = tri_inverse(lmat)
    og, sfin = pl.pallas_call(
        _gd3_kernel,
        grid=(bsz, nc),
        in_specs=[sect(0), sect(1), sect(2), hrow(), hrow(), mat(),
                  pl.BlockSpec((c, d), lambda bi, ci: (bi * nc + ci, 3)),
                  pl.BlockSpec((1, GDN_DK), lambda bi, ci: (0, 0))],
        out_specs=[pl.BlockSpec((c, d), lambda bi, ci: (bi * nc + ci, 0)),
                   pl.BlockSpec((None, gh, GDN_DK, GDN_DK), lambda bi, ci: (bi, 0, 0, 0))],
        out_shape=[jax.ShapeDtypeStruct((bsz * t, d), BF16),
                   jax.ShapeDtypeStruct((bsz, gh, GDN_DK, GDN_DK), F32)],
        scratch_shapes=[pltpu.VMEM((gh, GDN_DK, GDN_DK), F32), pltpu.VMEM((gh, c), F32)],
        compiler_params=_cparams("arbitrary", "arbitrary"),
        name="gdn_chunk_scan",
    )(qkv, qkv, qkv, g_r, beta_r, nmat, main, norm_w.reshape(1, GDN_DK))
    return og, sfin


def _gds_kernel(s_ref, q_ref, k_ref, v_ref, beta_ref, eg_ref, so_ref, o_ref):
    s = s_ref[...]
    n = s.shape[-1]
    eye = (lax.broadcasted_iota(jnp.int32, (n, n), 0) == lax.broadcasted_iota(jnp.int32, (n, n), 1))[None]
    kcol = jnp.sum(jnp.where(eye, k_ref[...], 0.0), axis=2, keepdims=True)
    qcol = jnp.sum(jnp.where(eye, q_ref[...], 0.0), axis=2, keepdims=True)
    eg = eg_ref[...]
    ks = jnp.sum(kcol * s, axis=1, keepdims=True)
    v_new = beta_ref[...] * (v_ref[...] - eg * ks)
    sn = s * eg + kcol * v_new
    so_ref[...] = sn
    o_ref[...] = jnp.sum(qcol * sn, axis=1, keepdims=True)


def gdn_decode_step(s0, q, k, v, beta, g):
    bsz, gh, dk, _ = s0.shape
    d = gh * dk
    rows = lambda x: x.reshape(bsz, gh, 1, dk)
    wide = lambda x: jnp.broadcast_to(x[:, :, None, None], (bsz, gh, 1, dk))
    vec = lambda: pl.BlockSpec((None, gh, 1, dk), lambda bi: (bi, 0, 0, 0))
    st = lambda: pl.BlockSpec((None, gh, dk, dk), lambda bi: (bi, 0, 0, 0))
    sn, o = pl.pallas_call(
        _gds_kernel,
        grid=(bsz,),
        in_specs=[st()] + [vec() for _ in range(5)],
        out_specs=[st(), vec()],
        out_shape=[jax.ShapeDtypeStruct(s0.shape, F32), jax.ShapeDtypeStruct((bsz, gh, 1, dk), F32)],
        compiler_params=_cparams("arbitrary"),
        name="gdn_decode",
    )(s0, rows(q), rows(k), rows(v), wide(beta), wide(jnp.exp(g)))
    return o.reshape(bsz, d), sn


def _rwkv_layer(x, modp, mods, nw, j, P, bp, t, shift_prev, s0):
    m, d = x.shape
    mp = bp * t
    bs = m - mp
    nh = d // RWKV_N
    xs, hlast, h_s = pre_rwkv(x, modp[:, 0:2], mods[1], mods[0], nw, P['rw_mix'][j], shift_prev, t)
    tn = min(d, 512)
    mm = lambda a, w, widx, n, tnn, **kw: pmatmul(a, w, widx, n_out=n, tn=tnn, **kw)
    r = mm(xs[0], P['rw_w_rkv'], (j, 0), d, tn)
    k = mm(xs[1], P['rw_w_rkv'], (j, 1), d, tn)
    v = mm(xs[2], P['rw_w_rkv'], (j, 2), d, tn)
    lora = P['rw_w1'].shape[-1]
    wl = mm(mm(xs[3], P['rw_w1'], (j,), lora, lora, act="tanh", out_dtype=BF16), P['rw_w2'], (j,), d, tn)
    al = mm(mm(xs[4], P['rw_a1'], (j,), lora, lora, out_dtype=BF16), P['rw_a2'], (j,), d, tn)
    gl = P['rw_g1'].shape[-1]
    glp = -(-gl // LANES) * LANES
    g1 = jnp.pad(P['rw_g1'][j], ((0, 0), (0, glp - gl)))
    g2 = jnp.pad(P['rw_g2'][j], ((0, glp - gl), (0, 0)))
    g = mm(mm(xs[5], g1, (), glp, glp, act="sigmoid", out_dtype=BF16), g2, (), d, tn)
    prm = [P[n][j] for n in ('rw_w0', 'rw_a0', 'rw_k_k', 'rw_k_a')] + [P['rw_r_k'][j].reshape(d)]
    ln_w, ln_b = P['rw_ln_w'][j], P['rw_ln_b'][j]
    yg_p, sp = rwkv_prompt_scan(r, k, v, wl, al, g, *prm, ln_w, ln_b, bp, t)

    rs, ks, vs, wls, als, gs = (a[mp:] for a in (r, k, v, wl, al, g))
    w_log = -jax.nn.softplus(-(prm[0] + wls)) - 0.5
    dec = jnp.exp(-jnp.exp(w_log))
    a = jax.nn.sigmoid(prm[1] + als)
    hv = lambda z: z.reshape(bs, nh, RWKV_N)
    kx = hv(ks * prm[2])
    kap = kx * lax.rsqrt(jnp.sum(kx * kx, axis=-1, keepdims=True) + 1e-6)
    k_mod = ks * (1.0 + (a - 1.0) * prm[3])
    ys, ss = rwkv_decode_step(s0, hv(dec), kap, kap * hv(a), hv(k_mod), hv(vs), hv(rs))
    mu = jnp.mean(ys, axis=-1, keepdims=True)
    var = jnp.mean(jnp.square(ys - mu), axis=-1, keepdims=True)
    yn = ((ys - mu) * lax.rsqrt(var + RWKV_LN_EPS)).reshape(bs, d) * ln_w + ln_b
    bonus = jnp.sum(hv(rs) * hv(k_mod) * prm[4].reshape(nh, RWKV_N), axis=-1, keepdims=True) * hv(vs)
    yg_s = ((yn + bonus.reshape(bs, d)) * gs).astype(BF16)

    out = mm(jnp.concatenate([yg_p, yg_s], axis=0), P['rw_w_o'], (j,), d, tn)
    return out, (sp, ss), (hlast[:, 0], h_s)


def _gdn_layer(h, j, P, bp, t, conv_prev, s0):
    m, d = h.shape
    mp = bp * t
    bs = m - mp
    gh = d // GDN_DK
    cdim = 3 * d
    tn = min(d, 512)
    w_in = P['gd_w_in']
    main = pmatmul(h, w_in, (j,), n_out=4 * d, tn=tn)
    w_tail = jnp.pad(w_in[j][:, 4 * d:], ((0, 0), (0, LANES - 2 * gh)))
    tail = pmatmul(h, w_tail, (), n_out=LANES, tn=LANES)
    qkv = gdn_conv(main, P['gd_conv_w'][j], jnp.swapaxes(conv_prev, 0, 1), bp, t)
    beta = jax.nn.sigmoid(tail[:, :gh])
    g = -jnp.exp(P['gd_A_log'][j]) * jax.nn.softplus(tail[:, gh:2 * gh] + P['gd_dt_bias'][j])
    norm_w = P['gd_norm_w'][j]
    og_p, sp = gdn_prompt_scan(qkv, main, g[:mp].reshape(bp, t, gh), beta[:mp].reshape(bp, t, gh), norm_w, bp, t)

    o_s, ss = gdn_decode_step(s0, qkv[0, mp:], qkv[1, mp:], qkv[2, mp:], beta[mp:], g[mp:])
    oh = o_s.reshape(bs, gh, GDN_DK)
    oh = oh * lax.rsqrt(jnp.mean(oh * oh, axis=-1, keepdims=True) + RMS_EPS) * norm_w
    og_s = (oh.reshape(bs, d) * jax.nn.silu(main[mp:, cdim:])).astype(BF16)

    out = pmatmul(jnp.concatenate([og_p, og_s], axis=0), P['gd_w_out'], (j,), n_out=d, tn=tn)
    conv_p = jnp.stack([main[(b + 1) * t - (GDN_CONV - 1):(b + 1) * t, :cdim] for b in range(bp)])
    conv_s = jnp.concatenate([conv_prev[:, 1:], main[mp:, None, :cdim]], axis=1)
    return out, (sp, ss), (conv_p, conv_s)


def kernel(x_prompt, x_sample, state_rwkv, state_rwkv_shift, state_gdn, state_gdn_conv, c_prompt, c_sample,
           w_ada, b_ada, norm1_w, norm2_w, rw_mix, rw_w_rkv, rw_w0, rw_w1, rw_w2, rw_a0, rw_a1, rw_a2,
           rw_g1, rw_g2, rw_k_k, rw_k_a, rw_r_k, rw_ln_w, rw_ln_b, rw_w_o, gd_w_in, gd_conv_w, gd_A_log,
           gd_dt_bias, gd_norm_w, gd_w_out, ffn_w_gate, ffn_w_up, ffn_w_down, final_norm_w):
    P = {
        'rw_mix': rw_mix, 'rw_w_rkv': rw_w_rkv, 'rw_w0': rw_w0, 'rw_w1': rw_w1, 'rw_w2': rw_w2,
        'rw_a0': rw_a0, 'rw_a1': rw_a1, 'rw_a2': rw_a2, 'rw_g1': rw_g1, 'rw_g2': rw_g2,
        'rw_k_k': rw_k_k, 'rw_k_a': rw_k_a, 'rw_r_k': rw_r_k, 'rw_ln_w': rw_ln_w, 'rw_ln_b': rw_ln_b,
        'rw_w_o': rw_w_o, 'gd_w_in': gd_w_in, 'gd_conv_w': gd_conv_w, 'gd_A_log': gd_A_log,
        'gd_dt_bias': gd_dt_bias, 'gd_norm_w': gd_norm_w, 'gd_w_out': gd_w_out,
    }
    bp, t, d = x_prompt.shape
    bs = x_sample.shape[0]
    depth = w_ada.shape[0]
    dff = ffn_w_gate.shape[-1]
    mp = bp * t
    assert x_sample.shape[1] == 1 and t % CHUNK == 0 and t % bs == 0 and bs % 16 == 0 and d % (2 * LANES) == 0

    x = jnp.concatenate([x_prompt.reshape(mp, d), x_sample.reshape(bs, d)], axis=0)
    c_all = jnp.concatenate([c_prompt, c_sample], axis=0)
    nb = bp + bs
    nbp = -(-nb // 16) * 16
    c_act = jnp.pad(jax.nn.silu(c_all), ((0, nbp - nb), (0, 0))).astype(BF16)
    tn_d = min(d, 512)
    tn_f = _pick_tile(dff, 256, LANES)
    tn_o = min(d, 512)
    modp, mods = [], []
    for layer in range(depth):
        mod = (pmatmul(c_act, w_ada, (layer,), n_out=6 * d, tn=tn_d)[:nb] + b_ada[layer]).reshape(nb, 6, d)
        modp.append(mod[:bp])
        mods.append(jnp.swapaxes(mod[bp:], 0, 1))

    new_rw_s, new_rw_shift, new_gd_s, new_gd_conv = [], [], [], []
    h = None
    for layer in range(depth):
        j = layer // 2
        mpl, msl = modp[layer], mods[layer]
        if layer % 2 == 0:
            out, s_new, sh_new = _rwkv_layer(x, mpl, msl, norm1_w[layer], j, P, bp, t,
                                             state_rwkv_shift[j], state_rwkv[j])
            new_rw_s.append(s_new)
            new_rw_shift.append(sh_new)
        else:
            if h is None:
                _, h = resnorm(x, jnp.zeros_like(x), jnp.stack([mpl[:, 2], mpl[:, 1], mpl[:, 0]], axis=1),
                               msl[2], msl[1], msl[0], norm1_w[layer], t)
            out, s_new, cb_new = _gdn_layer(h, j, P, bp, t, state_gdn_conv[j], state_gdn[j])
            new_gd_s.append(s_new)
            new_gd_conv.append(cb_new)
        x, h = resnorm(x, out, jnp.stack([mpl[:, 2], mpl[:, 4], mpl[:, 3]], axis=1),
                       msl[2], msl[4], msl[3], norm2_w[layer], t)
        act = pmatmul(h, ffn_w_gate, (layer,), n_out=dff, tn=tn_f, w2=ffn_w_up, out_dtype=BF16)
        kh = dff // 2
        part = pmatmul(act, ffn_w_down, (layer,), n_out=d, tn=tn_o, k0=0, kk=kh, tm_cap=640)
        ffn = pmatmul(act, ffn_w_down, (layer,), n_out=d, tn=tn_o, k0=kh, kk=kh, add=part, tm_cap=640)
        if layer + 1 < depth:
            nxt_p, nxt_s = modp[layer + 1], mods[layer + 1]
            x, h = resnorm(x, ffn, jnp.stack([mpl[:, 5], nxt_p[:, 1], nxt_p[:, 0]], axis=1),
                           msl[5], nxt_s[1], nxt_s[0], norm1_w[layer + 1], t, emit_h=(layer + 1) % 2 == 1)
        else:
            y_p, y_s = final_norm(x, ffn, mpl[:, 5:6], msl[5], final_norm_w, t)
    grp = lambda lst, i: jnp.stack([e[i] for e in lst])
    return (y_p.reshape(bp, t, d), y_s.reshape(bs, 1, d),
            grp(new_rw_s, 0), grp(new_rw_shift, 0), grp(new_gd_s, 0), grp(new_gd_conv, 0),
            grp(new_rw_s, 1), grp(new_rw_shift, 1), grp(new_gd_s, 1), grp(new_gd_conv, 1))
```
